```python
import jax
import jax.numpy as jnp
from jax import lax
import numpy as np


D_MODEL = 1024
BATCH = 2
SEQ = 8192
DEPTH = 1

CHUNK = 64
D_MIX = D_MODEL
RET_WIDTH = D_MIX // 2
RET_HEADS = 4
RET_DK = RET_WIDTH // RET_HEADS
RET_DV = RET_WIDTH // RET_HEADS
ROPE_BASE = 10000.0
GLA_WIDTH = D_MIX - RET_WIDTH
GLA_HEADS = 4
GLA_KEY_WIDTH = GLA_WIDTH // 2
GLA_DK = GLA_KEY_WIDTH // GLA_HEADS
GLA_DV = GLA_WIDTH // GLA_HEADS
GLA_GATE_RANK = 16
GLA_GATE_TAU = 16.0
D_FF = 4 * D_MODEL
LN_EPS = 1e-5
DEEPNORM_ALPHA = (2.0 * DEPTH) ** 0.25
DEEPNORM_BETA = (8.0 * DEPTH) ** -0.25
IN_SPLITS = (RET_WIDTH, RET_WIDTH, RET_WIDTH, RET_WIDTH,
             GLA_KEY_WIDTH, GLA_KEY_WIDTH, GLA_WIDTH, GLA_WIDTH, GLA_GATE_RANK)
D_IN_PROJ = sum(IN_SPLITS)
VALUE_SLOTS = (2, 6)

kernel_name = 'hybrid_retention_gla_deepnorm_adaln'


def _layer_norm(x, w=None, b=None):
    xf = x.astype(jnp.float32)
    mu = jnp.mean(xf, axis=-1, keepdims=True)
    var = jnp.mean(jnp.square(xf - mu), axis=-1, keepdims=True)
    y = (xf - mu) * lax.rsqrt(var + LN_EPS)
    if w is not None:
        y = y * w.astype(jnp.float32) + b.astype(jnp.float32)
    return y.astype(x.dtype)


def _head_norm(o, w, center):
    if center:
        o = o - jnp.mean(o, axis=-1, keepdims=True)
    o = o * lax.rsqrt(jnp.mean(jnp.square(o), axis=-1, keepdims=True) + LN_EPS)
    B, S, H, d = o.shape
    return o.reshape(B, S, H * d) * w.astype(jnp.float32)


def _rotary(x, pos):
    half = x.shape[-1] // 2
    inv = 1.0 / (ROPE_BASE ** jnp.linspace(0.0, 1.0, half, dtype=jnp.float32))
    ang = pos[:, None] * inv[None, :]
    cos = jnp.cos(ang)[:, None, :]
    sin = jnp.sin(ang)[:, None, :]
    x1, x2 = x[..., :half], x[..., half:]
    return jnp.concatenate([x1 * cos - x2 * sin, x2 * cos + x1 * sin], axis=-1)


def _retention(q, k, v):
    B, S, H, dk = q.shape
    dv = v.shape[-1]
    N = S // CHUNK
    log_gamma = jnp.log(1.0 - 2.0 ** (-5.0 - jnp.arange(H, dtype=jnp.float32)))
    pos = jnp.arange(S, dtype=jnp.float32)
    q = _rotary(q, pos) * (dk ** -0.5)
    k = _rotary(k, pos)
    qc = q.reshape(B, N, CHUNK, H, dk)
    kc = k.reshape(B, N, CHUNK, H, dk)
    vc = v.reshape(B, N, CHUNK, H, dv)
    idx = jnp.arange(CHUNK, dtype=jnp.float32)
    dist = jnp.abs(idx[:, None] - idx[None, :])
    intra_decay = jnp.exp(log_gamma[:, None, None] * dist)
    scores = jnp.einsum('bnchd,bnshd->bnhcs', qc, kc) * intra_decay
    o = jnp.einsum('bnhcs,bnshe->bnche', scores, vc)
    k_decay = jnp.exp(log_gamma[None, :] * (CHUNK - 1.0 - idx)[:, None])
    q_decay = jnp.exp(log_gamma[None, :] * (idx + 1.0)[:, None])
    chunk_decay = jnp.exp(log_gamma * CHUNK)[None, :, None, None]
    U = jnp.einsum('bnshd,bnshe->nbhde', kc * k_decay[:, :, None], vc)

    def step(R, U_i):
        return chunk_decay * R + U_i, R

    _, R_prev = lax.scan(step, jnp.zeros((B, H, dk, dv), jnp.float32), U)
    o = o + jnp.einsum('bnchd,nbhde->bnche', qc * q_decay[:, :, None], R_prev)
    return o.reshape(B, S, H, dv)


def _gla(q, k, v, log_a):
    B, S, H, dk = q.shape
    dv = v.shape[-1]
    N = S // CHUNK
    q = q * (dk ** -0.5)

    def to_chunks(t):
        return t.reshape(B, N, CHUNK, H, t.shape[-1]).transpose(1, 0, 2, 3, 4)

    def step(state, inp):
        qi, ki, vi, lai = inp
        b = jnp.cumsum(lai, axis=1)
        decay = jnp.exp(-jnp.abs(b[:, :, None] - b[:, None, :]))
        A = jnp.einsum('bthd,bshd,btshd->bhts', qi, ki, decay)
        o = jnp.einsum('bhts,bshe->bthe', A, vi)
        o = o + jnp.einsum('bthd,bhde->bthe', qi * jnp.exp(b), state)
        b_last = b[:, -1]
        new_state = jnp.exp(b_last)[..., None] * state + jnp.einsum(
            'bshd,bshe->bhde', ki * jnp.exp(b_last[:, None] - b), vi)
        return new_state, o

    _, o = lax.scan(step, jnp.zeros((B, H, dk, dv), jnp.float32),
                    (to_chunks(q), to_chunks(k), to_chunks(v), to_chunks(log_a)))
    return o.transpose(1, 0, 2, 3, 4).reshape(B, S, H, dv)


def _token_mixer(u, w_in, ret_norm_w, gla_gate_w, gla_gate_b, gla_norm_w, w_out):
    B, S, _ = u.shape
    proj = (u @ w_in).astype(jnp.float32)
    cuts = [int(v) for v in np.cumsum(IN_SPLITS)[:-1]]
    rq, rk, rv, rg, gq, gk, gv, gg, glr = jnp.split(proj, cuts, axis=-1)
    ro = _retention(rq.reshape(B, S, RET_HEADS, RET_DK),
                    rk.reshape(B, S, RET_HEADS, RET_DK),
                    rv.reshape(B, S, RET_HEADS, RET_DV))
    ro = _head_norm(ro, ret_norm_w, center=True) * jax.nn.silu(rg)
    gate_logit = glr @ gla_gate_w.astype(jnp.float32) + gla_gate_b.astype(jnp.float32)
    log_a = jax.nn.log_sigmoid(gate_logit) / GLA_GATE_TAU
    go = _gla(gq.reshape(B, S, GLA_HEADS, GLA_DK),
              gk.reshape(B, S, GLA_HEADS, GLA_DK),
              gv.reshape(B, S, GLA_HEADS, GLA_DV),
              log_a.reshape(B, S, GLA_HEADS, GLA_DK))
    go = _head_norm(go, gla_norm_w, center=False) * jax.nn.silu(gg)
    mixed = jnp.concatenate([ro, go], axis=-1).astype(u.dtype)
    return mixed @ w_out


def setup_inputs(seed: int = 0) -> dict:
    key = jax.random.key(seed)
    ks = jax.random.split(key, 16)
    nrm = jax.random.normal
    offs = np.concatenate([[0], np.cumsum(IN_SPLITS)])
    col_scale = np.ones((D_IN_PROJ,), np.float32)
    for slot in VALUE_SLOTS:
        col_scale[int(offs[slot]):int(offs[slot + 1])] = DEEPNORM_BETA
    return {
        'x': nrm(ks[0], (BATCH, SEQ, D_MODEL), jnp.float32),
        'c': nrm(ks[1], (BATCH, D_MODEL), jnp.float32),
        'w_ada': nrm(ks[2], (DEPTH, D_MODEL, 6 * D_MODEL), jnp.float32) * (0.5 * D_MODEL ** -0.5),
        'b_ada': 0.02 * nrm(ks[3], (DEPTH, 6 * D_MODEL), jnp.float32),
        'w_in': nrm(ks[4], (DEPTH, D_MODEL, D_IN_PROJ), jnp.float32) * (D_MODEL ** -0.5) * jnp.asarray(col_scale),
        'ret_norm_w': 1.0 + 0.02 * nrm(ks[5], (DEPTH, RET_WIDTH), jnp.float32),
        'gla_gate_w': nrm(ks[6], (DEPTH, GLA_GATE_RANK, GLA_KEY_WIDTH), jnp.float32) * (GLA_GATE_RANK ** -0.5),
        'gla_gate_b': 0.1 * nrm(ks[7], (DEPTH, GLA_KEY_WIDTH), jnp.float32),
        'gla_norm_w': 1.0 + 0.02 * nrm(ks[8], (DEPTH, GLA_WIDTH), jnp.float32),
        'w_out': nrm(ks[9], (DEPTH, D_MIX, D_MODEL), jnp.float32) * (D_MIX ** -0.5) * DEEPNORM_BETA,
        'ln1_w': 1.0 + 0.02 * nrm(ks[10], (DEPTH, D_MODEL), jnp.float32),
        'ln1_b': 0.02 * nrm(ks[11], (DEPTH, D_MODEL), jnp.float32),
        'w_ff1': nrm(ks[12], (DEPTH, D_MODEL, D_FF), jnp.float32) * (D_MODEL ** -0.5) * DEEPNORM_BETA,
        'w_ff2': nrm(ks[13], (DEPTH, D_FF, D_MODEL), jnp.float32) * (D_FF ** -0.5) * DEEPNORM_BETA,
        'ln2_w': 1.0 + 0.02 * nrm(ks[14], (DEPTH, D_MODEL), jnp.float32),
        'ln2_b': 0.02 * nrm(ks[15], (DEPTH, D_MODEL), jnp.float32),
    }


def reference(x, c, w_ada, b_ada, w_in, ret_norm_w, gla_gate_w, gla_gate_b, gla_norm_w,
              w_out, ln1_w, ln1_b, w_ff1, w_ff2, ln2_w, ln2_b):
    for l in range(DEPTH):
        mod = jax.nn.silu(c) @ w_ada[l] + b_ada[l]
        shift1, scale1, gate1, shift2, scale2, gate2 = jnp.split(mod, 6, axis=-1)
        u = _layer_norm(x) * (1.0 + scale1[:, None, :]) + shift1[:, None, :]
        m = _token_mixer(u, w_in[l], ret_norm_w[l], gla_gate_w[l], gla_gate_b[l],
                         gla_norm_w[l], w_out[l])
        x = _layer_norm(DEEPNORM_ALPHA * x + gate1[:, None, :] * m, ln1_w[l], ln1_b[l])
        u2 = _layer_norm(x) * (1.0 + scale2[:, None, :]) + shift2[:, None, :]
        f = jnp.square(jax.nn.relu(u2 @ w_ff1[l])) @ w_ff2[l]
        x = _layer_norm(DEEPNORM_ALPHA * x + gate2[:, None, :] * f, ln2_w[l], ln2_b[l])
    return x
```

```python
import functools

import numpy as np
import jax
import jax.numpy as jnp
from jax import lax
from jax.experimental import pallas as pl
from jax.experimental.pallas import tpu as pltpu

CHUNK = 64
RET_HEADS = 4
RET_D = 128
RET_WIDTH = RET_HEADS * RET_D
GLA_HEADS = 4
GLA_DK = 64
GLA_DV = 128
GLA_KEY_WIDTH = GLA_HEADS * GLA_DK
GLA_WIDTH = GLA_HEADS * GLA_DV
GATE_RANK = 16
GATE_TAU = 16.0
ROPE_BASE = 10000.0
LN_EPS = 1e-5
SLAB = 4 * RET_WIDTH + 2 * GLA_KEY_WIDTH + 2 * GLA_WIDTH
OFF_RQ, OFF_RK, OFF_RV, OFF_RG = 0, 512, 1024, 1536
OFF_GQ, OFF_GK, OFF_GV, OFF_GG = 2048, 2304, 2560, 3072
N_LEVELS = 6
VMEM_LIMIT = 56 * 1024 * 1024

BF16 = jnp.bfloat16
F32 = jnp.float32


def _dot(a, b):
    return jnp.dot(a, b, preferred_element_type=F32)


def _dot_nt(a, b):
    return lax.dot_general(a, b, (((1,), (1,)), ((), ())), preferred_element_type=F32)


def _dot_tn(a, b):
    return lax.dot_general(a, b, (((0,), (0,)), ((), ())), preferred_element_type=F32)


def _ln(x):
    mu = jnp.mean(x, axis=-1, keepdims=True)
    xc = x - mu
    var = jnp.mean(xc * xc, axis=-1, keepdims=True)
    return xc * lax.rsqrt(var + LN_EPS)


def _silu(x):
    return x * (1.0 / (1.0 + jnp.exp(-x)))


def _adaln_kernel(c_ref, w_ref, b_ref, o_ref):
    c = c_ref[...]
    o_ref[...] = _dot(_silu(c).astype(BF16), w_ref[...].astype(BF16)) + b_ref[...]


def _adaln(c, w, b):
    bsz, d = c.shape
    n = w.shape[1]
    tn = 1536
    return pl.pallas_call(
        _adaln_kernel,
        grid=(n // tn,),
        in_specs=[pl.BlockSpec((bsz, d), lambda j: (0, 0)),
                  pl.BlockSpec((d, tn), lambda j: (0, j)),
                  pl.BlockSpec((1, tn), lambda j: (0, j))],
        out_specs=pl.BlockSpec((bsz, tn), lambda j: (0, j)),
        out_shape=jax.ShapeDtypeStruct((bsz, n), F32),
        compiler_params=pltpu.CompilerParams(vmem_limit_bytes=VMEM_LIMIT),
        name="adaln",
    )(c, w, b.reshape(1, n))


def _inproj_kernel(x_ref, mod_ref, cos_ref, sin_ref, w_ref, wg_ref, gw_ref, gb_ref,
                   slab_ref, la_ref):
    x = x_ref[0]
    shift1 = mod_ref[0, 0:1, :]
    scale1 = mod_ref[0, 1:2, :]
    ub = (_ln(x) * (1.0 + scale1) + shift1).astype(BF16)
    cos = cos_ref[...]
    sin = sin_ref[...]
    group = 512
    for j in range(SLAB // group):
        p = _dot(ub, w_ref[:, j * group:(j + 1) * group])
        if j * group < OFF_RV:
            for h in range(RET_HEADS):
                xh = p[:, h * RET_D:(h + 1) * RET_D]
                rot = xh * cos + pltpu.roll(xh, RET_D // 2, 1) * sin
                slab_ref[0, :, j * group + h * RET_D:j * group + (h + 1) * RET_D] = rot.astype(BF16)
        else:
            slab_ref[0, :, j * group:(j + 1) * group] = p.astype(BF16)
    glr = _dot(ub, wg_ref[...])
    logit = _dot(glr.astype(BF16), gw_ref[...]) + gb_ref[...]
    log_sig = jnp.minimum(logit, 0.0) - jnp.log1p(jnp.exp(-jnp.abs(logit)))
    la_ref[0] = log_sig * (1.0 / GATE_TAU)


def _inproj(x, mod, cos_t, sin_t, w_slab, w_glr, gate_w, gate_b, tb):
    bsz, s, d = x.shape
    nb = s // tb
    const2 = lambda b, i: (0, 0)
    return pl.pallas_call(
        _inproj_kernel,
        grid=(bsz, nb),
        in_specs=[pl.BlockSpec((1, tb, d), lambda b, i: (b, i, 0)),
                  pl.BlockSpec((1, 6, d), lambda b, i: (b, 0, 0)),
                  pl.BlockSpec((tb, RET_D), lambda b, i: (i, 0)),
                  pl.BlockSpec((tb, RET_D), lambda b, i: (i, 0)),
                  pl.BlockSpec((d, SLAB), const2),
                  pl.BlockSpec((d, GATE_RANK), const2),
                  pl.BlockSpec((GATE_RANK, GLA_KEY_WIDTH), const2),
                  pl.BlockSpec((1, GLA_KEY_WIDTH), const2)],
        out_specs=[pl.BlockSpec((1, tb, SLAB), lambda b, i: (b, i, 0)),
                   pl.BlockSpec((1, tb, GLA_KEY_WIDTH), lambda b, i: (b, i, 0))],
        out_shape=[jax.ShapeDtypeStruct((bsz, s, SLAB), BF16),
                   jax.ShapeDtypeStruct((bsz, s, GLA_KEY_WIDTH), F32)],
        compiler_params=pltpu.CompilerParams(
            dimension_semantics=("parallel", "parallel"), vmem_limit_bytes=VMEM_LIMIT),
        name="inproj",
    )(x, mod, cos_t, sin_t, w_slab, w_glr, gate_w, gate_b)


def _rotary_tables(s):
    half = RET_D // 2
    inv = 1.0 / (ROPE_BASE ** (np.arange(half, dtype=np.float64) / (half - 1)))
    ang = np.arange(s, dtype=np.float64)[:, None] * inv[None, :]
    cos = np.concatenate([np.cos(ang), np.cos(ang)], axis=1)
    sin = np.concatenate([-np.sin(ang), np.sin(ang)], axis=1)
    return cos.astype(np.float32), sin.astype(np.float32)


def _retention_tables(tb):
    log_gamma = np.log(1.0 - 2.0 ** (-5.0 - np.arange(RET_HEADS, dtype=np.float64)))
    idx = np.arange(tb)
    dist = np.abs(idx[:, None] - idx[None, :])
    visible = (idx[None, :] // CHUNK) <= (idx[:, None] // CHUNK)
    scale = RET_D ** -0.5
    dmask = np.exp(log_gamma[:, None, None] * dist[None]) * visible[None] * scale
    qdec = np.exp(log_gamma[:, None] * (idx[None, :] + 1.0)) * scale
    kdec = np.exp(log_gamma[:, None] * (tb - 1.0 - idx[None, :]))
    qdec = np.broadcast_to(qdec[:, :, None], (RET_HEADS, tb, RET_D))
    kdec = np.broadcast_to(kdec[:, :, None], (RET_HEADS, tb, RET_D))
    block_decay = [float(np.exp(lg * tb)) for lg in log_gamma]
    return (dmask.astype(np.float32), np.ascontiguousarray(qdec, np.float32),
            np.ascontiguousarray(kdec, np.float32), block_decay)


def _gla_tables(tb):
    n_chunks = tb // CHUNK
    t = np.arange(CHUNK)
    groups = [(t[None, :] <= t[:, None]), (t[None, :] > t[:, None])]
    masks = []
    for lvl in range(N_LEVELS):
        c = CHUNK >> (lvl + 1)
        start = (t // (2 * c)) * (2 * c)
        split = start + c - 1
        right = t > split
        r = t[None, :]
        m_right = (r > split[:, None]) & (r <= t[:, None])
        m_left = (r > t[:, None]) & (r <= split[:, None])
        groups.append(np.where(right[:, None], m_right, m_left))
        same_block = start[:, None] == start[None, :]
        masks.append(same_block & (right[:, None] != right[None, :]))
    masks.append(np.eye(CHUNK, dtype=bool))
    summ = np.zeros((len(groups), tb, tb), np.float32)
    for g, m in enumerate(groups):
        for ch in range(n_chunks):
            sl = slice(ch * CHUNK, (ch + 1) * CHUNK)
            summ[g, sl, sl] = m
    summ = summ.reshape(len(groups) * tb, tb)
    lmask = np.stack([np.tile(m, (GLA_HEADS, 1)) for m in masks]).astype(np.float32)
    return summ, lmask


def _mixer_kernel(slab_ref, la_ref, dmask_ref, qdec_ref, kdec_ref, summ_ref, lmask_ref,
                  rnw_ref, gnw_ref, out_ref, r_state, g_state, *, tb, block_decay):
    @pl.when(pl.program_id(1) == 0)
    def _():
        r_state[...] = jnp.zeros_like(r_state)
        g_state[...] = jnp.zeros_like(g_state)

    for h in range(RET_HEADS):
        q = slab_ref[0, :, OFF_RQ + h * RET_D:OFF_RQ + (h + 1) * RET_D]
        k = slab_ref[0, :, OFF_RK + h * RET_D:OFF_RK + (h + 1) * RET_D]
        v = slab_ref[0, :, OFF_RV + h * RET_D:OFF_RV + (h + 1) * RET_D]
        g = slab_ref[0, :, OFF_RG + h * RET_D:OFF_RG + (h + 1) * RET_D].astype(F32)
        p = (_dot_nt(q, k) * dmask_ref[h]).astype(BF16)
        r_prev = r_state[h]
        o = _dot(p, v) + _dot(q, r_prev.astype(BF16)) * qdec_ref[h]
        kd = (k.astype(F32) * kdec_ref[h]).astype(BF16)
        r_state[h] = block_decay[h] * r_prev + _dot_tn(kd, v)
        o = o - jnp.mean(o, axis=-1, keepdims=True)
        o = o * lax.rsqrt(jnp.mean(o * o, axis=-1, keepdims=True) + LN_EPS)
        o = o * rnw_ref[:, h * RET_D:(h + 1) * RET_D] * _silu(g)
        out_ref[0, :, h * RET_D:(h + 1) * RET_D] = o.astype(out_ref.dtype)

    la = la_ref[0]
    la_hi = la.astype(BF16)
    la_lo = (la - la_hi.astype(F32)).astype(BF16)
    summ = summ_ref[...]
    sums = _dot(summ, la_hi) + _dot(summ, la_lo)
    lane = lax.broadcasted_iota(jnp.int32, (CHUNK, GLA_KEY_WIDTH), 1)
    head_of_lane = [(lane >= h * GLA_DK) & (lane < (h + 1) * GLA_DK) for h in range(GLA_HEADS)]

    def stack_heads(a):
        zero = jnp.zeros_like(a)
        return jnp.concatenate([jnp.where(head_of_lane[h], a, zero) for h in range(GLA_HEADS)], axis=0)

    for ch in range(tb // CHUNK):
        rows = slice(ch * CHUNK, (ch + 1) * CHUNK)
        gq = slab_ref[0, rows, OFF_GQ:OFF_GQ + GLA_KEY_WIDTH].astype(F32) * (GLA_DK ** -0.5)
        gk = slab_ref[0, rows, OFF_GK:OFF_GK + GLA_KEY_WIDTH].astype(F32)
        gv = slab_ref[0, rows, OFF_GV:OFF_GV + GLA_WIDTH]
        b = sums[ch * CHUNK:(ch + 1) * CHUNK]
        e = sums[tb + ch * CHUNK:tb + (ch + 1) * CHUNK]
        a = _dot_nt(stack_heads(gq.astype(BF16)), gk.astype(BF16)) * lmask_ref[N_LEVELS]
        for lvl in range(N_LEVELS):
            base = (2 + lvl) * tb + ch * CHUNK
            phi = jnp.exp(sums[base:base + CHUNK])
            a_l = _dot_nt(stack_heads((gq * phi).astype(BF16)), (gk * phi).astype(BF16))
            a = a + a_l * lmask_ref[lvl]
        a = a.astype(BF16)
        st = g_state[...]
        inter = _dot_nt(stack_heads((gq * jnp.exp(b)).astype(BF16)), st.astype(BF16))
        upd = _dot_tn(gv, (gk * jnp.exp(e)).astype(BF16))
        new_st = st * jnp.exp(b[CHUNK - 1:CHUNK, :])
        lane_s = lax.broadcasted_iota(jnp.int32, (GLA_DV, GLA_KEY_WIDTH), 1)
        for h in range(GLA_HEADS):
            in_head = (lane_s >= h * GLA_DK) & (lane_s < (h + 1) * GLA_DK)
            new_st = new_st + jnp.where(in_head, upd[h * GLA_DV:(h + 1) * GLA_DV], 0.0)
        g_state[...] = new_st
        for h in range(GLA_HEADS):
            o = _dot(a[h * CHUNK:(h + 1) * CHUNK], gv[:, h * GLA_DV:(h + 1) * GLA_DV])
            o = o + inter[h * CHUNK:(h + 1) * CHUNK]
            o = o * lax.rsqrt(jnp.mean(o * o, axis=-1, keepdims=True) + LN_EPS)
            gg = slab_ref[0, rows, OFF_GG + h * GLA_DV:OFF_GG + (h + 1) * GLA_DV].astype(F32)
            o = o * gnw_ref[:, h * GLA_DV:(h + 1) * GLA_DV] * _silu(gg)
            out_ref[0, rows, RET_WIDTH + h * GLA_DV:RET_WIDTH + (h + 1) * GLA_DV] = o.astype(out_ref.dtype)


def _mixer(slab, la, ret_norm_w, gla_norm_w, tb):
    bsz, s, _ = slab.shape
    nb = s // tb
    dmask, qdec, kdec, block_decay = _retention_tables(tb)
    summ, lmask = _gla_tables(tb)
    const2 = lambda b, i: (0, 0)
    const3 = lambda b, i: (0, 0, 0)
    d_mix = RET_WIDTH + GLA_WIDTH
    return pl.pallas_call(
        functools.partial(_mixer_kernel, tb=tb, block_decay=block_decay),
        grid=(bsz, nb),
        in_specs=[pl.BlockSpec((1, tb, SLAB), lambda b, i: (b, i, 0)),
                  pl.BlockSpec((1, tb, GLA_KEY_WIDTH), lambda b, i: (b, i, 0)),
                  pl.BlockSpec(dmask.shape, const3),
                  pl.BlockSpec(qdec.shape, const3),
                  pl.BlockSpec(kdec.shape, const3),
                  pl.BlockSpec(summ.shape, const2),
                  pl.BlockSpec(lmask.shape, const3),
                  pl.BlockSpec((1, RET_WIDTH), const2),
                  pl.BlockSpec((1, GLA_WIDTH), const2)],
        out_specs=pl.BlockSpec((1, tb, d_mix), lambda b, i: (b, i, 0)),
        out_shape=jax.ShapeDtypeStruct((bsz, s, d_mix), BF16),
        scratch_shapes=[pltpu.VMEM((RET_HEADS, RET_D, RET_D), F32),
                        pltpu.VMEM((GLA_DV, GLA_KEY_WIDTH), F32)],
        compiler_params=pltpu.CompilerParams(
            dimension_semantics=("arbitrary", "arbitrary"), vmem_limit_bytes=VMEM_LIMIT),
        name="mixer",
    )(slab, la, jnp.asarray(dmask), jnp.asarray(qdec), jnp.asarray(kdec),
      jnp.asarray(summ, BF16), jnp.asarray(lmask), ret_norm_w.reshape(1, -1), gla_norm_w.reshape(1, -1))


def _ffn_kernel(x_ref, mix_ref, mod_ref, wo_ref, w1_ref, w2_ref, ln1w_ref, ln1b_ref,
                ln2w_ref, ln2b_ref, out_ref, h_ref, *, alpha):
    x = x_ref[0]
    gate1 = mod_ref[0, 2:3, :]
    shift2 = mod_ref[0, 3:4, :]
    scale2 = mod_ref[0, 4:5, :]
    gate2 = mod_ref[0, 5:6, :]
    m = _dot(mix_ref[0], wo_ref[...])
    x1 = _ln(alpha * x + gate1 * m) * ln1w_ref[...] + ln1b_ref[...]
    u2 = (_ln(x1) * (1.0 + scale2) + shift2).astype(BF16)
    d_ff = w1_ref.shape[1]
    group = 1024
    for j in range(d_ff // group):
        hj = jnp.maximum(_dot(u2, w1_ref[:, j * group:(j + 1) * group]), 0.0)
        h_ref[:, j * group:(j + 1) * group] = (hj * hj).astype(BF16)
    f = _dot(h_ref[...], w2_ref[...])
    out_ref[0] = _ln(alpha * x1 + gate2 * f) * ln2w_ref[...] + ln2b_ref[...]


def _ffn(x, mixed, mod, w_out, w1, w2, ln1_w, ln1_b, ln2_w, ln2_b, alpha, tb):
    bsz, s, d = x.shape
    nb = s // tb
    d_ff = w1.shape[1]
    const2 = lambda b, i: (0, 0)
    row = lambda v: v.reshape(1, d)
    return pl.pallas_call(
        functools.partial(_ffn_kernel, alpha=alpha),
        grid=(bsz, nb),
        in_specs=[pl.BlockSpec((1, tb, d), lambda b, i: (b, i, 0)),
                  pl.BlockSpec((1, tb, d), lambda b, i: (b, i, 0)),
                  pl.BlockSpec((1, 6, d), lambda b, i: (b, 0, 0)),
                  pl.BlockSpec((d, d), const2, pipeline_mode=pl.Buffered(1)),
                  pl.BlockSpec((d, d_ff), const2, pipeline_mode=pl.Buffered(1)),
                  pl.BlockSpec((d_ff, d), const2, pipeline_mode=pl.Buffered(1)),
                  pl.BlockSpec((1, d), const2),
                  pl.BlockSpec((1, d), const2),
                  pl.BlockSpec((1, d), const2),
                  pl.BlockSpec((1, d), const2)],
        out_specs=pl.BlockSpec((1, tb, d), lambda b, i: (b, i, 0)),
        out_shape=jax.ShapeDtypeStruct((bsz, s, d), x.dtype),
        scratch_shapes=[pltpu.VMEM((tb, d_ff), BF16)],
        compiler_params=pltpu.CompilerParams(
            dimension_semantics=("parallel", "parallel"), vmem_limit_bytes=VMEM_LIMIT),
        name="ffn",
    )(x, mixed, mod, w_out, w1, w2, row(ln1_w), row(ln1_b), row(ln2_w), row(ln2_b))


def kernel(x, c, w_ada, b_ada, w_in, ret_norm_w, gla_gate_w, gla_gate_b, gla_norm_w,
           w_out, ln1_w, ln1_b, w_ff1, w_ff2, ln2_w, ln2_b):
    depth = w_in.shape[0]
    bsz, s, d = x.shape
    alpha = (2.0 * depth) ** 0.25
    tb_proj = min(512, s)
    tb_mix = min(256, s)
    tb_ffn = min(512, s)
    cos_t, sin_t = (jnp.asarray(t) for t in _rotary_tables(s))
    for l in range(depth):
        mod = _adaln(c, w_ada[l], b_ada[l]).reshape(bsz, 6, d)
        w_l = w_in[l].astype(BF16)
        slab, la = _inproj(x, mod, cos_t, sin_t, w_l[:, :SLAB], w_l[:, SLAB:],
                           gla_gate_w[l].astype(BF16), gla_gate_b[l].reshape(1, -1), tb_proj)
        mixed = _mixer(slab, la, ret_norm_w[l], gla_norm_w[l], tb_mix)
        x = _ffn(x, mixed, mod, w_out[l].astype(BF16), w_ff1[l].astype(BF16), w_ff2[l].astype(BF16),
                 ln1_w[l], ln1_b[l], ln2_w[l], ln2_b[l], alpha, tb_ffn)
    return x
```

```python
import functools

import numpy as np
import jax
import jax.numpy as jnp
from jax import lax
from jax.experimental import pallas as pl
from jax.experimental.pallas import tpu as pltpu

CHUNK = 64
RET_HEADS = 4
RET_D = 128
RET_WIDTH = RET_HEADS * RET_D
GLA_HEADS = 4
GLA_DK = 64
GLA_DV = 128
GLA_KEY_WIDTH = GLA_HEADS * GLA_DK
GLA_WIDTH = GLA_HEADS * GLA_DV
GATE_RANK = 16
GATE_TAU = 16.0
ROPE_BASE = 10000.0
LN_EPS = 1e-5
LANES = 128
SLAB = 4 * RET_WIDTH + 2 * GLA_KEY_WIDTH + 2 * GLA_WIDTH
OFF_RQ, OFF_RK, OFF_RV, OFF_RG = 0, 512, 1024, 1536
OFF_GQ, OFF_GK, OFF_GV, OFF_GG = 2048, 2304, 2560, 3072
VMEM_LIMIT = 56 * 1024 * 1024

BF16 = jnp.bfloat16
F32 = jnp.float32


def _dot(a, b):
    return jnp.dot(a, b, preferred_element_type=F32)


def _dot_nt(a, b):
    return lax.dot_general(a, b, (((1,), (1,)), ((), ())), preferred_element_type=F32)


def _dot_tn(a, b):
    return lax.dot_general(a, b, (((0,), (0,)), ((), ())), preferred_element_type=F32)


def _ln(x):
    mu = jnp.mean(x, axis=-1, keepdims=True)
    xc = x - mu
    var = jnp.mean(xc * xc, axis=-1, keepdims=True)
    return xc * lax.rsqrt(var + LN_EPS)


def _silu(x):
    return x * (1.0 / (1.0 + jnp.exp(-x)))


def _adaln_kernel(c_ref, w_ref, b_ref, o_ref):
    c = c_ref[...]
    o_ref[...] = _dot(_silu(c).astype(BF16), w_ref[...].astype(BF16)) + b_ref[...]


def _adaln(c, w, b):
    bsz, d = c.shape
    n = w.shape[1]
    tn = 1536
    return pl.pallas_call(
        _adaln_kernel,
        grid=(n // tn,),
        in_specs=[pl.BlockSpec((bsz, d), lambda j: (0, 0)),
                  pl.BlockSpec((d, tn), lambda j: (0, j)),
                  pl.BlockSpec((1, tn), lambda j: (0, j))],
        out_specs=pl.BlockSpec((bsz, tn), lambda j: (0, j)),
        out_shape=jax.ShapeDtypeStruct((bsz, n), F32),
        compiler_params=pltpu.CompilerParams(vmem_limit_bytes=VMEM_LIMIT),
        name="adaln",
    )(c, w, b.reshape(1, n))


def _inproj_kernel(x_ref, mod_ref, cos_ref, sin_ref, w_ref, wg_ref, gw_ref, gb_ref,
                   slab_ref, la_ref):
    x = x_ref[0]
    shift1 = mod_ref[0, 0:1, :]
    scale1 = mod_ref[0, 1:2, :]
    ub = (_ln(x) * (1.0 + scale1) + shift1).astype(BF16)
    cos = cos_ref[...]
    sin = sin_ref[...]
    group = 512
    for j in range(SLAB // group):
        p = _dot(ub, w_ref[:, j * group:(j + 1) * group])
        if j * group < OFF_RV:
            for h in range(RET_HEADS):
                xh = p[:, h * RET_D:(h + 1) * RET_D]
                rot = xh * cos + pltpu.roll(xh, RET_D // 2, 1) * sin
                slab_ref[0, :, j * group + h * RET_D:j * group + (h + 1) * RET_D] = rot.astype(BF16)
        else:
            slab_ref[0, :, j * group:(j + 1) * group] = p.astype(BF16)
    glr = _dot(ub, wg_ref[...])
    logit = _dot(glr.astype(BF16), gw_ref[...]) + gb_ref[...]
    log_sig = jnp.minimum(logit, 0.0) - jnp.log1p(jnp.exp(-jnp.abs(logit)))
    la_ref[0] = log_sig * (1.0 / GATE_TAU)


def _inproj(x, mod, cos_t, sin_t, w_slab, w_glr, gate_w, gate_b, tb):
    bsz, s, d = x.shape
    nb = s // tb
    const2 = lambda b, i: (0, 0)
    return pl.pallas_call(
        _inproj_kernel,
        grid=(bsz, nb),
        in_specs=[pl.BlockSpec((1, tb, d), lambda b, i: (b, i, 0)),
                  pl.BlockSpec((1, 6, d), lambda b, i: (b, 0, 0)),
                  pl.BlockSpec((tb, RET_D), lambda b, i: (i, 0)),
                  pl.BlockSpec((tb, RET_D), lambda b, i: (i, 0)),
                  pl.BlockSpec((d, SLAB), const2),
                  pl.BlockSpec((d, GATE_RANK), const2),
                  pl.BlockSpec((GATE_RANK, GLA_KEY_WIDTH), const2),
                  pl.BlockSpec((1, GLA_KEY_WIDTH), const2)],
        out_specs=[pl.BlockSpec((1, tb, SLAB), lambda b, i: (b, i, 0)),
                   pl.BlockSpec((1, tb, GLA_KEY_WIDTH), lambda b, i: (b, i, 0))],
        out_shape=[jax.ShapeDtypeStruct((bsz, s, SLAB), BF16),
                   jax.ShapeDtypeStruct((bsz, s, GLA_KEY_WIDTH), F32)],
        compiler_params=pltpu.CompilerParams(
            dimension_semantics=("parallel", "parallel"), vmem_limit_bytes=VMEM_LIMIT),
        name="inproj",
    )(x, mod, cos_t, sin_t, w_slab, w_glr, gate_w, gate_b)


def _rotary_tables(s):
    half = RET_D // 2
    inv = 1.0 / (ROPE_BASE ** (np.arange(half, dtype=np.float64) / (half - 1)))
    ang = np.arange(s, dtype=np.float64)[:, None] * inv[None, :]
    cos = np.concatenate([np.cos(ang), np.cos(ang)], axis=1)
    sin = np.concatenate([-np.sin(ang), np.sin(ang)], axis=1)
    return cos.astype(np.float32), sin.astype(np.float32)


def _retention_tables(tb):
    log_gamma = np.log(1.0 - 2.0 ** (-5.0 - np.arange(RET_HEADS, dtype=np.float64)))
    idx = np.arange(tb)
    dist = np.abs(idx[:, None] - idx[None, :])
    visible = (idx[None, :] // CHUNK) <= (idx[:, None] // CHUNK)
    scale = RET_D ** -0.5
    dmask = np.exp(log_gamma[:, None, None] * dist[None]) * visible[None] * scale
    qdec = np.exp(log_gamma[:, None] * (idx[None, :] + 1.0)) * scale
    kdec = np.exp(log_gamma[:, None] * (tb - 1.0 - idx[None, :]))
    qdec = np.broadcast_to(qdec[:, :, None], (RET_HEADS, tb, RET_D))
    kdec = np.broadcast_to(kdec[:, :, None], (RET_HEADS, tb, RET_D))
    block_decay = [float(np.exp(lg * tb)) for lg in log_gamma]
    return (dmask.astype(np.float32), np.ascontiguousarray(qdec, np.float32),
            np.ascontiguousarray(kdec, np.float32), block_decay)


def _cumsum_matrix(tb):
    t = np.arange(tb)
    same_chunk = (t[:, None] // CHUNK) == (t[None, :] // CHUNK)
    return (same_chunk & (t[None, :] <= t[:, None])).astype(np.float32)


def _row_of_block(b, block, row):
    n = b.shape[0] // block
    b3 = b.reshape(n, block, b.shape[1])
    return jnp.broadcast_to(b3[:, row:row + 1, :], b3.shape).reshape(b.shape)


def _mixer_kernel(slab_ref, la_ref, dmask_ref, qdec_ref, kdec_ref, csum_ref,
                  rnw_ref, gnw_ref, out_ref, r_state, g_state, *, tb, block_decay):
    @pl.when(pl.program_id(1) == 0)
    def _():
        r_state[...] = jnp.zeros_like(r_state)
        g_state[...] = jnp.zeros_like(g_state)

    for h in range(RET_HEADS):
        q = slab_ref[0, :, OFF_RQ + h * RET_D:OFF_RQ + (h + 1) * RET_D]
        k = slab_ref[0, :, OFF_RK + h * RET_D:OFF_RK + (h + 1) * RET_D]
        v = slab_ref[0, :, OFF_RV + h * RET_D:OFF_RV + (h + 1) * RET_D]
        g = slab_ref[0, :, OFF_RG + h * RET_D:OFF_RG + (h + 1) * RET_D].astype(F32)
        p = (_dot_nt(q, k) * dmask_ref[h]).astype(BF16)
        r_prev = r_state[h]
        o = _dot(p, v) + _dot(q, r_prev.astype(BF16)) * qdec_ref[h]
        kd = (k.astype(F32) * kdec_ref[h]).astype(BF16)
        r_state[h] = block_decay[h] * r_prev + _dot_tn(kd, v)
        o = o - jnp.mean(o, axis=-1, keepdims=True)
        o = o * lax.rsqrt(jnp.mean(o * o, axis=-1, keepdims=True) + LN_EPS)
        o = o * rnw_ref[:, h * RET_D:(h + 1) * RET_D] * _silu(g)
        out_ref[0, :, h * RET_D:(h + 1) * RET_D] = o.astype(out_ref.dtype)

    la = la_ref[0]
    la_hi = la.astype(BF16)
    la_lo = (la - la_hi.astype(F32)).astype(BF16)
    csum = csum_ref[...]
    b = _dot(csum, la_hi) + _dot(csum, la_lo)
    row = lax.broadcasted_iota(jnp.int32, (tb, GLA_KEY_WIDTH), 0)
    psis = [-jnp.abs(b - _row_of_block(b, 2 * c, c - 1)) for c in (32, 16, 8, 4)]
    la_next = pltpu.roll(la, tb - 1, 0)
    la_prev = pltpu.roll(la, 1, 0)
    r4 = row & 3
    psis.append(jnp.where(r4 == 0, la_next, jnp.where(r4 == 1, 0.0, jnp.where(r4 == 2, la, la + la_prev))))
    psis.append(jnp.where((row & 1) == 1, la, 0.0))
    gq = slab_ref[0, :, OFF_GQ:OFF_GQ + GLA_KEY_WIDTH] * jnp.asarray(GLA_DK ** -0.5, BF16)
    gk = slab_ref[0, :, OFF_GK:OFF_GK + GLA_KEY_WIDTH]
    phis = [jnp.exp(p).astype(BF16) for p in psis]
    q_lvl = [gq] + [gq * phi for phi in phis]
    phis = [None] + phis
    q_state = gq * jnp.exp(b).astype(BF16)
    k_state = gk * jnp.exp(_row_of_block(b, CHUNK, CHUNK - 1) - b).astype(BF16)

    lane = lax.broadcasted_iota(jnp.int32, (CHUNK, LANES), 1)
    low_half = lane < GLA_DK
    zero_k = jnp.zeros((CHUNK, LANES), BF16)
    zero_s = jnp.zeros((GLA_DV, LANES), BF16)
    t_xor_s = lax.broadcasted_iota(jnp.int32, (CHUNK, GLA_HEADS * CHUNK), 0) ^ (
        lax.broadcasted_iota(jnp.int32, (CHUNK, GLA_HEADS * CHUNK), 1) & (CHUNK - 1))
    lane_s = lax.broadcasted_iota(jnp.int32, (GLA_DV, LANES), 1)
    low_half_s = lane_s < GLA_DK

    def lane_tiles(piece, tile, zero, n_tiles):
        return jnp.concatenate([piece if j == tile else zero for j in range(n_tiles)], axis=1)

    for ch in range(tb // CHUNK):
        rows = slice(ch * CHUNK, (ch + 1) * CHUNK)
        k_heads = [jnp.where(low_half if h % 2 == 0 else ~low_half,
                             gk[rows, (h // 2) * LANES:(h // 2 + 1) * LANES], zero_k)
                   for h in range(GLA_HEADS)]
        a_lvl = []
        for lvl in range(len(q_lvl)):
            pieces = []
            for h in range(GLA_HEADS):
                tile = h // 2
                kh = k_heads[h]
                if phis[lvl] is not None:
                    kh = kh * phis[lvl][rows, tile * LANES:(tile + 1) * LANES]
                pieces.append(lane_tiles(kh, tile, zero_k, 2))
            k_stack = jnp.concatenate(pieces, axis=0)
            a_lvl.append(_dot_nt(q_lvl[lvl][rows], k_stack))
        a = a_lvl[0]
        for i, c in enumerate((1, 2, 4, 8, 16, 32)):
            a = jnp.where(t_xor_s >= c, a_lvl[len(a_lvl) - 1 - i], a)
        a = a.astype(BF16)
        gv = slab_ref[0, rows, OFF_GV:OFF_GV + GLA_WIDTH]
        v_bd = jnp.concatenate(
            [lane_tiles(gv[:, h * GLA_DV:(h + 1) * GLA_DV], h, jnp.zeros((CHUNK, GLA_DV), BF16), GLA_HEADS)
             for h in range(GLA_HEADS)], axis=0)
        s_bd = jnp.concatenate(
            [lane_tiles(g_state[h].astype(BF16), h // 2, zero_s, 2) for h in range(GLA_HEADS)],
            axis=0)
        o_all = _dot(a, v_bd) + _dot_nt(q_state[rows], s_bd)
        decay = jnp.exp(b[ch * CHUNK + CHUNK - 1:(ch + 1) * CHUNK, :])
        for h in range(GLA_HEADS):
            tile = h // 2
            upd = _dot_tn(gv[:, h * GLA_DV:(h + 1) * GLA_DV], k_state[rows, tile * LANES:(tile + 1) * LANES])
            upd = jnp.where(low_half_s if h % 2 == 0 else ~low_half_s, upd, 0.0)
            g_state[h] = g_state[h] * decay[:, tile * LANES:(tile + 1) * LANES] + upd
        for h in range(GLA_HEADS):
            o = o_all[:, h * GLA_DV:(h + 1) * GLA_DV]
            o = o * lax.rsqrt(jnp.mean(o * o, axis=-1, keepdims=True) + LN_EPS)
            gg = slab_ref[0, rows, OFF_GG + h * GLA_DV:OFF_GG + (h + 1) * GLA_DV].astype(F32)
            o = o * gnw_ref[:, h * GLA_DV:(h + 1) * GLA_DV] * _silu(gg)
            out_ref[0, rows, RET_WIDTH + h * GLA_DV:RET_WIDTH + (h + 1) * GLA_DV] = o.astype(out_ref.dtype)


def _mixer(slab, la, ret_norm_w, gla_norm_w, tb):
    bsz, s, _ = slab.shape
    nb = s // tb
    dmask, qdec, kdec, block_decay = _retention_tables(tb)
    csum = _cumsum_matrix(tb)
    const2 = lambda b, i: (0, 0)
    const3 = lambda b, i: (0, 0, 0)
    d_mix = RET_WIDTH + GLA_WIDTH
    return pl.pallas_call(
        functools.partial(_mixer_kernel, tb=tb, block_decay=block_decay),
        grid=(bsz, nb),
        in_specs=[pl.BlockSpec((1, tb, SLAB), lambda b, i: (b, i, 0)),
                  pl.BlockSpec((1, tb, GLA_KEY_WIDTH), lambda b, i: (b, i, 0)),
                  pl.BlockSpec(dmask.shape, const3),
                  pl.BlockSpec(qdec.shape, const3),
                  pl.BlockSpec(kdec.shape, const3),
                  pl.BlockSpec(csum.shape, const2),
                  pl.BlockSpec((1, RET_WIDTH), const2),
                  pl.BlockSpec((1, GLA_WIDTH), const2)],
        out_specs=pl.BlockSpec((1, tb, d_mix), lambda b, i: (b, i, 0)),
        out_shape=jax.ShapeDtypeStruct((bsz, s, d_mix), BF16),
        scratch_shapes=[pltpu.VMEM((RET_HEADS, RET_D, RET_D), F32),
                        pltpu.VMEM((GLA_HEADS, GLA_DV, LANES), F32)],
        compiler_params=pltpu.CompilerParams(
            dimension_semantics=("arbitrary", "arbitrary"), vmem_limit_bytes=VMEM_LIMIT),
        name="mixer",
    )(slab, la, jnp.asarray(dmask), jnp.asarray(qdec), jnp.asarray(kdec),
      jnp.asarray(csum, BF16), ret_norm_w.reshape(1, -1), gla_norm_w.reshape(1, -1))


def _ffn_kernel(x_ref, mix_ref, mod_ref, wo_ref, w1_ref, w2_ref, ln1w_ref, ln1b_ref,
                ln2w_ref, ln2b_ref, out_ref, h_ref, *, alpha):
    x = x_ref[0]
    gate1 = mod_ref[0, 2:3, :]
    shift2 = mod_ref[0, 3:4, :]
    scale2 = mod_ref[0, 4:5, :]
    gate2 = mod_ref[0, 5:6, :]
    m = _dot(mix_ref[0], wo_ref[...])
    x1 = _ln(alpha * x + gate1 * m) * ln1w_ref[...] + ln1b_ref[...]
    u2 = (_ln(x1) * (1.0 + scale2) + shift2).astype(BF16)
    d_ff = w1_ref.shape[1]
    group = 1024
    for j in range(d_ff // group):
        hj = jnp.maximum(_dot(u2, w1_ref[:, j * group:(j + 1) * group]), 0.0)
        h_ref[:, j * group:(j + 1) * group] = (hj * hj).astype(BF16)
    f = _dot(h_ref[...], w2_ref[...])
    out_ref[0] = _ln(alpha * x1 + gate2 * f) * ln2w_ref[...] + ln2b_ref[...]


def _ffn(x, mixed, mod, w_out, w1, w2, ln1_w, ln1_b, ln2_w, ln2_b, alpha, tb):
    bsz, s, d = x.shape
    nb = s // tb
    d_ff = w1.shape[1]
    const2 = lambda b, i: (0, 0)
    row = lambda v: v.reshape(1, d)
    return pl.pallas_call(
        functools.partial(_ffn_kernel, alpha=alpha),
        grid=(bsz, nb),
        in_specs=[pl.BlockSpec((1, tb, d), lambda b, i: (b, i, 0)),
                  pl.BlockSpec((1, tb, d), lambda b, i: (b, i, 0)),
                  pl.BlockSpec((1, 6, d), lambda b, i: (b, 0, 0)),
                  pl.BlockSpec((d, d), const2, pipeline_mode=pl.Buffered(1)),
                  pl.BlockSpec((d, d_ff), const2, pipeline_mode=pl.Buffered(1)),
                  pl.BlockSpec((d_ff, d), const2, pipeline_mode=pl.Buffered(1)),
                  pl.BlockSpec((1, d), const2),
                  pl.BlockSpec((1, d), const2),
                  pl.BlockSpec((1, d), const2),
                  pl.BlockSpec((1, d), const2)],
        out_specs=pl.BlockSpec((1, tb, d), lambda b, i: (b, i, 0)),
        out_shape=jax.ShapeDtypeStruct((bsz, s, d), x.dtype),
        scratch_shapes=[pltpu.VMEM((tb, d_ff), BF16)],
        compiler_params=pltpu.CompilerParams(
            dimension_semantics=("parallel", "parallel"), vmem_limit_bytes=VMEM_LIMIT),
        name="ffn",
    )(x, mixed, mod, w_out, w1, w2, row(ln1_w), row(ln1_b), row(ln2_w), row(ln2_b))


def kernel(x, c, w_ada, b_ada, w_in, ret_norm_w, gla_gate_w, gla_gate_b, gla_norm_w,
           w_out, ln1_w, ln1_b, w_ff1, w_ff2, ln2_w, ln2_b):
    depth = w_in.shape[0]
    bsz, s, d = x.shape
    alpha = (2.0 * depth) ** 0.25
    tb_proj = min(512, s)
    tb_mix = min(256, s)
    tb_ffn = min(512, s)
    cos_t, sin_t = (jnp.asarray(t) for t in _rotary_tables(s))
    for l in range(depth):
        mod = _adaln(c, w_ada[l], b_ada[l]).reshape(bsz, 6, d)
        w_l = w_in[l].astype(BF16)
        slab, la = _inproj(x, mod, cos_t, sin_t, w_l[:, :SLAB], w_l[:, SLAB:],
                           gla_gate_w[l].astype(BF16), gla_gate_b[l].reshape(1, -1), tb_proj)
        mixed = _mixer(slab, la, ret_norm_w[l], gla_norm_w[l], tb_mix)
        x = _ffn(x, mixed, mod, w_out[l].astype(BF16), w_ff1[l].astype(BF16), w_ff2[l].astype(BF16),
                 ln1_w[l], ln1_b[l], ln2_w[l], ln2_b[l], alpha, tb_ffn)
    return x
```

```python
import functools
import math

import numpy as np
import jax
import jax.numpy as jnp
from jax import lax
from jax.experimental import pallas as pl
from jax.experimental.pallas import tpu as pltpu

CHUNK = 64
RET_HEADS = 4
RET_D = 128
RET_WIDTH = RET_HEADS * RET_D
GLA_HEADS = 4
GLA_DK = 64
GLA_DV = 128
GLA_KEY_WIDTH = GLA_HEADS * GLA_DK
GLA_WIDTH = GLA_HEADS * GLA_DV
D_MIX = RET_WIDTH + GLA_WIDTH
GATE_RANK = 16
GATE_TAU = 16.0
ROPE_BASE = 10000.0
LN_EPS = 1e-5
LANES = 128
SLAB = 4 * RET_WIDTH + 2 * GLA_KEY_WIDTH + 2 * GLA_WIDTH
OFF_RQ, OFF_RK, OFF_RV, OFF_RG = 0, 512, 1024, 1536
OFF_GQ, OFF_GK, OFF_GV, OFF_GG = 2048, 2304, 2560, 3072
MIX_ROWS = 256
VMEM_LIMIT = 56 * 1024 * 1024
LOG2E = math.log2(math.e)

BF16 = jnp.bfloat16
F32 = jnp.float32


def _dot(a, b):
    return jnp.dot(a, b, preferred_element_type=F32)


def _dot_nt(a, b):
    return lax.dot_general(a, b, (((1,), (1,)), ((), ())), preferred_element_type=F32)


def _dot_tn(a, b):
    return lax.dot_general(a, b, (((0,), (0,)), ((), ())), preferred_element_type=F32)


def _ln(x):
    mu = jnp.mean(x, axis=-1, keepdims=True)
    xc = x - mu
    var = jnp.mean(xc * xc, axis=-1, keepdims=True)
    return xc * lax.rsqrt(var + LN_EPS)


def _silu(x):
    return x * (1.0 / (1.0 + jnp.exp(-x)))


def _adaln_kernel(c_ref, w_ref, b_ref, o_ref):
    c = c_ref[...]
    o_ref[...] = _dot(_silu(c).astype(BF16), w_ref[...].astype(BF16)) + b_ref[...]


def _adaln(c, w, b):
    bsz, d = c.shape
    n = w.shape[1]
    tn = 1536
    return pl.pallas_call(
        _adaln_kernel,
        grid=(n // tn,),
        in_specs=[pl.BlockSpec((bsz, d), lambda j: (0, 0)),
                  pl.BlockSpec((d, tn), lambda j: (0, j)),
                  pl.BlockSpec((1, tn), lambda j: (0, j))],
        out_specs=pl.BlockSpec((bsz, tn), lambda j: (0, j)),
        out_shape=jax.ShapeDtypeStruct((bsz, n), F32),
        compiler_params=pltpu.CompilerParams(vmem_limit_bytes=VMEM_LIMIT),
        name="adaln",
    )(c, w, b.reshape(1, n))


def _rotary_tables(s):
    half = RET_D // 2
    inv = 1.0 / (ROPE_BASE ** (np.arange(half, dtype=np.float64) / (half - 1)))
    ang = np.arange(s, dtype=np.float64)[:, None] * inv[None, :]
    cos = np.concatenate([np.cos(ang), np.cos(ang)], axis=1)
    sin = np.concatenate([-np.sin(ang), np.sin(ang)], axis=1)
    return cos.astype(np.float32), sin.astype(np.float32)


def _retention_tables(tb):
    log_gamma = np.log(1.0 - 2.0 ** (-5.0 - np.arange(RET_HEADS, dtype=np.float64)))
    idx = np.arange(tb)
    dist = np.abs(idx[:, None] - idx[None, :])
    visible = (idx[None, :] // CHUNK) <= (idx[:, None] // CHUNK)
    scale = RET_D ** -0.5
    dmask = np.exp(log_gamma[:, None, None] * dist[None]) * visible[None] * scale
    qdec = np.exp(log_gamma[:, None] * (idx[None, :] + 1.0)) * scale
    kdec = np.exp(log_gamma[:, None] * (tb - 1.0 - idx[None, :]))
    qdec = np.broadcast_to(qdec[:, :, None], (RET_HEADS, tb, RET_D))
    kdec = np.broadcast_to(kdec[:, :, None], (RET_HEADS, tb, RET_D))
    block_decay = [float(np.exp(lg * tb)) for lg in log_gamma]
    return (dmask.astype(np.float32), np.ascontiguousarray(qdec, np.float32),
            np.ascontiguousarray(kdec, np.float32), block_decay)


def _cumsum_matrix(tb):
    t = np.arange(tb)
    same_chunk = (t[:, None] // CHUNK) == (t[None, :] // CHUNK)
    return (same_chunk & (t[None, :] <= t[:, None])).astype(np.float32)


def _proj_stage(x_ref, mod_ref, cos_ref, sin_ref, w_ref, wg_ref, gw_ref, gb_ref, slab, la_out):
    x = x_ref[0]
    shift1 = mod_ref[0, 0:1, :]
    scale1 = mod_ref[0, 1:2, :]
    ub = (_ln(x) * (1.0 + scale1) + shift1).astype(BF16)
    yield
    cos = cos_ref[...]
    sin = sin_ref[...]
    group = 256
    for j in range(SLAB // group):
        p = _dot(ub, w_ref[:, j * group:(j + 1) * group])
        if j * group < OFF_RV:
            for h in range(group // RET_D):
                xh = p[:, h * RET_D:(h + 1) * RET_D]
                rot = xh * cos + pltpu.roll(xh, RET_D // 2, 1) * sin
                slab[:, j * group + h * RET_D:j * group + (h + 1) * RET_D] = rot.astype(BF16)
        else:
            slab[:, j * group:(j + 1) * group] = p.astype(BF16)
        yield
    glr = _dot(ub, wg_ref[...])
    logit = _dot(glr.astype(BF16), gw_ref[...]) + gb_ref[...]
    log_sig = jnp.minimum(logit, 0.0) - jnp.log1p(jnp.exp(-jnp.abs(logit)))
    la_out[...] = log_sig * (LOG2E / GATE_TAU)
    yield


def _row_of_block(b, block, row):
    n = b.shape[0] // block
    b3 = b.reshape(n, block, b.shape[1])
    return jnp.broadcast_to(b3[:, row:row + 1, :], b3.shape).reshape(b.shape)


def _mix_stage(slab, la_ref, r0, dmask_ref, qdec_ref, kdec_ref, csum_ref, rnw_ref, gnw_ref,
               out_ref, r_state, g_state, block_decay):
    tb = MIX_ROWS
    blk = slice(r0, r0 + tb)
    for h in range(RET_HEADS):
        q = slab[blk, OFF_RQ + h * RET_D:OFF_RQ + (h + 1) * RET_D]
        k = slab[blk, OFF_RK + h * RET_D:OFF_RK + (h + 1) * RET_D]
        v = slab[blk, OFF_RV + h * RET_D:OFF_RV + (h + 1) * RET_D]
        g = slab[blk, OFF_RG + h * RET_D:OFF_RG + (h + 1) * RET_D].astype(F32)
        p = (_dot_nt(q, k) * dmask_ref[h]).astype(BF16)
        r_prev = r_state[h]
        o = _dot(p, v) + _dot(q, r_prev.astype(BF16)) * qdec_ref[h]
        kd = (k.astype(F32) * kdec_ref[h]).astype(BF16)
        r_state[h] = block_decay[h] * r_prev + _dot_tn(kd, v)
        o = o - jnp.mean(o, axis=-1, keepdims=True)
        o = o * lax.rsqrt(jnp.mean(o * o, axis=-1, keepdims=True) + LN_EPS)
        o = o * rnw_ref[:, h * RET_D:(h + 1) * RET_D] * _silu(g)
        out_ref[0, blk, h * RET_D:(h + 1) * RET_D] = o.astype(out_ref.dtype)
        yield

    la = la_ref[blk, :]
    la_hi = la.astype(BF16)
    la_lo = (la - la_hi.astype(F32)).astype(BF16)
    csum = csum_ref[...]
    b = _dot(csum, la_hi) + _dot(csum, la_lo)
    row = lax.broadcasted_iota(jnp.int32, (tb, GLA_KEY_WIDTH), 0)
    psis = [-jnp.abs(b - _row_of_block(b, 2 * c, c - 1)) for c in (32, 16, 8, 4)]
    la_next = pltpu.roll(la, tb - 1, 0)
    la_prev = pltpu.roll(la, 1, 0)
    r4 = row & 3
    psis.append(jnp.where(r4 == 0, la_next, jnp.where(r4 == 1, 0.0, jnp.where(r4 == 2, la, la + la_prev))))
    psis.append(jnp.where((row & 1) == 1, la, 0.0))
    gq = slab[blk, OFF_GQ:OFF_GQ + GLA_KEY_WIDTH] * jnp.asarray(GLA_DK ** -0.5, BF16)
    gk = slab[blk, OFF_GK:OFF_GK + GLA_KEY_WIDTH]
    phis = [jnp.exp2(p).astype(BF16) for p in psis]
    q_lvl = [gq] + [gq * phi for phi in phis]
    phis = [None] + phis
    q_state = gq * jnp.exp2(b).astype(BF16)
    k_state = gk * jnp.exp2(_row_of_block(b, CHUNK, CHUNK - 1) - b).astype(BF16)
    yield

    lane = lax.broadcasted_iota(jnp.int32, (CHUNK, LANES), 1)
    low_half = lane < GLA_DK
    zero_k = jnp.zeros((CHUNK, LANES), BF16)
    zero_s = jnp.zeros((GLA_DV, LANES), BF16)
    t_xor_s = lax.broadcasted_iota(jnp.int32, (CHUNK, GLA_HEADS * CHUNK), 0) ^ (
        lax.broadcasted_iota(jnp.int32, (CHUNK, GLA_HEADS * CHUNK), 1) & (CHUNK - 1))
    lane_s = lax.broadcasted_iota(jnp.int32, (GLA_DV, LANES), 1)
    low_half_s = lane_s < GLA_DK

    def lane_tiles(piece, tile, zero, n_tiles):
        return jnp.concatenate([piece if j == tile else zero for j in range(n_tiles)], axis=1)

    for ch in range(tb // CHUNK):
        rows = slice(ch * CHUNK, (ch + 1) * CHUNK)
        orow = slice(r0 + ch * CHUNK, r0 + (ch + 1) * CHUNK)
        k_heads = [jnp.where(low_half if h % 2 == 0 else ~low_half,
                             gk[rows, (h // 2) * LANES:(h // 2 + 1) * LANES], zero_k)
                   for h in range(GLA_HEADS)]
        a_lvl = []
        for lvl in range(len(q_lvl)):
            pieces = []
            for h in range(GLA_HEADS):
                tile = h // 2
                kh = k_heads[h]
                if phis[lvl] is not None:
                    kh = kh * phis[lvl][rows, tile * LANES:(tile + 1) * LANES]
                pieces.append(lane_tiles(kh, tile, zero_k, 2))
            k_stack = jnp.concatenate(pieces, axis=0)
            a_lvl.append(_dot_nt(q_lvl[lvl][rows], k_stack))
        a = a_lvl[0]
        for i, c in enumerate((1, 2, 4, 8, 16, 32)):
            a = jnp.where(t_xor_s >= c, a_lvl[len(a_lvl) - 1 - i], a)
        a = a.astype(BF16)
        gv = slab[orow, OFF_GV:OFF_GV + GLA_WIDTH]
        v_bd = jnp.concatenate(
            [lane_tiles(gv[:, h * GLA_DV:(h + 1) * GLA_DV], h, jnp.zeros((CHUNK, GLA_DV), BF16), GLA_HEADS)
             for h in range(GLA_HEADS)], axis=0)
        s_bd = jnp.concatenate(
            [lane_tiles(g_state[h].astype(BF16), h // 2, zero_s, 2) for h in range(GLA_HEADS)],
            axis=0)
        o_all = _dot(a, v_bd) + _dot_nt(q_state[rows], s_bd)
        decay = jnp.exp2(b[ch * CHUNK + CHUNK - 1:(ch + 1) * CHUNK, :])
        for h in range(GLA_HEADS):
            tile = h // 2
            upd = _dot_tn(gv[:, h * GLA_DV:(h + 1) * GLA_DV], k_state[rows, tile * LANES:(tile + 1) * LANES])
            upd = jnp.where(low_half_s if h % 2 == 0 else ~low_half_s, upd, 0.0)
            g_state[h] = g_state[h] * decay[:, tile * LANES:(tile + 1) * LANES] + upd
        for h in range(GLA_HEADS):
            o = o_all[:, h * GLA_DV:(h + 1) * GLA_DV]
            o = o * lax.rsqrt(jnp.mean(o * o, axis=-1, keepdims=True) + LN_EPS)
            gg = slab[orow, OFF_GG + h * GLA_DV:OFF_GG + (h + 1) * GLA_DV].astype(F32)
            o = o * gnw_ref[:, h * GLA_DV:(h + 1) * GLA_DV] * _silu(gg)
            out_ref[0, orow, RET_WIDTH + h * GLA_DV:RET_WIDTH + (h + 1) * GLA_DV] = o.astype(out_ref.dtype)
        yield


def _interleave(gen_a, n_a, gen_b, n_b):
    done_a = done_b = 0
    while done_a < n_a or done_b < n_b:
        if done_b >= n_b or (done_a < n_a and (done_a + 0.5) * n_b <= (done_b + 0.5) * n_a):
            next(gen_a)
            done_a += 1
        else:
            next(gen_b)
            done_b += 1
    for g in (gen_a, gen_b):
        assert next(g, "done") == "done", "piece count mismatch"


def _projmix_kernel(x_ref, mod_ref, cos_ref, sin_ref, w_ref, wg_ref, gw_ref, gb_ref,
                    dmask_ref, qdec_ref, kdec_ref, csum_ref, rnw_ref, gnw_ref,
                    out_ref, slab0, slab1, la0, la1, r_state, g_state,
                    *, tb, blocks_per_seq, block_decay):
    j = pl.program_id(0)

    @pl.when(j == 0)
    def _():
        slab1[...] = jnp.zeros_like(slab1)
        la1[...] = jnp.zeros_like(la1)

    @pl.when((j == 0) | ((j - 1) % blocks_per_seq == 0))
    def _():
        r_state[...] = jnp.zeros_like(r_state)
        g_state[...] = jnp.zeros_like(g_state)

    def mix_all(slab_r, la_r):
        for r0 in range(0, tb, MIX_ROWS):
            yield from _mix_stage(slab_r, la_r, r0, dmask_ref, qdec_ref, kdec_ref, csum_ref,
                                  rnw_ref, gnw_ref, out_ref, r_state, g_state, block_decay)

    def step(slab_w, la_w, slab_r, la_r):
        proj = _proj_stage(x_ref, mod_ref, cos_ref, sin_ref, w_ref, wg_ref, gw_ref, gb_ref, slab_w, la_w)
        n_proj = 2 + SLAB // 256
        n_mix = (tb // MIX_ROWS) * (RET_HEADS + 1 + MIX_ROWS // CHUNK)
        _interleave(proj, n_proj, mix_all(slab_r, la_r), n_mix)

    @pl.when(j % 2 == 0)
    def _():
        step(slab0, la0, slab1, la1)

    @pl.when(j % 2 == 1)
    def _():
        step(slab1, la1, slab0, la0)


def _projmix(x, mod, cos_t, sin_t, w_slab, w_glr, gate_w, gate_b, ret_norm_w, gla_norm_w, tb):
    bsz, s, d = x.shape
    bps = s // tb
    nblk = bsz * bps
    dmask, qdec, kdec, block_decay = _retention_tables(MIX_ROWS)
    csum = _cumsum_matrix(MIX_ROWS)
    const2 = lambda j: (0, 0)
    const3 = lambda j: (0, 0, 0)
    proj_blk = lambda j: jnp.minimum(j, nblk - 1)
    mix_blk = lambda j: jnp.maximum(j - 1, 0)
    single = dict(pipeline_mode=pl.Buffered(1))
    return pl.pallas_call(
        functools.partial(_projmix_kernel, tb=tb, blocks_per_seq=bps, block_decay=block_decay),
        grid=(nblk + 1,),
        in_specs=[pl.BlockSpec((1, tb, d), lambda j: (proj_blk(j) // bps, proj_blk(j) % bps, 0)),
                  pl.BlockSpec((1, 6, d), lambda j: (proj_blk(j) // bps, 0, 0)),
                  pl.BlockSpec((tb, RET_D), lambda j: (proj_blk(j) % bps, 0)),
                  pl.BlockSpec((tb, RET_D), lambda j: (proj_blk(j) % bps, 0)),
                  pl.BlockSpec((d, SLAB), const2, **single),
                  pl.BlockSpec((d, GATE_RANK), const2, **single),
                  pl.BlockSpec((GATE_RANK, GLA_KEY_WIDTH), const2, **single),
                  pl.BlockSpec((1, GLA_KEY_WIDTH), const2, **single),
                  pl.BlockSpec(dmask.shape, const3, **single),
                  pl.BlockSpec(qdec.shape, const3, **single),
                  pl.BlockSpec(kdec.shape, const3, **single),
                  pl.BlockSpec(csum.shape, const2, **single),
                  pl.BlockSpec((1, RET_WIDTH), const2, **single),
                  pl.BlockSpec((1, GLA_WIDTH), const2, **single)],
        out_specs=pl.BlockSpec((1, tb, D_MIX), lambda j: (mix_blk(j) // bps, mix_blk(j) % bps, 0)),
        out_shape=jax.ShapeDtypeStruct((bsz, s, D_MIX), BF16),
        scratch_shapes=[pltpu.VMEM((tb, SLAB), BF16), pltpu.VMEM((tb, SLAB), BF16),
                        pltpu.VMEM((tb, GLA_KEY_WIDTH), F32), pltpu.VMEM((tb, GLA_KEY_WIDTH), F32),
                        pltpu.VMEM((RET_HEADS, RET_D, RET_D), F32),
                        pltpu.VMEM((GLA_HEADS, GLA_DV, LANES), F32)],
        compiler_params=pltpu.CompilerParams(
            dimension_semantics=("arbitrary",), vmem_limit_bytes=VMEM_LIMIT),
        name="projmix",
    )(x, mod, cos_t, sin_t, w_slab, w_glr, gate_w, gate_b,
      jnp.asarray(dmask), jnp.asarray(qdec), jnp.asarray(kdec), jnp.asarray(csum, BF16),
      ret_norm_w.reshape(1, -1), gla_norm_w.reshape(1, -1))


def _ffn_kernel(x_ref, mix_ref, mod_ref, wo_ref, w1_ref, w2_ref, ln1w_ref, ln1b_ref,
                ln2w_ref, ln2b_ref, out_ref, h_ref, *, alpha):
    x = x_ref[0]
    gate1 = mod_ref[0, 2:3, :]
    shift2 = mod_ref[0, 3:4, :]
    scale2 = mod_ref[0, 4:5, :]
    gate2 = mod_ref[0, 5:6, :]
    m = _dot(mix_ref[0], wo_ref[...])
    x1 = _ln(alpha * x + gate1 * m) * ln1w_ref[...] + ln1b_ref[...]
    u2 = (_ln(x1) * (1.0 + scale2) + shift2).astype(BF16)
    d_ff = w1_ref.shape[1]
    group = 1024
    for j in range(d_ff // group):
        hj = jnp.maximum(_dot(u2, w1_ref[:, j * group:(j + 1) * group]), 0.0)
        h_ref[:, j * group:(j + 1) * group] = (hj * hj).astype(BF16)
    f = _dot(h_ref[...], w2_ref[...])
    out_ref[0] = _ln(alpha * x1 + gate2 * f) * ln2w_ref[...] + ln2b_ref[...]


def _ffn(x, mixed, mod, w_out, w1, w2, ln1_w, ln1_b, ln2_w, ln2_b, alpha, tb):
    bsz, s, d = x.shape
    nb = s // tb
    d_ff = w1.shape[1]
    const2 = lambda b, i: (0, 0)
    row = lambda v: v.reshape(1, d)
    return pl.pallas_call(
        functools.partial(_ffn_kernel, alpha=alpha),
        grid=(bsz, nb),
        in_specs=[pl.BlockSpec((1, tb, d), lambda b, i: (b, i, 0)),
                  pl.BlockSpec((1, tb, d), lambda b, i: (b, i, 0)),
                  pl.BlockSpec((1, 6, d), lambda b, i: (b, 0, 0)),
                  pl.BlockSpec((d, d), const2, pipeline_mode=pl.Buffered(1)),
                  pl.BlockSpec((d, d_ff), const2, pipeline_mode=pl.Buffered(1)),
                  pl.BlockSpec((d_ff, d), const2, pipeline_mode=pl.Buffered(1)),
                  pl.BlockSpec((1, d), const2),
                  pl.BlockSpec((1, d), const2),
                  pl.BlockSpec((1, d), const2),
                  pl.BlockSpec((1, d), const2)],
        out_specs=pl.BlockSpec((1, tb, d), lambda b, i: (b, i, 0)),
        out_shape=jax.ShapeDtypeStruct((bsz, s, d), x.dtype),
        scratch_shapes=[pltpu.VMEM((tb, d_ff), BF16)],
        compiler_params=pltpu.CompilerParams(
            dimension_semantics=("parallel", "parallel"), vmem_limit_bytes=VMEM_LIMIT),
        name="ffn",
    )(x, mixed, mod, w_out, w1, w2, row(ln1_w), row(ln1_b), row(ln2_w), row(ln2_b))


def kernel(x, c, w_ada, b_ada, w_in, ret_norm_w, gla_gate_w, gla_gate_b, gla_norm_w,
           w_out, ln1_w, ln1_b, w_ff1, w_ff2, ln2_w, ln2_b):
    depth = w_in.shape[0]
    bsz, s, d = x.shape
    alpha = (2.0 * depth) ** 0.25
    tb = min(512, s)
    cos_t, sin_t = (jnp.asarray(t) for t in _rotary_tables(s))
    for l in range(depth):
        mod = _adaln(c, w_ada[l], b_ada[l]).reshape(bsz, 6, d)
        w_l = w_in[l].astype(BF16)
        mixed = _projmix(x, mod, cos_t, sin_t, w_l[:, :SLAB], w_l[:, SLAB:],
                         gla_gate_w[l].astype(BF16), gla_gate_b[l].reshape(1, -1),
                         ret_norm_w[l], gla_norm_w[l], tb)
        x = _ffn(x, mixed, mod, w_out[l].astype(BF16), w_ff1[l].astype(BF16), w_ff2[l].astype(BF16),
                 ln1_w[l], ln1_b[l], ln2_w[l], ln2_b[l], alpha, tb)
    return x
```

```python
import functools
import math

import numpy as np
import jax
import jax.numpy as jnp
from jax import lax
from jax.experimental import pallas as pl
from jax.experimental.pallas import tpu as pltpu

CHUNK = 64
RET_HEADS = 4
RET_D = 128
RET_WIDTH = RET_HEADS * RET_D
GLA_HEADS = 4
GLA_DK = 64
GLA_DV = 128
GLA_KEY_WIDTH = GLA_HEADS * GLA_DK
GLA_WIDTH = GLA_HEADS * GLA_DV
D_MIX = RET_WIDTH + GLA_WIDTH
GATE_RANK = 16
GATE_TAU = 16.0
ROPE_BASE = 10000.0
LN_EPS = 1e-5
LANES = 128
SLAB = 4 * RET_WIDTH + 2 * GLA_KEY_WIDTH + 2 * GLA_WIDTH
OFF_RQ, OFF_RK, OFF_RV, OFF_RG = 0, 512, 1024, 1536
OFF_GQ, OFF_GK, OFF_GV, OFF_GG = 2048, 2304, 2560, 3072
MIX_ROWS = 256
VMEM_LIMIT = 56 * 1024 * 1024
LOG2E = math.log2(math.e)

BF16 = jnp.bfloat16
F32 = jnp.float32


def _dot(a, b):
    return jnp.dot(a, b, preferred_element_type=F32)


def _dot_nt(a, b):
    return lax.dot_general(a, b, (((1,), (1,)), ((), ())), preferred_element_type=F32)


def _dot_tn(a, b):
    return lax.dot_general(a, b, (((0,), (0,)), ((), ())), preferred_element_type=F32)


def _ln(x):
    mu = jnp.mean(x, axis=-1, keepdims=True)
    xc = x - mu
    var = jnp.mean(xc * xc, axis=-1, keepdims=True)
    return xc * lax.rsqrt(var + LN_EPS)


def _silu(x):
    return x * (1.0 / (1.0 + jnp.exp(-x)))


def _adaln_kernel(c_ref, w_ref, b_ref, o_ref):
    c = c_ref[...]
    val = _dot(_silu(c).astype(BF16), w_ref[...].astype(BF16)) + b_ref[...]
    o_ref[:, pl.ds(pl.program_id(0), 1), :] = val[:, None, :]


def _adaln(c, w_ada, b_ada, layer):
    bsz, d = c.shape
    depth, _, n = w_ada.shape
    return pl.pallas_call(
        _adaln_kernel,
        grid=(n // d,),
        in_specs=[pl.BlockSpec((bsz, d), lambda j: (0, 0)),
                  pl.BlockSpec((None, d, d), lambda j: (layer, 0, j)),
                  pl.BlockSpec((None, 1, d), lambda j: (layer, 0, j))],
        out_specs=pl.BlockSpec((bsz, n // d, d), lambda j: (0, 0, 0)),
        out_shape=jax.ShapeDtypeStruct((bsz, n // d, d), F32),
        compiler_params=pltpu.CompilerParams(vmem_limit_bytes=VMEM_LIMIT),
        name="adaln",
    )(c, w_ada, b_ada.reshape(depth, 1, n))


def _rotary_tables(s):
    half = RET_D // 2
    inv = 1.0 / (ROPE_BASE ** (np.arange(half, dtype=np.float64) / (half - 1)))
    ang = np.arange(s, dtype=np.float64)[:, None] * inv[None, :]
    cos = np.concatenate([np.cos(ang), np.cos(ang)], axis=1)
    sin = np.concatenate([-np.sin(ang), np.sin(ang)], axis=1)
    return cos.astype(np.float32), sin.astype(np.float32)


def _retention_tables(tb):
    log_gamma = np.log(1.0 - 2.0 ** (-5.0 - np.arange(RET_HEADS, dtype=np.float64)))
    idx = np.arange(tb)
    dist = np.abs(idx[:, None] - idx[None, :])
    visible = (idx[None, :] // CHUNK) <= (idx[:, None] // CHUNK)
    scale = RET_D ** -0.5
    dmask = np.exp(log_gamma[:, None, None] * dist[None]) * visible[None] * scale
    qdec = np.exp(log_gamma[:, None] * (idx[None, :] + 1.0)) * scale
    kdec = np.exp(log_gamma[:, None] * (tb - 1.0 - idx[None, :]))
    qdec = np.broadcast_to(qdec[:, :, None], (RET_HEADS, tb, RET_D))
    kdec = np.broadcast_to(kdec[:, :, None], (RET_HEADS, tb, RET_D))
    block_decay = [float(np.exp(lg * tb)) for lg in log_gamma]
    return (dmask.astype(np.float32), np.ascontiguousarray(qdec, np.float32),
            np.ascontiguousarray(kdec, np.float32), block_decay)


def _cumsum_matrix(tb):
    t = np.arange(tb)
    same_chunk = (t[:, None] // CHUNK) == (t[None, :] // CHUNK)
    return (same_chunk & (t[None, :] <= t[:, None])).astype(np.float32)


def _proj_stage(x_ref, mod_ref, cos_ref, sin_ref, w_ref, gw_ref, gb_ref, slab, la_out):
    x = x_ref[0]
    shift1 = mod_ref[0, 0:1, :]
    scale1 = mod_ref[0, 1:2, :]
    ub = (_ln(x) * (1.0 + scale1) + shift1).astype(BF16)
    yield
    cos = cos_ref[...]
    sin = sin_ref[...]
    group = 256
    for j in range(SLAB // group):
        p = _dot(ub, w_ref[:, j * group:(j + 1) * group])
        if j * group < OFF_RV:
            for h in range(group // RET_D):
                xh = p[:, h * RET_D:(h + 1) * RET_D]
                rot = xh * cos + pltpu.roll(xh, RET_D // 2, 1) * sin
                slab[:, j * group + h * RET_D:j * group + (h + 1) * RET_D] = rot.astype(BF16)
        else:
            slab[:, j * group:(j + 1) * group] = p.astype(BF16)
        yield
    glr = _dot(ub, w_ref[:, SLAB:SLAB + GATE_RANK])
    logit = _dot(glr.astype(BF16), gw_ref[...].astype(BF16)) + gb_ref[...]
    log_sig = jnp.minimum(logit, 0.0) - jnp.log1p(jnp.exp(-jnp.abs(logit)))
    la_out[...] = log_sig * (LOG2E / GATE_TAU)
    yield


def _row_of_block(b, block, row):
    n = b.shape[0] // block
    b3 = b.reshape(n, block, b.shape[1])
    return jnp.broadcast_to(b3[:, row:row + 1, :], b3.shape).reshape(b.shape)


def _mix_stage(slab, la_ref, r0, dmask_ref, qdec_ref, kdec_ref, csum_ref, rnw_ref, gnw_ref,
               out_ref, r_state, g_state, block_decay):
    tb = MIX_ROWS
    blk = slice(r0, r0 + tb)
    for h in range(RET_HEADS):
        q = slab[blk, OFF_RQ + h * RET_D:OFF_RQ + (h + 1) * RET_D]
        k = slab[blk, OFF_RK + h * RET_D:OFF_RK + (h + 1) * RET_D]
        v = slab[blk, OFF_RV + h * RET_D:OFF_RV + (h + 1) * RET_D]
        g = slab[blk, OFF_RG + h * RET_D:OFF_RG + (h + 1) * RET_D].astype(F32)
        p = (_dot_nt(q, k) * dmask_ref[h]).astype(BF16)
        r_prev = r_state[h]
        o = _dot(p, v) + _dot(q, r_prev.astype(BF16)) * qdec_ref[h]
        kd = (k.astype(F32) * kdec_ref[h]).astype(BF16)
        r_state[h] = block_decay[h] * r_prev + _dot_tn(kd, v)
        o = o - jnp.mean(o, axis=-1, keepdims=True)
        o = o * lax.rsqrt(jnp.mean(o * o, axis=-1, keepdims=True) + LN_EPS)
        o = o * rnw_ref[:, h * RET_D:(h + 1) * RET_D] * _silu(g)
        out_ref[0, blk, h * RET_D:(h + 1) * RET_D] = o.astype(out_ref.dtype)
        yield

    la = la_ref[blk, :]
    la_hi = la.astype(BF16)
    la_lo = (la - la_hi.astype(F32)).astype(BF16)
    csum = csum_ref[...]
    b = _dot(csum, la_hi) + _dot(csum, la_lo)
    row = lax.broadcasted_iota(jnp.int32, (tb, GLA_KEY_WIDTH), 0)
    psis = [-jnp.abs(b - _row_of_block(b, 2 * c, c - 1)) for c in (32, 16, 8, 4)]
    la_next = pltpu.roll(la, tb - 1, 0)
    la_prev = pltpu.roll(la, 1, 0)
    r4 = row & 3
    psis.append(jnp.where(r4 == 0, la_next, jnp.where(r4 == 1, 0.0, jnp.where(r4 == 2, la, la + la_prev))))
    psis.append(jnp.where((row & 1) == 1, la, 0.0))
    gq = slab[blk, OFF_GQ:OFF_GQ + GLA_KEY_WIDTH] * jnp.asarray(GLA_DK ** -0.5, BF16)
    gk = slab[blk, OFF_GK:OFF_GK + GLA_KEY_WIDTH]
    phis = [jnp.exp2(p).astype(BF16) for p in psis]
    q_lvl = [gq] + [gq * phi for phi in phis]
    phis = [None] + phis
    q_state = gq * jnp.exp2(b).astype(BF16)
    k_state = gk * jnp.exp2(_row_of_block(b, CHUNK, CHUNK - 1) - b).astype(BF16)
    yield

    lane = lax.broadcasted_iota(jnp.int32, (CHUNK, LANES), 1)
    low_half = lane < GLA_DK
    zero_k = jnp.zeros((CHUNK, LANES), BF16)
    zero_s = jnp.zeros((GLA_DV, LANES), BF16)
    t_xor_s = lax.broadcasted_iota(jnp.int32, (CHUNK, GLA_HEADS * CHUNK), 0) ^ (
        lax.broadcasted_iota(jnp.int32, (CHUNK, GLA_HEADS * CHUNK), 1) & (CHUNK - 1))
    lane_s = lax.broadcasted_iota(jnp.int32, (GLA_DV, LANES), 1)
    low_half_s = lane_s < GLA_DK

    def lane_tiles(piece, tile, zero, n_tiles):
        return jnp.concatenate([piece if j == tile else zero for j in range(n_tiles)], axis=1)

    for ch in range(tb // CHUNK):
        rows = slice(ch * CHUNK, (ch + 1) * CHUNK)
        orow = slice(r0 + ch * CHUNK, r0 + (ch + 1) * CHUNK)
        k_heads = [jnp.where(low_half if h % 2 == 0 else ~low_half,
                             gk[rows, (h // 2) * LANES:(h // 2 + 1) * LANES], zero_k)
                   for h in range(GLA_HEADS)]
        a_lvl = []
        for lvl in range(len(q_lvl)):
            pieces = []
            for h in range(GLA_HEADS):
                tile = h // 2
                kh = k_heads[h]
                if phis[lvl] is not None:
                    kh = kh * phis[lvl][rows, tile * LANES:(tile + 1) * LANES]
                pieces.append(lane_tiles(kh, tile, zero_k, 2))
            k_stack = jnp.concatenate(pieces, axis=0)
            a_lvl.append(_dot_nt(q_lvl[lvl][rows], k_stack))
        a = a_lvl[0]
        for i, c in enumerate((1, 2, 4, 8, 16, 32)):
            a = jnp.where(t_xor_s >= c, a_lvl[len(a_lvl) - 1 - i], a)
        a = a.astype(BF16)
        gv = slab[orow, OFF_GV:OFF_GV + GLA_WIDTH]
        v_bd = jnp.concatenate(
            [lane_tiles(gv[:, h * GLA_DV:(h + 1) * GLA_DV], h, jnp.zeros((CHUNK, GLA_DV), BF16), GLA_HEADS)
             for h in range(GLA_HEADS)], axis=0)
        s_bd = jnp.concatenate(
            [lane_tiles(g_state[h].astype(BF16), h // 2, zero_s, 2) for h in range(GLA_HEADS)],
            axis=0)
        o_all = _dot(a, v_bd) + _dot_nt(q_state[rows], s_bd)
        decay = jnp.exp2(b[ch * CHUNK + CHUNK - 1:(ch + 1) * CHUNK, :])
        for h in range(GLA_HEADS):
            tile = h // 2
            upd = _dot_tn(gv[:, h * GLA_DV:(h + 1) * GLA_DV], k_state[rows, tile * LANES:(tile + 1) * LANES])
            upd = jnp.where(low_half_s if h % 2 == 0 else ~low_half_s, upd, 0.0)
            g_state[h] = g_state[h] * decay[:, tile * LANES:(tile + 1) * LANES] + upd
        for h in range(GLA_HEADS):
            o = o_all[:, h * GLA_DV:(h + 1) * GLA_DV]
            o = o * lax.rsqrt(jnp.mean(o * o, axis=-1, keepdims=True) + LN_EPS)
            gg = slab[orow, OFF_GG + h * GLA_DV:OFF_GG + (h + 1) * GLA_DV].astype(F32)
            o = o * gnw_ref[:, h * GLA_DV:(h + 1) * GLA_DV] * _silu(gg)
            out_ref[0, orow, RET_WIDTH + h * GLA_DV:RET_WIDTH + (h + 1) * GLA_DV] = o.astype(out_ref.dtype)
        yield


def _interleave(gen_a, n_a, gen_b, n_b):
    done_a = done_b = 0
    while done_a < n_a or done_b < n_b:
        if done_b >= n_b or (done_a < n_a and (done_a + 0.5) * n_b <= (done_b + 0.5) * n_a):
            next(gen_a)
            done_a += 1
        else:
            next(gen_b)
            done_b += 1
    for g in (gen_a, gen_b):
        assert next(g, "done") == "done", "piece count mismatch"


def _projmix_kernel(x_ref, mod_ref, cos_ref, sin_ref, w_ref, gw_ref, gb_ref,
                    dmask_ref, qdec_ref, kdec_ref, csum_ref, rnw_ref, gnw_ref,
                    wo_ref, w1_ref, w2_ref,
                    out_ref, wo_bf_ref, w1_bf_ref, w2_bf_ref,
                    w_bf, slab0, slab1, la0, la1, r_state, g_state,
                    *, tb, blocks_per_seq, block_decay):
    j = pl.program_id(0)

    @pl.when(j == 0)
    def _():
        slab1[...] = jnp.zeros_like(slab1)
        la1[...] = jnp.zeros_like(la1)
        w_bf[...] = w_ref[...].astype(BF16)

    wo_bf_ref[...] = wo_ref[...].astype(BF16)
    w1_bf_ref[...] = w1_ref[...].astype(BF16)
    w2_bf_ref[...] = w2_ref[...].astype(BF16)

    @pl.when((j == 0) | ((j - 1) % blocks_per_seq == 0))
    def _():
        r_state[...] = jnp.zeros_like(r_state)
        g_state[...] = jnp.zeros_like(g_state)

    def mix_all(slab_r, la_r):
        for r0 in range(0, tb, MIX_ROWS):
            yield from _mix_stage(slab_r, la_r, r0, dmask_ref, qdec_ref, kdec_ref, csum_ref,
                                  rnw_ref, gnw_ref, out_ref, r_state, g_state, block_decay)

    def step(slab_w, la_w, slab_r, la_r):
        proj = _proj_stage(x_ref, mod_ref, cos_ref, sin_ref, w_bf, gw_ref, gb_ref, slab_w, la_w)
        n_proj = 2 + SLAB // 256
        n_mix = (tb // MIX_ROWS) * (RET_HEADS + 1 + MIX_ROWS // CHUNK)
        _interleave(proj, n_proj, mix_all(slab_r, la_r), n_mix)

    @pl.when(j % 2 == 0)
    def _():
        step(slab0, la0, slab1, la1)

    @pl.when(j % 2 == 1)
    def _():
        step(slab1, la1, slab0, la0)


def _layer_row(v):
    return v.reshape(v.shape[0], 1, v.shape[1])


def _projmix(x, mod, cos_t, sin_t, w_in, gate_w, gate_b, ret_norm_w, gla_norm_w,
             w_out, w_ff1, w_ff2, layer, tb):
    bsz, s, d = x.shape
    d_ff = w_ff1.shape[2]
    bps = s // tb
    nblk = bsz * bps
    dmask, qdec, kdec, block_decay = _retention_tables(MIX_ROWS)
    csum = _cumsum_matrix(MIX_ROWS)
    const2 = lambda j: (0, 0)
    const3 = lambda j: (0, 0, 0)
    of_layer = lambda j: (layer, 0, 0)
    proj_blk = lambda j: jnp.minimum(j, nblk - 1)
    mix_blk = lambda j: jnp.maximum(j - 1, 0)
    single = dict(pipeline_mode=pl.Buffered(1))
    wo_rows, w1_rows, w2_rows = d // nblk, d // nblk, d_ff // nblk
    return pl.pallas_call(
        functools.partial(_projmix_kernel, tb=tb, blocks_per_seq=bps, block_decay=block_decay),
        grid=(nblk + 1,),
        in_specs=[pl.BlockSpec((1, tb, d), lambda j: (proj_blk(j) // bps, proj_blk(j) % bps, 0)),
                  pl.BlockSpec((1, 6, d), lambda j: (proj_blk(j) // bps, 0, 0)),
                  pl.BlockSpec((tb, RET_D), lambda j: (proj_blk(j) % bps, 0)),
                  pl.BlockSpec((tb, RET_D), lambda j: (proj_blk(j) % bps, 0)),
                  pl.BlockSpec((None,) + w_in.shape[1:], of_layer, **single),
                  pl.BlockSpec((None, GATE_RANK, GLA_KEY_WIDTH), of_layer, **single),
                  pl.BlockSpec((None, 1, GLA_KEY_WIDTH), of_layer, **single),
                  pl.BlockSpec(dmask.shape, const3, **single),
                  pl.BlockSpec(qdec.shape, const3, **single),
                  pl.BlockSpec(kdec.shape, const3, **single),
                  pl.BlockSpec(csum.shape, const2, **single),
                  pl.BlockSpec((None, 1, RET_WIDTH), of_layer, **single),
                  pl.BlockSpec((None, 1, GLA_WIDTH), of_layer, **single),
                  pl.BlockSpec((None, wo_rows, d), lambda j: (layer, proj_blk(j), 0)),
                  pl.BlockSpec((None, w1_rows, d_ff), lambda j: (layer, proj_blk(j), 0)),
                  pl.BlockSpec((None, w2_rows, d), lambda j: (layer, proj_blk(j), 0))],
        out_specs=[pl.BlockSpec((1, tb, D_MIX), lambda j: (mix_blk(j) // bps, mix_blk(j) % bps, 0)),
                   pl.BlockSpec((wo_rows, d), lambda j: (proj_blk(j), 0)),
                   pl.BlockSpec((w1_rows, d_ff), lambda j: (proj_blk(j), 0)),
                   pl.BlockSpec((w2_rows, d), lambda j: (proj_blk(j), 0))],
        out_shape=[jax.ShapeDtypeStruct((bsz, s, D_MIX), BF16),
                   jax.ShapeDtypeStruct((d, d), BF16),
                   jax.ShapeDtypeStruct((d, d_ff), BF16),
                   jax.ShapeDtypeStruct((d_ff, d), BF16)],
        scratch_shapes=[pltpu.VMEM(w_in.shape[1:], BF16),
                        pltpu.VMEM((tb, SLAB), BF16), pltpu.VMEM((tb, SLAB), BF16),
                        pltpu.VMEM((tb, GLA_KEY_WIDTH), F32), pltpu.VMEM((tb, GLA_KEY_WIDTH), F32),
                        pltpu.VMEM((RET_HEADS, RET_D, RET_D), F32),
                        pltpu.VMEM((GLA_HEADS, GLA_DV, LANES), F32)],
        compiler_params=pltpu.CompilerParams(
            dimension_semantics=("arbitrary",), vmem_limit_bytes=VMEM_LIMIT),
        name="projmix",
    )(x, mod, cos_t, sin_t, w_in, gate_w, _layer_row(gate_b),
      jnp.asarray(dmask), jnp.asarray(qdec), jnp.asarray(kdec), jnp.asarray(csum, BF16),
      _layer_row(ret_norm_w), _layer_row(gla_norm_w), w_out, w_ff1, w_ff2)


def _ffn_kernel(x_ref, mix_ref, mod_ref, wo_ref, w1_ref, w2_ref, ln1w_ref, ln1b_ref,
                ln2w_ref, ln2b_ref, out_ref, h_ref, *, alpha):
    gate1 = mod_ref[0, 2:3, :]
    shift2 = mod_ref[0, 3:4, :]
    scale2 = mod_ref[0, 4:5, :]
    gate2 = mod_ref[0, 5:6, :]
    d_ff = w1_ref.shape[1]
    group = 1024

    def rows_stage(rows):
        m = _dot(mix_ref[0, rows, :], wo_ref[...])
        yield
        x1 = _ln(alpha * x_ref[0, rows, :] + gate1 * m) * ln1w_ref[...] + ln1b_ref[...]
        u2 = (_ln(x1) * (1.0 + scale2) + shift2).astype(BF16)
        yield
        for j in range(d_ff // group):
            hj = jnp.maximum(_dot(u2, w1_ref[:, j * group:(j + 1) * group]), 0.0)
            h_ref[rows, j * group:(j + 1) * group] = (hj * hj).astype(BF16)
            yield
        f = _dot(h_ref[rows, :], w2_ref[...])
        yield
        out_ref[0, rows, :] = _ln(alpha * x1 + gate2 * f) * ln2w_ref[...] + ln2b_ref[...]
        yield

    half = x_ref.shape[1] // 2
    n_pieces = 4 + d_ff // group
    _interleave(rows_stage(slice(0, half)), n_pieces, rows_stage(slice(half, 2 * half)), n_pieces)


def _ffn(x, mixed, mod, w_out, w1, w2, ln1_w, ln1_b, ln2_w, ln2_b, layer, alpha, tb):
    bsz, s, d = x.shape
    nb = s // tb
    d_ff = w1.shape[1]
    const2 = lambda b, i: (0, 0)
    of_layer = lambda b, i: (layer, 0, 0)
    return pl.pallas_call(
        functools.partial(_ffn_kernel, alpha=alpha),
        grid=(bsz, nb),
        in_specs=[pl.BlockSpec((1, tb, d), lambda b, i: (b, i, 0)),
                  pl.BlockSpec((1, tb, d), lambda b, i: (b, i, 0)),
                  pl.BlockSpec((1, 6, d), lambda b, i: (b, 0, 0)),
                  pl.BlockSpec((d, d), const2, pipeline_mode=pl.Buffered(1)),
                  pl.BlockSpec((d, d_ff), const2, pipeline_mode=pl.Buffered(1)),
                  pl.BlockSpec((d_ff, d), const2, pipeline_mode=pl.Buffered(1)),
                  pl.BlockSpec((None, 1, d), of_layer),
                  pl.BlockSpec((None, 1, d), of_layer),
                  pl.BlockSpec((None, 1, d), of_layer),
                  pl.BlockSpec((None, 1, d), of_layer)],
        out_specs=pl.BlockSpec((1, tb, d), lambda b, i: (b, i, 0)),
        out_shape=jax.ShapeDtypeStruct((bsz, s, d), x.dtype),
        scratch_shapes=[pltpu.VMEM((tb, d_ff), BF16)],
        compiler_params=pltpu.CompilerParams(
            dimension_semantics=("parallel", "parallel"), vmem_limit_bytes=VMEM_LIMIT),
        name="ffn",
    )(x, mixed, mod, w_out, w1, w2, _layer_row(ln1_w), _layer_row(ln1_b), _layer_row(ln2_w),
      _layer_row(ln2_b))


def kernel(x, c, w_ada, b_ada, w_in, ret_norm_w, gla_gate_w, gla_gate_b, gla_norm_w,
           w_out, ln1_w, ln1_b, w_ff1, w_ff2, ln2_w, ln2_b):
    depth = w_in.shape[0]
    bsz, s, d = x.shape
    alpha = (2.0 * depth) ** 0.25
    tb = min(512, s)
    cos_t, sin_t = (jnp.asarray(t) for t in _rotary_tables(s))
    for l in range(depth):
        mod = _adaln(c, w_ada, b_ada, l)
        mixed, wo_bf, w1_bf, w2_bf = _projmix(x, mod, cos_t, sin_t, w_in, gla_gate_w, gla_gate_b,
                                              ret_norm_w, gla_norm_w, w_out, w_ff1, w_ff2, l, tb)
        x = _ffn(x, mixed, mod, wo_bf, w1_bf, w2_bf, ln1_w, ln1_b, ln2_w, ln2_b, l, alpha, tb)
    return x
```

```python
import functools
import math

import numpy as np
import jax
import jax.numpy as jnp
from jax import lax
from jax.experimental import pallas as pl
from jax.experimental.pallas import tpu as pltpu

CHUNK = 64
RET_HEADS = 4
RET_D = 128
RET_WIDTH = RET_HEADS * RET_D
GLA_HEADS = 4
GLA_DK = 64
GLA_DV = 128
GLA_KEY_WIDTH = GLA_HEADS * GLA_DK
GLA_WIDTH = GLA_HEADS * GLA_DV
D_MIX = RET_WIDTH + GLA_WIDTH
GATE_RANK = 16
GATE_TAU = 16.0
ROPE_BASE = 10000.0
LN_EPS = 1e-5
LANES = 128
SLAB = 4 * RET_WIDTH + 2 * GLA_KEY_WIDTH + 2 * GLA_WIDTH
OFF_RQ, OFF_RK, OFF_RV, OFF_RG = 0, 512, 1024, 1536
OFF_GQ, OFF_GK, OFF_GV, OFF_GG = 2048, 2304, 2560, 3072
MIX_ROWS = 256
VMEM_LIMIT = 56 * 1024 * 1024
LOG2E = math.log2(math.e)

BF16 = jnp.bfloat16
F32 = jnp.float32


def _dot(a, b):
    return jnp.dot(a, b, preferred_element_type=F32)


def _dot_nt(a, b):
    return lax.dot_general(a, b, (((1,), (1,)), ((), ())), preferred_element_type=F32)


def _dot_tn(a, b):
    return lax.dot_general(a, b, (((0,), (0,)), ((), ())), preferred_element_type=F32)


def _zero_of(v):
    bits = lax.shift_right_logical(pltpu.bitcast(v, jnp.uint32), jnp.uint32(32))
    return pltpu.bitcast(bits, F32)


def _ln(x, not_before=None):
    mu = jnp.mean(x, axis=-1, keepdims=True)
    if not_before is not None:
        mu = mu + _zero_of(not_before)[0:1, 0:1]
    xc = x - mu
    var = jnp.mean(xc * xc, axis=-1, keepdims=True)
    return xc * lax.rsqrt(var + LN_EPS)


def _silu(x):
    return x * (1.0 / (1.0 + jnp.exp(-x)))


def _adaln_kernel(c_ref, w_ref, b_ref, o_ref):
    c = c_ref[...]
    val = _dot(_silu(c).astype(BF16), w_ref[...].astype(BF16)) + b_ref[...]
    o_ref[:, pl.ds(pl.program_id(0), 1), :] = val[:, None, :]


def _adaln(c, w_ada, b_ada, layer):
    bsz, d = c.shape
    depth, _, n = w_ada.shape
    return pl.pallas_call(
        _adaln_kernel,
        grid=(n // d,),
        in_specs=[pl.BlockSpec((bsz, d), lambda j: (0, 0)),
                  pl.BlockSpec((None, d, d), lambda j: (layer, 0, j)),
                  pl.BlockSpec((None, 1, d), lambda j: (layer, 0, j))],
        out_specs=pl.BlockSpec((bsz, n // d, d), lambda j: (0, 0, 0)),
        out_shape=jax.ShapeDtypeStruct((bsz, n // d, d), F32),
        compiler_params=pltpu.CompilerParams(vmem_limit_bytes=VMEM_LIMIT),
        name="adaln",
    )(c, w_ada, b_ada.reshape(depth, 1, n))


def _rotary_tables(s):
    half = RET_D // 2
    inv = 1.0 / (ROPE_BASE ** (np.arange(half, dtype=np.float64) / (half - 1)))
    ang = np.arange(s, dtype=np.float64)[:, None] * inv[None, :]
    cos = np.concatenate([np.cos(ang), np.cos(ang)], axis=1)
    sin = np.concatenate([-np.sin(ang), np.sin(ang)], axis=1)
    return cos.astype(np.float32), sin.astype(np.float32)


def _retention_tables(tb):
    log_gamma = np.log(1.0 - 2.0 ** (-5.0 - np.arange(RET_HEADS, dtype=np.float64)))
    idx = np.arange(tb)
    dist = np.abs(idx[:, None] - idx[None, :])
    visible = (idx[None, :] // CHUNK) <= (idx[:, None] // CHUNK)
    scale = RET_D ** -0.5
    dmask = np.exp(log_gamma[:, None, None] * dist[None]) * visible[None] * scale
    qdec = np.exp(log_gamma[:, None] * (idx[None, :] + 1.0)) * scale
    kdec = np.exp(log_gamma[:, None] * (tb - 1.0 - idx[None, :]))
    qdec = np.broadcast_to(qdec[:, :, None], (RET_HEADS, tb, RET_D))
    kdec = np.broadcast_to(kdec[:, :, None], (RET_HEADS, tb, RET_D))
    block_decay = [float(np.exp(lg * tb)) for lg in log_gamma]
    return (dmask.astype(np.float32), np.ascontiguousarray(qdec, np.float32),
            np.ascontiguousarray(kdec, np.float32), block_decay)


def _cumsum_matrix(tb):
    t = np.arange(tb)
    same_chunk = (t[:, None] // CHUNK) == (t[None, :] // CHUNK)
    return (same_chunk & (t[None, :] <= t[:, None])).astype(np.float32)


def _proj_stage(x_ref, mod_ref, cos_ref, sin_ref, w_ref, wtail_ref, gw_ref, gb_ref, slab, la_out):
    x = x_ref[0]
    shift1 = mod_ref[0, 0:1, :]
    scale1 = mod_ref[0, 1:2, :]
    ub = (_ln(x) * (1.0 + scale1) + shift1).astype(BF16)
    yield
    cos = cos_ref[...]
    sin = sin_ref[...]
    group = 256
    for j in range(SLAB // group):
        p = _dot(ub, w_ref[:, j * group:(j + 1) * group])
        if j * group < OFF_RV:
            for h in range(group // RET_D):
                xh = p[:, h * RET_D:(h + 1) * RET_D]
                rot = xh * cos + pltpu.roll(xh, RET_D // 2, 1) * sin
                slab[:, j * group + h * RET_D:j * group + (h + 1) * RET_D] = rot.astype(BF16)
        else:
            slab[:, j * group:(j + 1) * group] = p.astype(BF16)
        yield
    glr = _dot_nt(ub, wtail_ref[...])
    logit = _dot(glr.astype(BF16), gw_ref[...].astype(BF16)) + gb_ref[...]
    log_sig = jnp.minimum(logit, 0.0) - jnp.log1p(jnp.exp(-jnp.abs(logit)))
    la_out[...] = log_sig * (LOG2E / GATE_TAU)
    yield


def _row_of_block(b, block, row):
    n = b.shape[0] // block
    b3 = b.reshape(n, block, b.shape[1])
    return jnp.broadcast_to(b3[:, row:row + 1, :], b3.shape).reshape(b.shape)


def _mix_stage(slab, la_ref, r0, dmask_ref, qdec_ref, kdec_ref, csum_ref, rnw_ref, gnw_ref,
               out_ref, r_state, g_state, block_decay):
    tb = MIX_ROWS
    blk = slice(r0, r0 + tb)
    for h in range(RET_HEADS):
        q = slab[blk, OFF_RQ + h * RET_D:OFF_RQ + (h + 1) * RET_D]
        k = slab[blk, OFF_RK + h * RET_D:OFF_RK + (h + 1) * RET_D]
        v = slab[blk, OFF_RV + h * RET_D:OFF_RV + (h + 1) * RET_D]
        g = slab[blk, OFF_RG + h * RET_D:OFF_RG + (h + 1) * RET_D].astype(F32)
        p = (_dot_nt(q, k) * dmask_ref[h]).astype(BF16)
        r_prev = r_state[h]
        o = _dot(p, v) + _dot(q, r_prev.astype(BF16)) * qdec_ref[h]
        kd = (k.astype(F32) * kdec_ref[h]).astype(BF16)
        r_state[h] = block_decay[h] * r_prev + _dot_tn(kd, v)
        o = o - jnp.mean(o, axis=-1, keepdims=True)
        o = o * lax.rsqrt(jnp.mean(o * o, axis=-1, keepdims=True) + LN_EPS)
        o = o * rnw_ref[:, h * RET_D:(h + 1) * RET_D] * _silu(g)
        out_ref[0, blk, h * RET_D:(h + 1) * RET_D] = o.astype(out_ref.dtype)
        yield

    la = la_ref[blk, :]
    la_hi = la.astype(BF16)
    la_lo = (la - la_hi.astype(F32)).astype(BF16)
    csum = csum_ref[...]
    b = _dot(csum, la_hi) + _dot(csum, la_lo)
    row = lax.broadcasted_iota(jnp.int32, (tb, GLA_KEY_WIDTH), 0)
    psis = [-jnp.abs(b - _row_of_block(b, 2 * c, c - 1)) for c in (32, 16, 8, 4)]
    la_next = pltpu.roll(la, tb - 1, 0)
    la_prev = pltpu.roll(la, 1, 0)
    r4 = row & 3
    psis.append(jnp.where(r4 == 0, la_next, jnp.where(r4 == 1, 0.0, jnp.where(r4 == 2, la, la + la_prev))))
    psis.append(jnp.where((row & 1) == 1, la, 0.0))
    gq = slab[blk, OFF_GQ:OFF_GQ + GLA_KEY_WIDTH] * jnp.asarray(GLA_DK ** -0.5, BF16)
    gk = slab[blk, OFF_GK:OFF_GK + GLA_KEY_WIDTH]
    phis = [jnp.exp2(p).astype(BF16) for p in psis]
    q_lvl = [gq] + [gq * phi for phi in phis]
    phis = [None] + phis
    q_state = gq * jnp.exp2(b).astype(BF16)
    k_state = gk * jnp.exp2(_row_of_block(b, CHUNK, CHUNK - 1) - b).astype(BF16)
    yield

    lane = lax.broadcasted_iota(jnp.int32, (CHUNK, LANES), 1)
    low_half = lane < GLA_DK
    zero_k = jnp.zeros((CHUNK, LANES), BF16)
    zero_s = jnp.zeros((GLA_DV, LANES), BF16)
    t_xor_s = lax.broadcasted_iota(jnp.int32, (CHUNK, GLA_HEADS * CHUNK), 0) ^ (
        lax.broadcasted_iota(jnp.int32, (CHUNK, GLA_HEADS * CHUNK), 1) & (CHUNK - 1))
    lane_s = lax.broadcasted_iota(jnp.int32, (GLA_DV, LANES), 1)
    low_half_s = lane_s < GLA_DK

    def lane_tiles(piece, tile, zero, n_tiles):
        return jnp.concatenate([piece if j == tile else zero for j in range(n_tiles)], axis=1)

    for ch in range(tb // CHUNK):
        rows = slice(ch * CHUNK, (ch + 1) * CHUNK)
        orow = slice(r0 + ch * CHUNK, r0 + (ch + 1) * CHUNK)
        k_heads = [jnp.where(low_half if h % 2 == 0 else ~low_half,
                             gk[rows, (h // 2) * LANES:(h // 2 + 1) * LANES], zero_k)
                   for h in range(GLA_HEADS)]
        a_lvl = []
        for lvl in range(len(q_lvl)):
            pieces = []
            for h in range(GLA_HEADS):
                tile = h // 2
                kh = k_heads[h]
                if phis[lvl] is not None:
                    kh = kh * phis[lvl][rows, tile * LANES:(tile + 1) * LANES]
                pieces.append(lane_tiles(kh, tile, zero_k, 2))
            k_stack = jnp.concatenate(pieces, axis=0)
            a_lvl.append(_dot_nt(q_lvl[lvl][rows], k_stack))
        a = a_lvl[0]
        for i, c in enumerate((1, 2, 4, 8, 16, 32)):
            a = jnp.where(t_xor_s >= c, a_lvl[len(a_lvl) - 1 - i], a)
        a = a.astype(BF16)
        gv = slab[orow, OFF_GV:OFF_GV + GLA_WIDTH]
        v_bd = jnp.concatenate(
            [lane_tiles(gv[:, h * GLA_DV:(h + 1) * GLA_DV], h, jnp.zeros((CHUNK, GLA_DV), BF16), GLA_HEADS)
             for h in range(GLA_HEADS)], axis=0)
        s_bd = jnp.concatenate(
            [lane_tiles(g_state[h].astype(BF16), h // 2, zero_s, 2) for h in range(GLA_HEADS)],
            axis=0)
        o_all = _dot(a, v_bd) + _dot_nt(q_state[rows], s_bd)
        decay = jnp.exp2(b[ch * CHUNK + CHUNK - 1:(ch + 1) * CHUNK, :])
        for h in range(GLA_HEADS):
            tile = h // 2
            upd = _dot_tn(gv[:, h * GLA_DV:(h + 1) * GLA_DV], k_state[rows, tile * LANES:(tile + 1) * LANES])
            upd = jnp.where(low_half_s if h % 2 == 0 else ~low_half_s, upd, 0.0)
            g_state[h] = g_state[h] * decay[:, tile * LANES:(tile + 1) * LANES] + upd
        for h in range(GLA_HEADS):
            o = o_all[:, h * GLA_DV:(h + 1) * GLA_DV]
            o = o * lax.rsqrt(jnp.mean(o * o, axis=-1, keepdims=True) + LN_EPS)
            gg = slab[orow, OFF_GG + h * GLA_DV:OFF_GG + (h + 1) * GLA_DV].astype(F32)
            o = o * gnw_ref[:, h * GLA_DV:(h + 1) * GLA_DV] * _silu(gg)
            out_ref[0, orow, RET_WIDTH + h * GLA_DV:RET_WIDTH + (h + 1) * GLA_DV] = o.astype(out_ref.dtype)
        yield


def _interleave(gen_a, n_a, gen_b, n_b):
    done_a = done_b = 0
    while done_a < n_a or done_b < n_b:
        if done_b >= n_b or (done_a < n_a and (done_a + 0.5) * n_b <= (done_b + 0.5) * n_a):
            next(gen_a)
            done_a += 1
        else:
            next(gen_b)
            done_b += 1
    for g in (gen_a, gen_b):
        assert next(g, "done") == "done", "piece count mismatch"


def _projmix_kernel(x_ref, mod_ref, cos_ref, sin_ref, w_ref, gw_ref, gb_ref,
                    dmask_ref, qdec_ref, kdec_ref, csum_ref, rnw_ref, gnw_ref,
                    wo_ref, w1_ref, w2_ref,
                    out_ref, wo_bf_ref, w1_bf_ref, w2_bf_ref,
                    w_bf, w_tail, slab0, slab1, la0, la1, r_state, g_state,
                    *, tb, blocks_per_seq, block_decay):
    j = pl.program_id(0)

    @pl.when(j == 0)
    def _():
        slab1[...] = jnp.zeros_like(slab1)
        la1[...] = jnp.zeros_like(la1)
        for g in range(0, SLAB, 256):
            w_bf[:, g:g + 256] = w_ref[g:g + 256, :].T.astype(BF16)
        w_tail[...] = w_ref[SLAB:SLAB + GATE_RANK, :].astype(BF16)

    wo_bf_ref[...] = wo_ref[...].astype(BF16)
    w1_bf_ref[...] = w1_ref[...].astype(BF16)
    w2_bf_ref[...] = w2_ref[...].astype(BF16)

    @pl.when((j == 0) | ((j - 1) % blocks_per_seq == 0))
    def _():
        r_state[...] = jnp.zeros_like(r_state)
        g_state[...] = jnp.zeros_like(g_state)

    def mix_all(slab_r, la_r):
        for r0 in range(0, tb, MIX_ROWS):
            yield from _mix_stage(slab_r, la_r, r0, dmask_ref, qdec_ref, kdec_ref, csum_ref,
                                  rnw_ref, gnw_ref, out_ref, r_state, g_state, block_decay)

    def step(slab_w, la_w, slab_r, la_r):
        proj = _proj_stage(x_ref, mod_ref, cos_ref, sin_ref, w_bf, w_tail, gw_ref, gb_ref, slab_w, la_w)
        n_proj = 2 + SLAB // 256
        n_mix = (tb // MIX_ROWS) * (RET_HEADS + 1 + MIX_ROWS // CHUNK)
        _interleave(proj, n_proj, mix_all(slab_r, la_r), n_mix)

    @pl.when(j % 2 == 0)
    def _():
        step(slab0, la0, slab1, la1)

    @pl.when(j % 2 == 1)
    def _():
        step(slab1, la1, slab0, la0)


def _layer_row(v):
    return v.reshape(v.shape[0], 1, v.shape[1])


def _projmix(x, mod, cos_t, sin_t, w_in, gate_w, gate_b, ret_norm_w, gla_norm_w,
             w_out, w_ff1, w_ff2, layer, tb):
    bsz, s, d = x.shape
    d_ff = w_ff1.shape[2]
    bps = s // tb
    nblk = bsz * bps
    dmask, qdec, kdec, block_decay = _retention_tables(MIX_ROWS)
    csum = _cumsum_matrix(MIX_ROWS)
    const2 = lambda j: (0, 0)
    const3 = lambda j: (0, 0, 0)
    of_layer = lambda j: (layer, 0, 0)
    proj_blk = lambda j: jnp.minimum(j, nblk - 1)
    mix_blk = lambda j: jnp.maximum(j - 1, 0)
    single = dict(pipeline_mode=pl.Buffered(1))
    wo_rows, w1_rows, w2_rows = d // nblk, d // nblk, d_ff // nblk
    return pl.pallas_call(
        functools.partial(_projmix_kernel, tb=tb, blocks_per_seq=bps, block_decay=block_decay),
        grid=(nblk + 1,),
        in_specs=[pl.BlockSpec((1, tb, d), lambda j: (proj_blk(j) // bps, proj_blk(j) % bps, 0)),
                  pl.BlockSpec((1, 6, d), lambda j: (proj_blk(j) // bps, 0, 0)),
                  pl.BlockSpec((tb, RET_D), lambda j: (proj_blk(j) % bps, 0)),
                  pl.BlockSpec((tb, RET_D), lambda j: (proj_blk(j) % bps, 0)),
                  pl.BlockSpec((None, SLAB + GATE_RANK, d), of_layer, **single),
                  pl.BlockSpec((None, GATE_RANK, GLA_KEY_WIDTH), of_layer, **single),
                  pl.BlockSpec((None, 1, GLA_KEY_WIDTH), of_layer, **single),
                  pl.BlockSpec(dmask.shape, const3, **single),
                  pl.BlockSpec(qdec.shape, const3, **single),
                  pl.BlockSpec(kdec.shape, const3, **single),
                  pl.BlockSpec(csum.shape, const2, **single),
                  pl.BlockSpec((None, 1, RET_WIDTH), of_layer, **single),
                  pl.BlockSpec((None, 1, GLA_WIDTH), of_layer, **single),
                  pl.BlockSpec((None, wo_rows, d), lambda j: (layer, proj_blk(j), 0)),
                  pl.BlockSpec((None, w1_rows, d_ff), lambda j: (layer, proj_blk(j), 0)),
                  pl.BlockSpec((None, w2_rows, d), lambda j: (layer, proj_blk(j), 0))],
        out_specs=[pl.BlockSpec((1, tb, D_MIX), lambda j: (mix_blk(j) // bps, mix_blk(j) % bps, 0)),
                   pl.BlockSpec((wo_rows, d), lambda j: (proj_blk(j), 0)),
                   pl.BlockSpec((w1_rows, d_ff), lambda j: (proj_blk(j), 0)),
                   pl.BlockSpec((w2_rows, d), lambda j: (proj_blk(j), 0))],
        out_shape=[jax.ShapeDtypeStruct((bsz, s, D_MIX), BF16),
                   jax.ShapeDtypeStruct((d, d), BF16),
                   jax.ShapeDtypeStruct((d, d_ff), BF16),
                   jax.ShapeDtypeStruct((d_ff, d), BF16)],
        scratch_shapes=[pltpu.VMEM((d, SLAB), BF16), pltpu.VMEM((GATE_RANK, d), BF16),
                        pltpu.VMEM((tb, SLAB), BF16), pltpu.VMEM((tb, SLAB), BF16),
                        pltpu.VMEM((tb, GLA_KEY_WIDTH), F32), pltpu.VMEM((tb, GLA_KEY_WIDTH), F32),
                        pltpu.VMEM((RET_HEADS, RET_D, RET_D), F32),
                        pltpu.VMEM((GLA_HEADS, GLA_DV, LANES), F32)],
        compiler_params=pltpu.CompilerParams(
            dimension_semantics=("arbitrary",), vmem_limit_bytes=VMEM_LIMIT),
        name="projmix",
    )(x, mod, cos_t, sin_t, jnp.swapaxes(w_in, 1, 2), gate_w, _layer_row(gate_b),
      jnp.asarray(dmask), jnp.asarray(qdec), jnp.asarray(kdec), jnp.asarray(csum, BF16),
      _layer_row(ret_norm_w), _layer_row(gla_norm_w), w_out, w_ff1, w_ff2)


def _ffn_kernel(x_ref, mix_ref, mod_ref, wo_ref, w1_ref, w2_ref, ln1w_ref, ln1b_ref,
                ln2w_ref, ln2b_ref, out_ref, h_ref, x1_s, y_s, *, alpha, nblk):
    i = pl.program_id(0)
    gate1 = mod_ref[0, 2:3, :]
    shift2 = mod_ref[0, 3:4, :]
    scale2 = mod_ref[0, 4:5, :]
    gate2 = mod_ref[0, 5:6, :]
    d_ff = w1_ref.shape[1]
    n_groups = 4
    group = d_ff // n_groups
    tb = x_ref.shape[1]
    halves = (slice(0, tb // 2), slice(tb // 2, tb))
    quarters = [slice(q * tb // 4, (q + 1) * tb // 4) for q in range(4)]

    @pl.when(i == 0)
    def _():
        y_s[...] = jnp.zeros_like(y_s)

    def final_norm(rows, not_before=None):
        o = _ln(y_s[rows, :], not_before) * ln2w_ref[...] + ln2b_ref[...]
        out_ref[0, rows, :] = o
        return _zero_of(jnp.sum(o.reshape(-1, 16, o.shape[-1]), axis=0)).astype(BF16)

    def chain(rows, after, done):
        m = _dot(mix_ref[0, rows, :], wo_ref[...])
        yield
        x1 = (_ln(alpha * x_ref[0, rows, :] + gate1 * m, after.get("ln1")) * ln1w_ref[...]
              + ln1b_ref[...])
        x1_s[rows, :] = x1
        u2f = _ln(x1) * (1.0 + scale2) + shift2
        done["ln1"] = u2f[u2f.shape[0] - 8:, u2f.shape[1] - LANES:]
        u2 = u2f.astype(BF16)
        yield
        for j in range(n_groups):
            lhs = u2
            if j in after:
                lhs = jnp.concatenate([u2[:16] + after[j], u2[16:]], axis=0)
            hj = jnp.maximum(_dot(lhs, w1_ref[:, j * group:(j + 1) * group]), 0.0)
            done[j] = hj[hj.shape[0] - 8:, group - LANES:]
            h_ref[rows, j * group:(j + 1) * group] = (hj * hj).astype(BF16)
            yield
        y_s[rows, :] = alpha * x1_s[rows, :] + gate2 * _dot(h_ref[rows, :], w2_ref[...])
        yield

    @pl.when(i < nblk)
    def _():
        after_a, after_b, done_a, done_b = {}, {}, {}, {}
        half_a, half_b = chain(halves[0], after_a, done_a), chain(halves[1], after_b, done_b)
        next(half_a)
        next(half_b)
        next(half_a)
        next(half_a)
        after_b["ln1"] = done_a["ln1"]
        next(half_b)
        next(half_a)
        next(half_b)
        after_a[2] = final_norm(quarters[0], done_a[0])
        next(half_a)
        after_b[1] = final_norm(quarters[1], done_a[1])
        next(half_b)
        after_a[3] = final_norm(quarters[2], done_b[0])
        next(half_a)
        after_b[2] = final_norm(quarters[3], done_a[2])
        next(half_b)
        next(half_a)
        next(half_b)
        next(half_b)
        for g in (half_a, half_b):
            assert next(g, "done") == "done", "piece count mismatch"

    @pl.when(i == nblk)
    def _():
        for rows in quarters:
            final_norm(rows)


def _ffn(x, mixed, mod, w_out, w1, w2, ln1_w, ln1_b, ln2_w, ln2_b, layer, alpha, tb):
    bsz, s, d = x.shape
    bps = s // tb
    nblk = bsz * bps
    d_ff = w1.shape[1]
    const2 = lambda i: (0, 0)
    of_layer = lambda i: (layer, 0, 0)
    cur = lambda i: jnp.minimum(i, nblk - 1)
    prev = lambda i: jnp.maximum(i - 1, 0)
    single = dict(pipeline_mode=pl.Buffered(1))
    return pl.pallas_call(
        functools.partial(_ffn_kernel, alpha=alpha, nblk=nblk),
        grid=(nblk + 1,),
        in_specs=[pl.BlockSpec((1, tb, d), lambda i: (cur(i) // bps, cur(i) % bps, 0)),
                  pl.BlockSpec((1, tb, d), lambda i: (cur(i) // bps, cur(i) % bps, 0)),
                  pl.BlockSpec((1, 6, d), lambda i: (cur(i) // bps, 0, 0)),
                  pl.BlockSpec((d, d), const2, **single),
                  pl.BlockSpec((d, d_ff), const2, **single),
                  pl.BlockSpec((d_ff, d), const2, **single),
                  pl.BlockSpec((None, 1, d), of_layer, **single),
                  pl.BlockSpec((None, 1, d), of_layer, **single),
                  pl.BlockSpec((None, 1, d), of_layer, **single),
                  pl.BlockSpec((None, 1, d), of_layer, **single)],
        out_specs=pl.BlockSpec((1, tb, d), lambda i: (prev(i) // bps, prev(i) % bps, 0)),
        out_shape=jax.ShapeDtypeStruct((bsz, s, d), x.dtype),
        scratch_shapes=[pltpu.VMEM((tb, d_ff), BF16), pltpu.VMEM((tb, d), F32),
                        pltpu.VMEM((tb, d), F32)],
        compiler_params=pltpu.CompilerParams(
            dimension_semantics=("arbitrary",), vmem_limit_bytes=VMEM_LIMIT),
        name="ffn",
    )(x, mixed, mod, w_out, w1, w2, _layer_row(ln1_w), _layer_row(ln1_b), _layer_row(ln2_w),
      _layer_row(ln2_b))


def kernel(x, c, w_ada, b_ada, w_in, ret_norm_w, gla_gate_w, gla_gate_b, gla_norm_w,
           w_out, ln1_w, ln1_b, w_ff1, w_ff2, ln2_w, ln2_b):
    depth = w_in.shape[0]
    bsz, s, d = x.shape
    alpha = (2.0 * depth) ** 0.25
    tb = min(512, s)
    cos_t, sin_t = (jnp.asarray(t) for t in _rotary_tables(s))
    for l in range(depth):
        mod = _adaln(c, w_ada, b_ada, l)
        mixed, wo_bf, w1_bf, w2_bf = _projmix(x, mod, cos_t, sin_t, w_in, gla_gate_w, gla_gate_b,
                                              ret_norm_w, gla_norm_w, w_out, w_ff1, w_ff2, l, tb)
        x = _ffn(x, mixed, mod, wo_bf, w1_bf, w2_bf, ln1_w, ln1_b, ln2_w, ln2_b, l, alpha, tb)
    return x
```

```python
import functools
import math

import numpy as np
import jax
import jax.numpy as jnp
from jax import lax
from jax.experimental import pallas as pl
from jax.experimental.pallas import tpu as pltpu

CHUNK = 64
RET_HEADS = 4
RET_D = 128
RET_WIDTH = RET_HEADS * RET_D
GLA_HEADS = 4
GLA_DK = 64
GLA_DV = 128
GLA_KEY_WIDTH = GLA_HEADS * GLA_DK
GLA_WIDTH = GLA_HEADS * GLA_DV
D_MIX = RET_WIDTH + GLA_WIDTH
GATE_RANK = 16
GATE_TAU = 16.0
ROPE_BASE = 10000.0
LN_EPS = 1e-5
LANES = 128
SLAB = 4 * RET_WIDTH + 2 * GLA_KEY_WIDTH + 2 * GLA_WIDTH
OFF_RQ, OFF_RK, OFF_RV, OFF_RG = 0, 512, 1024, 1536
OFF_GQ, OFF_GK, OFF_GV, OFF_GG = 2048, 2304, 2560, 3072
MIX_ROWS = 256
VMEM_LIMIT = 56 * 1024 * 1024
LOG2E = math.log2(math.e)

BF16 = jnp.bfloat16
F32 = jnp.float32


def _dot(a, b):
    return jnp.dot(a, b, preferred_element_type=F32)


def _dot_nt(a, b):
    return lax.dot_general(a, b, (((1,), (1,)), ((), ())), preferred_element_type=F32)


def _dot_tn(a, b):
    return lax.dot_general(a, b, (((0,), (0,)), ((), ())), preferred_element_type=F32)


def _zero_of(v):
    bits = lax.shift_right_logical(pltpu.bitcast(v, jnp.uint32), jnp.uint32(32))
    return pltpu.bitcast(bits, F32)


def _ln(x, not_before=None):
    mu = jnp.mean(x, axis=-1, keepdims=True)
    if not_before is not None:
        mu = mu + _zero_of(not_before)[0:1, 0:1]
    xc = x - mu
    var = jnp.mean(xc * xc, axis=-1, keepdims=True)
    return xc * lax.rsqrt(var + LN_EPS)


def _silu(x):
    return x * (1.0 / (1.0 + jnp.exp(-x)))


def _adaln_kernel(c_ref, w_ref, b_ref, o_ref):
    c = c_ref[...]
    val = _dot(_silu(c).astype(BF16), w_ref[...].astype(BF16)) + b_ref[...]
    o_ref[:, pl.ds(pl.program_id(0), 1), :] = val[:, None, :]


def _adaln(c, w_ada, b_ada, layer):
    bsz, d = c.shape
    depth, _, n = w_ada.shape
    return pl.pallas_call(
        _adaln_kernel,
        grid=(n // d,),
        in_specs=[pl.BlockSpec((bsz, d), lambda j: (0, 0)),
                  pl.BlockSpec((None, d, d), lambda j: (layer, 0, j)),
                  pl.BlockSpec((None, 1, d), lambda j: (layer, 0, j))],
        out_specs=pl.BlockSpec((bsz, n // d, d), lambda j: (0, 0, 0)),
        out_shape=jax.ShapeDtypeStruct((bsz, n // d, d), F32),
        compiler_params=pltpu.CompilerParams(vmem_limit_bytes=VMEM_LIMIT),
        name="adaln",
    )(c, w_ada, b_ada.reshape(depth, 1, n))


def _rotary_tables(s):
    half = RET_D // 2
    inv = 1.0 / (ROPE_BASE ** (np.arange(half, dtype=np.float64) / (half - 1)))
    ang = np.arange(s, dtype=np.float64)[:, None] * inv[None, :]
    cos = np.concatenate([np.cos(ang), np.cos(ang)], axis=1)
    sin = np.concatenate([-np.sin(ang), np.sin(ang)], axis=1)
    return cos.astype(np.float32), sin.astype(np.float32)


def _retention_tables(tb):
    log_gamma = np.log(1.0 - 2.0 ** (-5.0 - np.arange(RET_HEADS, dtype=np.float64)))
    idx = np.arange(tb)
    dist = np.abs(idx[:, None] - idx[None, :])
    visible = (idx[None, :] // CHUNK) <= (idx[:, None] // CHUNK)
    scale = RET_D ** -0.5
    dmask = np.exp(log_gamma[:, None, None] * dist[None]) * visible[None] * scale
    qdec = np.exp(log_gamma[:, None] * (idx[None, :] + 1.0)) * scale
    kdec = np.exp(log_gamma[:, None] * (tb - 1.0 - idx[None, :]))
    qdec = np.broadcast_to(qdec[:, :, None], (RET_HEADS, tb, RET_D))
    kdec = np.broadcast_to(kdec[:, :, None], (RET_HEADS, tb, RET_D))
    block_decay = [float(np.exp(lg * tb)) for lg in log_gamma]
    return (dmask.astype(np.float32), np.ascontiguousarray(qdec, np.float32),
            np.ascontiguousarray(kdec, np.float32), block_decay)


def _cumsum_matrix(tb):
    t = np.arange(tb)
    same_chunk = (t[:, None] // CHUNK) == (t[None, :] // CHUNK)
    return (same_chunk & (t[None, :] <= t[:, None])).astype(np.float32)


def _proj_stage(x_ref, mod_ref, cos_ref, sin_ref, w_ref, wtail_ref, gw_ref, gb_ref, slab, la_out):
    x = x_ref[0]
    shift1 = mod_ref[0, 0:1, :]
    scale1 = mod_ref[0, 1:2, :]
    ub = (_ln(x) * (1.0 + scale1) + shift1).astype(BF16)
    yield
    cos = cos_ref[...]
    sin = sin_ref[...]
    group = 256
    for j in range(SLAB // group):
        p = _dot(ub, w_ref[:, j * group:(j + 1) * group])
        if j * group < OFF_RV:
            for h in range(group // RET_D):
                xh = p[:, h * RET_D:(h + 1) * RET_D]
                rot = xh * cos + pltpu.roll(xh, RET_D // 2, 1) * sin
                slab[:, j * group + h * RET_D:j * group + (h + 1) * RET_D] = rot.astype(BF16)
        else:
            slab[:, j * group:(j + 1) * group] = p.astype(BF16)
        yield
    glr = _dot_nt(ub, wtail_ref[...])
    logit = _dot(glr.astype(BF16), gw_ref[...].astype(BF16)) + gb_ref[...]
    log_sig = jnp.minimum(logit, 0.0) - jnp.log1p(jnp.exp(-jnp.abs(logit)))
    la_out[...] = log_sig * (LOG2E / GATE_TAU)
    yield


def _row_of_block(b, block, row):
    n = b.shape[0] // block
    b3 = b.reshape(n, block, b.shape[1])
    return jnp.broadcast_to(b3[:, row:row + 1, :], b3.shape).reshape(b.shape)


def _mix_stage(slab, la_ref, r0, dmask_ref, qdec_ref, kdec_ref, csum_ref, rnw_ref, gnw_ref,
               out_ref, r_state, g_state, block_decay):
    tb = MIX_ROWS
    blk = slice(r0, r0 + tb)
    for h in range(RET_HEADS):
        q = slab[blk, OFF_RQ + h * RET_D:OFF_RQ + (h + 1) * RET_D]
        k = slab[blk, OFF_RK + h * RET_D:OFF_RK + (h + 1) * RET_D]
        v = slab[blk, OFF_RV + h * RET_D:OFF_RV + (h + 1) * RET_D]
        g = slab[blk, OFF_RG + h * RET_D:OFF_RG + (h + 1) * RET_D].astype(F32)
        p = (_dot_nt(q, k) * dmask_ref[h]).astype(BF16)
        r_prev = r_state[h]
        o = _dot(p, v) + _dot(q, r_prev.astype(BF16)) * qdec_ref[h]
        kd = (k.astype(F32) * kdec_ref[h]).astype(BF16)
        r_state[h] = block_decay[h] * r_prev + _dot_tn(kd, v)
        o = o - jnp.mean(o, axis=-1, keepdims=True)
        o = o * lax.rsqrt(jnp.mean(o * o, axis=-1, keepdims=True) + LN_EPS)
        o = o * rnw_ref[:, h * RET_D:(h + 1) * RET_D] * _silu(g)
        out_ref[0, blk, h * RET_D:(h + 1) * RET_D] = o.astype(out_ref.dtype)
        yield

    la = la_ref[blk, :]
    la_hi = la.astype(BF16)
    la_lo = (la - la_hi.astype(F32)).astype(BF16)
    csum = csum_ref[...]
    b = _dot(csum, la_hi) + _dot(csum, la_lo)
    row = lax.broadcasted_iota(jnp.int32, (tb, GLA_KEY_WIDTH), 0)
    psis = [-jnp.abs(b - _row_of_block(b, 2 * c, c - 1)) for c in (32, 16, 8, 4)]
    la_next = pltpu.roll(la, tb - 1, 0)
    la_prev = pltpu.roll(la, 1, 0)
    r4 = row & 3
    psis.append(jnp.where(r4 == 0, la_next, jnp.where(r4 == 1, 0.0, jnp.where(r4 == 2, la, la + la_prev))))
    psis.append(jnp.where((row & 1) == 1, la, 0.0))
    gq = slab[blk, OFF_GQ:OFF_GQ + GLA_KEY_WIDTH] * jnp.asarray(GLA_DK ** -0.5, BF16)
    gk = slab[blk, OFF_GK:OFF_GK + GLA_KEY_WIDTH]
    phis = [jnp.exp2(p).astype(BF16) for p in psis]
    q_lvl = [gq] + [gq * phi for phi in phis]
    phis = [None] + phis
    q_state = gq * jnp.exp2(b).astype(BF16)
    k_state = gk * jnp.exp2(_row_of_block(b, CHUNK, CHUNK - 1) - b).astype(BF16)
    yield

    lane = lax.broadcasted_iota(jnp.int32, (CHUNK, LANES), 1)
    low_half = lane < GLA_DK
    zero_k = jnp.zeros((CHUNK, LANES), BF16)
    zero_s = jnp.zeros((GLA_DK, GLA_DV), BF16)
    t_xor_s = lax.broadcasted_iota(jnp.int32, (CHUNK, GLA_HEADS * CHUNK), 0) ^ (
        lax.broadcasted_iota(jnp.int32, (CHUNK, GLA_HEADS * CHUNK), 1) & (CHUNK - 1))
    b_last = jnp.concatenate([b[ch * CHUNK + CHUNK - 1:(ch + 1) * CHUNK, :] for ch in range(tb // CHUNK)]
                             + [jnp.zeros((8 - tb // CHUNK, GLA_KEY_WIDTH), F32)], axis=0)
    decay_cols = jnp.exp2(b_last.T)

    def lane_tiles(piece, tile, zero, n_tiles):
        return jnp.concatenate([piece if j == tile else zero for j in range(n_tiles)], axis=1)

    for ch in range(tb // CHUNK):
        rows = slice(ch * CHUNK, (ch + 1) * CHUNK)
        orow = slice(r0 + ch * CHUNK, r0 + (ch + 1) * CHUNK)
        k_heads = [jnp.where(low_half if h % 2 == 0 else ~low_half,
                             gk[rows, (h // 2) * LANES:(h // 2 + 1) * LANES], zero_k)
                   for h in range(GLA_HEADS)]
        a_lvl = []
        for lvl in range(len(q_lvl)):
            pieces = []
            for h in range(GLA_HEADS):
                tile = h // 2
                kh = k_heads[h]
                if phis[lvl] is not None:
                    kh = kh * phis[lvl][rows, tile * LANES:(tile + 1) * LANES]
                pieces.append(lane_tiles(kh, tile, zero_k, 2))
            k_stack = jnp.concatenate(pieces, axis=0)
            a_lvl.append(_dot_nt(q_lvl[lvl][rows], k_stack))
        a = a_lvl[0]
        for i, c in enumerate((1, 2, 4, 8, 16, 32)):
            a = jnp.where(t_xor_s >= c, a_lvl[len(a_lvl) - 1 - i], a)
        a = a.astype(BF16)
        gv = slab[orow, OFF_GV:OFF_GV + GLA_WIDTH]
        v_bd = jnp.concatenate(
            [lane_tiles(gv[:, h * GLA_DV:(h + 1) * GLA_DV], h, jnp.zeros((CHUNK, GLA_DV), BF16), GLA_HEADS)
             for h in range(GLA_HEADS)], axis=0)
        s_bd = jnp.concatenate(
            [lane_tiles(g_state[h].astype(BF16), h, zero_s, GLA_HEADS) for h in range(GLA_HEADS)],
            axis=0)
        o_all = _dot(a, v_bd) + _dot(q_state[rows], s_bd)
        for tile in range(GLA_HEADS // 2):
            upd = _dot_tn(k_state[rows, tile * LANES:(tile + 1) * LANES],
                          gv[:, 2 * tile * GLA_DV:(2 * tile + 2) * GLA_DV])
            for i in range(2):
                h = 2 * tile + i
                g_state[h] = (g_state[h] * decay_cols[h * GLA_DK:(h + 1) * GLA_DK, ch:ch + 1]
                              + upd[i * GLA_DK:(i + 1) * GLA_DK, i * GLA_DV:(i + 1) * GLA_DV])
        for h in range(GLA_HEADS):
            o = o_all[:, h * GLA_DV:(h + 1) * GLA_DV]
            o = o * lax.rsqrt(jnp.mean(o * o, axis=-1, keepdims=True) + LN_EPS)
            gg = slab[orow, OFF_GG + h * GLA_DV:OFF_GG + (h + 1) * GLA_DV].astype(F32)
            o = o * gnw_ref[:, h * GLA_DV:(h + 1) * GLA_DV] * _silu(gg)
            out_ref[0, orow, RET_WIDTH + h * GLA_DV:RET_WIDTH + (h + 1) * GLA_DV] = o.astype(out_ref.dtype)
        yield


def _interleave(gen_a, n_a, gen_b, n_b):
    done_a = done_b = 0
    while done_a < n_a or done_b < n_b:
        if done_b >= n_b or (done_a < n_a and (done_a + 0.5) * n_b <= (done_b + 0.5) * n_a):
            next(gen_a)
            done_a += 1
        else:
            next(gen_b)
            done_b += 1
    for g in (gen_a, gen_b):
        assert next(g, "done") == "done", "piece count mismatch"


def _projmix_kernel(x_ref, mod_ref, cos_ref, sin_ref, w_ref, gw_ref, gb_ref,
                    dmask_ref, qdec_ref, kdec_ref, csum_ref, rnw_ref, gnw_ref,
                    wo_ref, w1_ref, w2_ref,
                    out_ref, wo_bf_ref, w1_bf_ref, w2_bf_ref,
                    w_bf, w_tail, slab0, slab1, la0, la1, r_state, g_state,
                    *, tb, blocks_per_seq, block_decay):
    j = pl.program_id(0)

    @pl.when(j == 0)
    def _():
        slab1[...] = jnp.zeros_like(slab1)
        la1[...] = jnp.zeros_like(la1)
        for g in range(0, SLAB, 256):
            w_bf[:, g:g + 256] = w_ref[g:g + 256, :].T.astype(BF16)
        w_tail[...] = w_ref[SLAB:SLAB + GATE_RANK, :].astype(BF16)

    wo_bf_ref[...] = wo_ref[...].astype(BF16)
    w1_bf_ref[...] = w1_ref[...].astype(BF16)
    w2_bf_ref[...] = w2_ref[...].astype(BF16)

    @pl.when((j == 0) | ((j - 1) % blocks_per_seq == 0))
    def _():
        r_state[...] = jnp.zeros_like(r_state)
        g_state[...] = jnp.zeros_like(g_state)

    def mix_all(slab_r, la_r):
        for r0 in range(0, tb, MIX_ROWS):
            yield from _mix_stage(slab_r, la_r, r0, dmask_ref, qdec_ref, kdec_ref, csum_ref,
                                  rnw_ref, gnw_ref, out_ref, r_state, g_state, block_decay)

    def step(slab_w, la_w, slab_r, la_r):
        proj = _proj_stage(x_ref, mod_ref, cos_ref, sin_ref, w_bf, w_tail, gw_ref, gb_ref, slab_w, la_w)
        n_proj = 2 + SLAB // 256
        n_mix = (tb // MIX_ROWS) * (RET_HEADS + 1 + MIX_ROWS // CHUNK)
        _interleave(proj, n_proj, mix_all(slab_r, la_r), n_mix)

    @pl.when(j % 2 == 0)
    def _():
        step(slab0, la0, slab1, la1)

    @pl.when(j % 2 == 1)
    def _():
        step(slab1, la1, slab0, la0)


def _layer_row(v):
    return v.reshape(v.shape[0], 1, v.shape[1])


def _projmix(x, mod, cos_t, sin_t, w_in, gate_w, gate_b, ret_norm_w, gla_norm_w,
             w_out, w_ff1, w_ff2, layer, tb):
    bsz, s, d = x.shape
    d_ff = w_ff1.shape[2]
    bps = s // tb
    nblk = bsz * bps
    dmask, qdec, kdec, block_decay = _retention_tables(MIX_ROWS)
    csum = _cumsum_matrix(MIX_ROWS)
    const2 = lambda j: (0, 0)
    const3 = lambda j: (0, 0, 0)
    of_layer = lambda j: (layer, 0, 0)
    proj_blk = lambda j: jnp.minimum(j, nblk - 1)
    mix_blk = lambda j: jnp.maximum(j - 1, 0)
    single = dict(pipeline_mode=pl.Buffered(1))
    wo_rows, w1_rows, w2_rows = d // nblk, d // nblk, d_ff // nblk
    return pl.pallas_call(
        functools.partial(_projmix_kernel, tb=tb, blocks_per_seq=bps, block_decay=block_decay),
        grid=(nblk + 1,),
        in_specs=[pl.BlockSpec((1, tb, d), lambda j: (proj_blk(j) // bps, proj_blk(j) % bps, 0)),
                  pl.BlockSpec((1, 6, d), lambda j: (proj_blk(j) // bps, 0, 0)),
                  pl.BlockSpec((tb, RET_D), lambda j: (proj_blk(j) % bps, 0)),
                  pl.BlockSpec((tb, RET_D), lambda j: (proj_blk(j) % bps, 0)),
                  pl.BlockSpec((None, SLAB + GATE_RANK, d), of_layer, **single),
                  pl.BlockSpec((None, GATE_RANK, GLA_KEY_WIDTH), of_layer, **single),
                  pl.BlockSpec((None, 1, GLA_KEY_WIDTH), of_layer, **single),
                  pl.BlockSpec(dmask.shape, const3, **single),
                  pl.BlockSpec(qdec.shape, const3, **single),
                  pl.BlockSpec(kdec.shape, const3, **single),
                  pl.BlockSpec(csum.shape, const2, **single),
                  pl.BlockSpec((None, 1, RET_WIDTH), of_layer, **single),
                  pl.BlockSpec((None, 1, GLA_WIDTH), of_layer, **single),
                  pl.BlockSpec((None, wo_rows, d), lambda j: (layer, proj_blk(j), 0)),
                  pl.BlockSpec((None, w1_rows, d_ff), lambda j: (layer, proj_blk(j), 0)),
                  pl.BlockSpec((None, w2_rows, d), lambda j: (layer, proj_blk(j), 0))],
        out_specs=[pl.BlockSpec((1, tb, D_MIX), lambda j: (mix_blk(j) // bps, mix_blk(j) % bps, 0)),
                   pl.BlockSpec((wo_rows, d), lambda j: (proj_blk(j), 0)),
                   pl.BlockSpec((w1_rows, d_ff), lambda j: (proj_blk(j), 0)),
                   pl.BlockSpec((w2_rows, d), lambda j: (proj_blk(j), 0))],
        out_shape=[jax.ShapeDtypeStruct((bsz, s, D_MIX), BF16),
                   jax.ShapeDtypeStruct((d, d), BF16),
                   jax.ShapeDtypeStruct((d, d_ff), BF16),
                   jax.ShapeDtypeStruct((d_ff, d), BF16)],
        scratch_shapes=[pltpu.VMEM((d, SLAB), BF16), pltpu.VMEM((GATE_RANK, d), BF16),
                        pltpu.VMEM((tb, SLAB), BF16), pltpu.VMEM((tb, SLAB), BF16),
                        pltpu.VMEM((tb, GLA_KEY_WIDTH), F32), pltpu.VMEM((tb, GLA_KEY_WIDTH), F32),
                        pltpu.VMEM((RET_HEADS, RET_D, RET_D), F32),
                        pltpu.VMEM((GLA_HEADS, GLA_DK, GLA_DV), F32)],
        compiler_params=pltpu.CompilerParams(
            dimension_semantics=("arbitrary",), vmem_limit_bytes=VMEM_LIMIT),
        name="projmix",
    )(x, mod, cos_t, sin_t, jnp.swapaxes(w_in, 1, 2), gate_w, _layer_row(gate_b),
      jnp.asarray(dmask), jnp.asarray(qdec), jnp.asarray(kdec), jnp.asarray(csum, BF16),
      _layer_row(ret_norm_w), _layer_row(gla_norm_w), w_out, w_ff1, w_ff2)


def _ffn_kernel(x_ref, mix_ref, mod_ref, wo_ref, w1_ref, w2_ref, ln1w_ref, ln1b_ref,
                ln2w_ref, ln2b_ref, out_ref, h_ref, *, alpha):
    gate1 = mod_ref[0, 2:3, :]
    shift2 = mod_ref[0, 3:4, :]
    scale2 = mod_ref[0, 4:5, :]
    gate2 = mod_ref[0, 5:6, :]
    d_ff = w1_ref.shape[1]
    n_groups = 4
    group = d_ff // n_groups
    tb = x_ref.shape[1]
    halves = (slice(0, tb // 2), slice(tb // 2, tb))

    def chain(rows, wait):
        m = _dot(mix_ref[0, rows, :], wo_ref[...])
        yield
        x1 = (_ln(alpha * x_ref[0, rows, :] + gate1 * m, wait.get("ln1")) * ln1w_ref[...]
              + ln1b_ref[...])
        u2f = _ln(x1) * (1.0 + scale2) + shift2
        wait["ln1_done"] = u2f[u2f.shape[0] - 8:, u2f.shape[1] - LANES:]
        u2 = u2f.astype(BF16)
        yield
        for j in range(n_groups):
            hj = jnp.maximum(_dot(u2, w1_ref[:, j * group:(j + 1) * group]), 0.0)
            h_ref[rows, j * group:(j + 1) * group] = (hj * hj).astype(BF16)
            yield
        f = _dot(h_ref[rows, :], w2_ref[...])
        yield
        out_ref[0, rows, :] = _ln(alpha * x1 + gate2 * f) * ln2w_ref[...] + ln2b_ref[...]
        yield

    wait_a, wait_b = {}, {}
    half_a, half_b = chain(halves[0], wait_a), chain(halves[1], wait_b)
    next(half_a)
    next(half_b)
    next(half_a)
    wait_b["ln1"] = wait_a["ln1_done"]
    n_rest = n_groups + 2
    _interleave(half_a, n_rest, half_b, n_rest + 1)


def _ffn(x, mixed, mod, w_out, w1, w2, ln1_w, ln1_b, ln2_w, ln2_b, layer, alpha, tb):
    bsz, s, d = x.shape
    nb = s // tb
    d_ff = w1.shape[1]
    const2 = lambda b, i: (0, 0)
    of_layer = lambda b, i: (layer, 0, 0)
    single = dict(pipeline_mode=pl.Buffered(1))
    return pl.pallas_call(
        functools.partial(_ffn_kernel, alpha=alpha),
        grid=(bsz, nb),
        in_specs=[pl.BlockSpec((1, tb, d), lambda b, i: (b, i, 0)),
                  pl.BlockSpec((1, tb, d), lambda b, i: (b, i, 0)),
                  pl.BlockSpec((1, 6, d), lambda b, i: (b, 0, 0)),
                  pl.BlockSpec((d, d), const2, **single),
                  pl.BlockSpec((d, d_ff), const2, **single),
                  pl.BlockSpec((d_ff, d), const2, **single),
                  pl.BlockSpec((None, 1, d), of_layer, **single),
                  pl.BlockSpec((None, 1, d), of_layer, **single),
                  pl.BlockSpec((None, 1, d), of_layer, **single),
                  pl.BlockSpec((None, 1, d), of_layer, **single)],
        out_specs=pl.BlockSpec((1, tb, d), lambda b, i: (b, i, 0)),
        out_shape=jax.ShapeDtypeStruct((bsz, s, d), x.dtype),
        scratch_shapes=[pltpu.VMEM((tb, d_ff), BF16)],
        compiler_params=pltpu.CompilerParams(
            dimension_semantics=("parallel", "parallel"), vmem_limit_bytes=VMEM_LIMIT),
        name="ffn",
    )(x, mixed, mod, w_out, w1, w2, _layer_row(ln1_w), _layer_row(ln1_b), _layer_row(ln2_w),
      _layer_row(ln2_b))


def kernel(x, c, w_ada, b_ada, w_in, ret_norm_w, gla_gate_w, gla_gate_b, gla_norm_w,
           w_out, ln1_w, ln1_b, w_ff1, w_ff2, ln2_w, ln2_b):
    depth = w_in.shape[0]
    bsz, s, d = x.shape
    alpha = (2.0 * depth) ** 0.25
    tb = min(512, s)
    cos_t, sin_t = (jnp.asarray(t) for t in _rotary_tables(s))
    for l in range(depth):
        mod = _adaln(c, w_ada, b_ada, l)
        mixed, wo_bf, w1_bf, w2_bf = _projmix(x, mod, cos_t, sin_t, w_in, gla_gate_w, gla_gate_b,
                                              ret_norm_w, gla_norm_w, w_out, w_ff1, w_ff2, l, tb)
        x = _ffn(x, mixed, mod, wo_bf, w1_bf, w2_bf, ln1_w, ln1_b, ln2_w, ln2_b, l, alpha,
                 min(1024, s))
    return x
```

```python
import functools
import math

import numpy as np
import jax
import jax.numpy as jnp
from jax import lax
from jax.experimental import pallas as pl
from jax.experimental.pallas import tpu as pltpu

CHUNK = 64
RET_HEADS = 4
RET_D = 128
RET_WIDTH = RET_HEADS * RET_D
GLA_HEADS = 4
GLA_DK = 64
GLA_DV = 128
GLA_KEY_WIDTH = GLA_HEADS * GLA_DK
GLA_WIDTH = GLA_HEADS * GLA_DV
D_MIX = RET_WIDTH + GLA_WIDTH
GATE_RANK = 16
GATE_TAU = 16.0
ROPE_BASE = 10000.0
LN_EPS = 1e-5
LANES = 128
SLAB = 4 * RET_WIDTH + 2 * GLA_KEY_WIDTH + 2 * GLA_WIDTH
OFF_RQ, OFF_RK, OFF_RV, OFF_RG = 0, 512, 1024, 1536
OFF_GQ, OFF_GK, OFF_GV, OFF_GG = 2048, 2304, 2560, 3072
MIX_ROWS = 256
VMEM_LIMIT = 56 * 1024 * 1024
LOG2E = math.log2(math.e)

BF16 = jnp.bfloat16
F32 = jnp.float32


def _dot(a, b):
    return jnp.dot(a, b, preferred_element_type=F32)


def _dot_nt(a, b):
    return lax.dot_general(a, b, (((1,), (1,)), ((), ())), preferred_element_type=F32)


def _dot_tn(a, b):
    return lax.dot_general(a, b, (((0,), (0,)), ((), ())), preferred_element_type=F32)


def _zero_of(v):
    bits = lax.shift_right_logical(pltpu.bitcast(v, jnp.uint32), jnp.uint32(32))
    return pltpu.bitcast(bits, F32)


def _ln(x, not_before=None):
    mu = jnp.mean(x, axis=-1, keepdims=True)
    if not_before is not None:
        mu = mu + _zero_of(not_before)[0:1, 0:1]
    xc = x - mu
    var = jnp.mean(xc * xc, axis=-1, keepdims=True)
    return xc * lax.rsqrt(var + LN_EPS)


def _silu(x):
    return x * (1.0 / (1.0 + jnp.exp(-x)))


def _adaln_kernel(c_ref, w_ref, b_ref, o_ref):
    c = c_ref[...]
    val = _dot(_silu(c).astype(BF16), w_ref[...].astype(BF16)) + b_ref[...]
    o_ref[:, pl.ds(pl.program_id(0), 1), :] = val[:, None, :]


def _adaln(c, w_ada, b_ada, layer):
    bsz, d = c.shape
    depth, _, n = w_ada.shape
    return pl.pallas_call(
        _adaln_kernel,
        grid=(n // d,),
        in_specs=[pl.BlockSpec((bsz, d), lambda j: (0, 0)),
                  pl.BlockSpec((None, d, d), lambda j: (layer, 0, j)),
                  pl.BlockSpec((None, 1, d), lambda j: (layer, 0, j))],
        out_specs=pl.BlockSpec((bsz, n // d, d), lambda j: (0, 0, 0)),
        out_shape=jax.ShapeDtypeStruct((bsz, n // d, d), F32),
        compiler_params=pltpu.CompilerParams(vmem_limit_bytes=VMEM_LIMIT),
        name="adaln",
    )(c, w_ada, b_ada.reshape(depth, 1, n))


def _rotary_tables(s):
    half = RET_D // 2
    inv = 1.0 / (ROPE_BASE ** (np.arange(half, dtype=np.float64) / (half - 1)))
    ang = np.arange(s, dtype=np.float64)[:, None] * inv[None, :]
    cos = np.concatenate([np.cos(ang), np.cos(ang)], axis=1)
    sin = np.concatenate([-np.sin(ang), np.sin(ang)], axis=1)
    return cos.astype(np.float32), sin.astype(np.float32)


def _retention_tables(tb):
    log_gamma = np.log(1.0 - 2.0 ** (-5.0 - np.arange(RET_HEADS, dtype=np.float64)))
    idx = np.arange(tb)
    dist = np.abs(idx[:, None] - idx[None, :])
    visible = (idx[None, :] // CHUNK) <= (idx[:, None] // CHUNK)
    scale = RET_D ** -0.5
    dmask = np.exp(log_gamma[:, None, None] * dist[None]) * visible[None] * scale
    qdec = np.exp(log_gamma[:, None] * (idx[None, :] + 1.0)) * scale
    kdec = np.exp(log_gamma[:, None] * (tb - 1.0 - idx[None, :]))
    qdec = np.broadcast_to(qdec[:, :, None], (RET_HEADS, tb, RET_D))
    kdec = np.broadcast_to(kdec[:, :, None], (RET_HEADS, tb, RET_D))
    block_decay = [float(np.exp(lg * tb)) for lg in log_gamma]
    return (dmask.astype(np.float32), np.ascontiguousarray(qdec, np.float32),
            np.ascontiguousarray(kdec, np.float32), block_decay)


def _cumsum_matrix(tb):
    t = np.arange(tb)
    same_chunk = (t[:, None] // CHUNK) == (t[None, :] // CHUNK)
    return (same_chunk & (t[None, :] <= t[:, None])).astype(np.float32)


def _proj_stage(x_ref, mod_ref, cos_ref, sin_ref, w_ref, wtail_ref, gw_ref, gb_ref, slab, la_out):
    x = x_ref[0]
    shift1 = mod_ref[0, 0:1, :]
    scale1 = mod_ref[0, 1:2, :]
    ub = (_ln(x) * (1.0 + scale1) + shift1).astype(BF16)
    yield
    cos = cos_ref[...]
    sin = sin_ref[...]
    group = 256
    for j in range(SLAB // group):
        p = _dot(ub, w_ref[:, j * group:(j + 1) * group])
        if j * group < OFF_RV:
            for h in range(group // RET_D):
                xh = p[:, h * RET_D:(h + 1) * RET_D]
                rot = xh * cos + pltpu.roll(xh, RET_D // 2, 1) * sin
                slab[:, j * group + h * RET_D:j * group + (h + 1) * RET_D] = rot.astype(BF16)
        else:
            slab[:, j * group:(j + 1) * group] = p.astype(BF16)
        yield
    glr = _dot_nt(ub, wtail_ref[...])
    logit = _dot(glr.astype(BF16), gw_ref[...].astype(BF16)) + gb_ref[...]
    log_sig = jnp.minimum(logit, 0.0) - jnp.log1p(jnp.exp(-jnp.abs(logit)))
    la_out[...] = log_sig * (LOG2E / GATE_TAU)
    yield


def _row_of_block(b, block, row):
    n = b.shape[0] // block
    b3 = b.reshape(n, block, b.shape[1])
    return jnp.broadcast_to(b3[:, row:row + 1, :], b3.shape).reshape(b.shape)


def _mix_stage(slab, la_ref, r0, dmask_ref, qdec_ref, kdec_ref, csum_ref, rnw_ref, gnw_ref,
               out_ref, r_state, g_state, block_decay):
    tb = MIX_ROWS
    blk = slice(r0, r0 + tb)
    for h in range(RET_HEADS):
        q = slab[blk, OFF_RQ + h * RET_D:OFF_RQ + (h + 1) * RET_D]
        k = slab[blk, OFF_RK + h * RET_D:OFF_RK + (h + 1) * RET_D]
        v = slab[blk, OFF_RV + h * RET_D:OFF_RV + (h + 1) * RET_D]
        g = slab[blk, OFF_RG + h * RET_D:OFF_RG + (h + 1) * RET_D].astype(F32)
        p = (_dot_nt(q, k) * dmask_ref[h]).astype(BF16)
        r_prev = r_state[h]
        o = _dot(p, v) + _dot(q, r_prev.astype(BF16)) * qdec_ref[h]
        kd = (k.astype(F32) * kdec_ref[h]).astype(BF16)
        r_state[h] = block_decay[h] * r_prev + _dot_tn(kd, v)
        o = o - jnp.mean(o, axis=-1, keepdims=True)
        o = o * lax.rsqrt(jnp.mean(o * o, axis=-1, keepdims=True) + LN_EPS)
        o = o * rnw_ref[:, h * RET_D:(h + 1) * RET_D] * _silu(g)
        out_ref[0, blk, h * RET_D:(h + 1) * RET_D] = o.astype(out_ref.dtype)
        yield

    la = la_ref[blk, :]
    la_hi = la.astype(BF16)
    la_lo = (la - la_hi.astype(F32)).astype(BF16)
    csum = csum_ref[...]
    b = _dot(csum, la_hi) + _dot(csum, la_lo)
    row = lax.broadcasted_iota(jnp.int32, (tb, GLA_KEY_WIDTH), 0)
    psis = [-jnp.abs(b - _row_of_block(b, 2 * c, c - 1)) for c in (32, 16, 8, 4)]
    la_next = pltpu.roll(la, tb - 1, 0)
    la_prev = pltpu.roll(la, 1, 0)
    r4 = row & 3
    psis.append(jnp.where(r4 == 0, la_next, jnp.where(r4 == 1, 0.0, jnp.where(r4 == 2, la, la + la_prev))))
    psis.append(jnp.where((row & 1) == 1, la, 0.0))
    gq = slab[blk, OFF_GQ:OFF_GQ + GLA_KEY_WIDTH] * jnp.asarray(GLA_DK ** -0.5, BF16)
    gk = slab[blk, OFF_GK:OFF_GK + GLA_KEY_WIDTH]
    phis = [jnp.exp2(p).astype(BF16) for p in psis]
    q_lvl = [gq] + [gq * phi for phi in phis]
    phis = [None] + phis
    q_state = gq * jnp.exp2(b).astype(BF16)
    k_state = gk * jnp.exp2(_row_of_block(b, CHUNK, CHUNK - 1) - b).astype(BF16)
    yield

    lane = lax.broadcasted_iota(jnp.int32, (CHUNK, LANES), 1)
    low_half = lane < GLA_DK
    zero_k = jnp.zeros((CHUNK, LANES), BF16)
    zero_s = jnp.zeros((GLA_DV, LANES), BF16)
    t_xor_s = lax.broadcasted_iota(jnp.int32, (CHUNK, GLA_HEADS * CHUNK), 0) ^ (
        lax.broadcasted_iota(jnp.int32, (CHUNK, GLA_HEADS * CHUNK), 1) & (CHUNK - 1))
    lane_s = lax.broadcasted_iota(jnp.int32, (GLA_DV, LANES), 1)
    low_half_s = lane_s < GLA_DK

    def lane_tiles(piece, tile, zero, n_tiles):
        return jnp.concatenate([piece if j == tile else zero for j in range(n_tiles)], axis=1)

    for ch in range(tb // CHUNK):
        rows = slice(ch * CHUNK, (ch + 1) * CHUNK)
        orow = slice(r0 + ch * CHUNK, r0 + (ch + 1) * CHUNK)
        k_heads = [jnp.where(low_half if h % 2 == 0 else ~low_half,
                             gk[rows, (h // 2) * LANES:(h // 2 + 1) * LANES], zero_k)
                   for h in range(GLA_HEADS)]
        a_lvl = []
        for lvl in range(len(q_lvl)):
            pieces = []
            for h in range(GLA_HEADS):
                tile = h // 2
                kh = k_heads[h]
                if phis[lvl] is not None:
                    kh = kh * phis[lvl][rows, tile * LANES:(tile + 1) * LANES]
                pieces.append(lane_tiles(kh, tile, zero_k, 2))
            k_stack = jnp.concatenate(pieces, axis=0)
            a_lvl.append(_dot_nt(q_lvl[lvl][rows], k_stack))
        a = a_lvl[0]
        for i, c in enumerate((1, 2, 4, 8, 16, 32)):
            a = jnp.where(t_xor_s >= c, a_lvl[len(a_lvl) - 1 - i], a)
        a = a.astype(BF16)
        gv = slab[orow, OFF_GV:OFF_GV + GLA_WIDTH]
        v_bd = jnp.concatenate(
            [lane_tiles(gv[:, h * GLA_DV:(h + 1) * GLA_DV], h, jnp.zeros((CHUNK, GLA_DV), BF16), GLA_HEADS)
             for h in range(GLA_HEADS)], axis=0)
        s_bd = jnp.concatenate(
            [lane_tiles(g_state[h].astype(BF16), h // 2, zero_s, 2) for h in range(GLA_HEADS)],
            axis=0)
        o_all = _dot(a, v_bd) + _dot_nt(q_state[rows], s_bd)
        decay = jnp.exp2(b[ch * CHUNK + CHUNK - 1:(ch + 1) * CHUNK, :])
        for h in range(GLA_HEADS):
            tile = h // 2
            upd = _dot_tn(gv[:, h * GLA_DV:(h + 1) * GLA_DV], k_state[rows, tile * LANES:(tile + 1) * LANES])
            upd = jnp.where(low_half_s if h % 2 == 0 else ~low_half_s, upd, 0.0)
            g_state[h] = g_state[h] * decay[:, tile * LANES:(tile + 1) * LANES] + upd
        for h in range(GLA_HEADS):
            o = o_all[:, h * GLA_DV:(h + 1) * GLA_DV]
            o = o * lax.rsqrt(jnp.mean(o * o, axis=-1, keepdims=True) + LN_EPS)
            gg = slab[orow, OFF_GG + h * GLA_DV:OFF_GG + (h + 1) * GLA_DV].astype(F32)
            o = o * gnw_ref[:, h * GLA_DV:(h + 1) * GLA_DV] * _silu(gg)
            out_ref[0, orow, RET_WIDTH + h * GLA_DV:RET_WIDTH + (h + 1) * GLA_DV] = o.astype(out_ref.dtype)
        yield


def _interleave(gen_a, n_a, gen_b, n_b):
    done_a = done_b = 0
    while done_a < n_a or done_b < n_b:
        if done_b >= n_b or (done_a < n_a and (done_a + 0.5) * n_b <= (done_b + 0.5) * n_a):
            next(gen_a)
            done_a += 1
        else:
            next(gen_b)
            done_b += 1
    for g in (gen_a, gen_b):
        assert next(g, "done") == "done", "piece count mismatch"


def _projmix_kernel(x_ref, mod_ref, cos_ref, sin_ref, w_ref, gw_ref, gb_ref,
                    dmask_ref, qdec_ref, kdec_ref, csum_ref, rnw_ref, gnw_ref,
                    wo_ref, w1_ref, w2_ref,
                    out_ref, wo_bf_ref, w1_bf_ref, w2_bf_ref,
                    w_bf, w_tail, slab0, slab1, la0, la1, r_state, g_state,
                    *, tb, blocks_per_seq, n_blocks, block_decay):
    j = pl.program_id(0)

    slabs, las = (slab0, slab1), (la0, la1)
    n_proj = 2 + SLAB // 256
    n_mix = (tb // MIX_ROWS) * (RET_HEADS + 1 + MIX_ROWS // CHUNK)

    @pl.when(j == 0)
    def _():
        for g in range(0, SLAB, 256):
            w_bf[:, g:g + 256] = w_ref[g:g + 256, :].T.astype(BF16)
        w_tail[...] = w_ref[SLAB:SLAB + GATE_RANK, :].astype(BF16)

    wo_bf_ref[...] = wo_ref[...].astype(BF16)
    w1_bf_ref[...] = w1_ref[...].astype(BF16)
    w2_bf_ref[...] = w2_ref[...].astype(BF16)

    @pl.when((j - 1) % blocks_per_seq == 0)
    def _():
        r_state[...] = jnp.zeros_like(r_state)
        g_state[...] = jnp.zeros_like(g_state)

    def project(buf):
        return _proj_stage(x_ref, mod_ref, cos_ref, sin_ref, w_bf, w_tail, gw_ref, gb_ref,
                           slabs[buf], las[buf])

    def mix(buf):
        for r0 in range(0, tb, MIX_ROWS):
            yield from _mix_stage(slabs[buf], las[buf], r0, dmask_ref, qdec_ref, kdec_ref, csum_ref,
                                  rnw_ref, gnw_ref, out_ref, r_state, g_state, block_decay)

    @pl.when(j == 0)
    def _():
        for _ in project(0):
            pass

    for parity in (0, 1):
        @pl.when((j > 0) & (j < n_blocks) & (j % 2 == parity))
        def _():
            _interleave(project(parity), n_proj, mix(1 - parity), n_mix)

    @pl.when(j == n_blocks)
    def _():
        for _ in mix((n_blocks - 1) % 2):
            pass


def _layer_row(v):
    return v.reshape(v.shape[0], 1, v.shape[1])


def _projmix(x, mod, cos_t, sin_t, w_in, gate_w, gate_b, ret_norm_w, gla_norm_w,
             w_out, w_ff1, w_ff2, layer, tb):
    bsz, s, d = x.shape
    d_ff = w_ff1.shape[2]
    bps = s // tb
    nblk = bsz * bps
    dmask, qdec, kdec, block_decay = _retention_tables(MIX_ROWS)
    csum = _cumsum_matrix(MIX_ROWS)
    const2 = lambda j: (0, 0)
    const3 = lambda j: (0, 0, 0)
    of_layer = lambda j: (layer, 0, 0)
    proj_blk = lambda j: jnp.minimum(j, nblk - 1)
    mix_blk = lambda j: jnp.maximum(j - 1, 0)
    single = dict(pipeline_mode=pl.Buffered(1))
    wo_rows, w1_rows, w2_rows = d // nblk, d // nblk, d_ff // nblk
    return pl.pallas_call(
        functools.partial(_projmix_kernel, tb=tb, blocks_per_seq=bps, n_blocks=nblk,
                          block_decay=block_decay),
        grid=(nblk + 1,),
        in_specs=[pl.BlockSpec((1, tb, d), lambda j: (proj_blk(j) // bps, proj_blk(j) % bps, 0)),
                  pl.BlockSpec((1, 6, d), lambda j: (proj_blk(j) // bps, 0, 0)),
                  pl.BlockSpec((tb, RET_D), lambda j: (proj_blk(j) % bps, 0)),
                  pl.BlockSpec((tb, RET_D), lambda j: (proj_blk(j) % bps, 0)),
                  pl.BlockSpec((None, SLAB + GATE_RANK, d), of_layer, **single),
                  pl.BlockSpec((None, GATE_RANK, GLA_KEY_WIDTH), of_layer, **single),
                  pl.BlockSpec((None, 1, GLA_KEY_WIDTH), of_layer, **single),
                  pl.BlockSpec(dmask.shape, const3, **single),
                  pl.BlockSpec(qdec.shape, const3, **single),
                  pl.BlockSpec(kdec.shape, const3, **single),
                  pl.BlockSpec(csum.shape, const2, **single),
                  pl.BlockSpec((None, 1, RET_WIDTH), of_layer, **single),
                  pl.BlockSpec((None, 1, GLA_WIDTH), of_layer, **single),
                  pl.BlockSpec((None, wo_rows, d), lambda j: (layer, proj_blk(j), 0)),
                  pl.BlockSpec((None, w1_rows, d_ff), lambda j: (layer, proj_blk(j), 0)),
                  pl.BlockSpec((None, w2_rows, d), lambda j: (layer, proj_blk(j), 0))],
        out_specs=[pl.BlockSpec((1, tb, D_MIX), lambda j: (mix_blk(j) // bps, mix_blk(j) % bps, 0)),
                   pl.BlockSpec((wo_rows, d), lambda j: (proj_blk(j), 0)),
                   pl.BlockSpec((w1_rows, d_ff), lambda j: (proj_blk(j), 0)),
                   pl.BlockSpec((w2_rows, d), lambda j: (proj_blk(j), 0))],
        out_shape=[jax.ShapeDtypeStruct((bsz, s, D_MIX), BF16),
                   jax.ShapeDtypeStruct((d, d), BF16),
                   jax.ShapeDtypeStruct((d, d_ff), BF16),
                   jax.ShapeDtypeStruct((d_ff, d), BF16)],
        scratch_shapes=[pltpu.VMEM((d, SLAB), BF16), pltpu.VMEM((GATE_RANK, d), BF16),
                        pltpu.VMEM((tb, SLAB), BF16), pltpu.VMEM((tb, SLAB), BF16),
                        pltpu.VMEM((tb, GLA_KEY_WIDTH), F32), pltpu.VMEM((tb, GLA_KEY_WIDTH), F32),
                        pltpu.VMEM((RET_HEADS, RET_D, RET_D), F32),
                        pltpu.VMEM((GLA_HEADS, GLA_DV, LANES), F32)],
        compiler_params=pltpu.CompilerParams(
            dimension_semantics=("arbitrary",), vmem_limit_bytes=VMEM_LIMIT),
        name="projmix",
    )(x, mod, cos_t, sin_t, jnp.swapaxes(w_in, 1, 2), gate_w, _layer_row(gate_b),
      jnp.asarray(dmask), jnp.asarray(qdec), jnp.asarray(kdec), jnp.asarray(csum, BF16),
      _layer_row(ret_norm_w), _layer_row(gla_norm_w), w_out, w_ff1, w_ff2)


def _ffn_kernel(x_ref, mix_ref, mod_ref, wo_ref, w1_ref, w2_ref, ln1w_ref, ln1b_ref,
                ln2w_ref, ln2b_ref, out_ref, h_ref, *, alpha):
    gate1 = mod_ref[0, 2:3, :]
    shift2 = mod_ref[0, 3:4, :]
    scale2 = mod_ref[0, 4:5, :]
    gate2 = mod_ref[0, 5:6, :]
    d_ff = w1_ref.shape[1]
    n_groups = 4
    group = d_ff // n_groups
    tb = x_ref.shape[1]
    halves = (slice(0, tb // 2), slice(tb // 2, tb))

    def chain(rows, wait):
        m = _dot(mix_ref[0, rows, :], wo_ref[...])
        yield
        x1 = (_ln(alpha * x_ref[0, rows, :] + gate1 * m, wait.get("ln1")) * ln1w_ref[...]
              + ln1b_ref[...])
        u2f = _ln(x1) * (1.0 + scale2) + shift2
        wait["ln1_done"] = u2f[u2f.shape[0] - 8:, u2f.shape[1] - LANES:]
        u2 = u2f.astype(BF16)
        yield
        for j in range(n_groups):
            hj = jnp.maximum(_dot(u2, w1_ref[:, j * group:(j + 1) * group]), 0.0)
            h_ref[rows, j * group:(j + 1) * group] = (hj * hj).astype(BF16)
            yield
        f = _dot(h_ref[rows, :], w2_ref[...])
        yield
        out_ref[0, rows, :] = _ln(alpha * x1 + gate2 * f) * ln2w_ref[...] + ln2b_ref[...]
        yield

    wait_a, wait_b = {}, {}
    half_a, half_b = chain(halves[0], wait_a), chain(halves[1], wait_b)
    next(half_a)
    next(half_b)
    next(half_a)
    wait_b["ln1"] = wait_a["ln1_done"]
    n_rest = n_groups + 2
    _interleave(half_a, n_rest, half_b, n_rest + 1)


def _ffn(x, mixed, mod, w_out, w1, w2, ln1_w, ln1_b, ln2_w, ln2_b, layer, alpha, tb):
    bsz, s, d = x.shape
    nb = s // tb
    d_ff = w1.shape[1]
    const2 = lambda b, i: (0, 0)
    of_layer = lambda b, i: (layer, 0, 0)
    single = dict(pipeline_mode=pl.Buffered(1))
    return pl.pallas_call(
        functools.partial(_ffn_kernel, alpha=alpha),
        grid=(bsz, nb),
        in_specs=[pl.BlockSpec((1, tb, d), lambda b, i: (b, i, 0)),
                  pl.BlockSpec((1, tb, d), lambda b, i: (b, i, 0)),
                  pl.BlockSpec((1, 6, d), lambda b, i: (b, 0, 0)),
                  pl.BlockSpec((d, d), const2, **single),
                  pl.BlockSpec((d, d_ff), const2, **single),
                  pl.BlockSpec((d_ff, d), const2, **single),
                  pl.BlockSpec((None, 1, d), of_layer, **single),
                  pl.BlockSpec((None, 1, d), of_layer, **single),
                  pl.BlockSpec((None, 1, d), of_layer, **single),
                  pl.BlockSpec((None, 1, d), of_layer, **single)],
        out_specs=pl.BlockSpec((1, tb, d), lambda b, i: (b, i, 0)),
        out_shape=jax.ShapeDtypeStruct((bsz, s, d), x.dtype),
        scratch_shapes=[pltpu.VMEM((tb, d_ff), BF16)],
        compiler_params=pltpu.CompilerParams(
            dimension_semantics=("parallel", "parallel"), vmem_limit_bytes=VMEM_LIMIT),
        name="ffn",
    )(x, mixed, mod, w_out, w1, w2, _layer_row(ln1_w), _layer_row(ln1_b), _layer_row(ln2_w),
      _layer_row(ln2_b))


def kernel(x, c, w_ada, b_ada, w_in, ret_norm_w, gla_gate_w, gla_gate_b, gla_norm_w,
           w_out, ln1_w, ln1_b, w_ff1, w_ff2, ln2_w, ln2_b):
    depth = w_in.shape[0]
    bsz, s, d = x.shape
    alpha = (2.0 * depth) ** 0.25
    tb = min(512, s)
    cos_t, sin_t = (jnp.asarray(t) for t in _rotary_tables(s))
    for l in range(depth):
        mod = _adaln(c, w_ada, b_ada, l)
        mixed, wo_bf, w1_bf, w2_bf = _projmix(x, mod, cos_t, sin_t, w_in, gla_gate_w, gla_gate_b,
                                              ret_norm_w, gla_norm_w, w_out, w_ff1, w_ff2, l, tb)
        x = _ffn(x, mixed, mod, wo_bf, w1_bf, w2_bf, ln1_w, ln1_b, ln2_w, ln2_b, l, alpha,
                 min(1024, s))
    return x
```

```python
import functools
import math

import numpy as np
import jax
import jax.numpy as jnp
from jax import lax
from jax.experimental import pallas as pl
from jax.experimental.pallas import tpu as pltpu

CHUNK = 64
RET_HEADS = 4
RET_D = 128
RET_WIDTH = RET_HEADS * RET_D
GLA_HEADS = 4
GLA_DK = 64
GLA_DV = 128
GLA_KEY_WIDTH = GLA_HEADS * GLA_DK
GLA_WIDTH = GLA_HEADS * GLA_DV
D_MIX = RET_WIDTH + GLA_WIDTH
GATE_RANK = 16
GATE_TAU = 16.0
ROPE_BASE = 10000.0
LN_EPS = 1e-5
LANES = 128
SLAB = 4 * RET_WIDTH + 2 * GLA_KEY_WIDTH + 2 * GLA_WIDTH
OFF_RQ, OFF_RK, OFF_RV, OFF_RG = 0, 512, 1024, 1536
OFF_GQ, OFF_GK, OFF_GV, OFF_GG = 2048, 2304, 2560, 3072
MIX_ROWS = 256
VMEM_LIMIT = 56 * 1024 * 1024
LOG2E = math.log2(math.e)

BF16 = jnp.bfloat16
F32 = jnp.float32


def _dot(a, b):
    return jnp.dot(a, b, preferred_element_type=F32)


def _dot_nt(a, b):
    return lax.dot_general(a, b, (((1,), (1,)), ((), ())), preferred_element_type=F32)


def _dot_tn(a, b):
    return lax.dot_general(a, b, (((0,), (0,)), ((), ())), preferred_element_type=F32)


def _zero_of(v):
    bits = lax.shift_right_logical(pltpu.bitcast(v, jnp.uint32), jnp.uint32(32))
    return pltpu.bitcast(bits, F32)


def _ln(x, not_before=None):
    mu = jnp.mean(x, axis=-1, keepdims=True)
    if not_before is not None:
        mu = mu + _zero_of(not_before)[0:1, 0:1]
    xc = x - mu
    var = jnp.mean(xc * xc, axis=-1, keepdims=True)
    return xc * lax.rsqrt(var + LN_EPS)


def _silu(x):
    return x * (1.0 / (1.0 + jnp.exp(-x)))


def _adaln_kernel(c_ref, w_ref, b_ref, o_ref):
    c = c_ref[...]
    val = _dot(_silu(c).astype(BF16), w_ref[...].astype(BF16)) + b_ref[...]
    o_ref[:, pl.ds(pl.program_id(0), 1), :] = val[:, None, :]


def _adaln(c, w_ada, b_ada, layer):
    bsz, d = c.shape
    depth, _, n = w_ada.shape
    return pl.pallas_call(
        _adaln_kernel,
        grid=(n // d,),
        in_specs=[pl.BlockSpec((bsz, d), lambda j: (0, 0)),
                  pl.BlockSpec((None, d, d), lambda j: (layer, 0, j)),
                  pl.BlockSpec((None, 1, d), lambda j: (layer, 0, j))],
        out_specs=pl.BlockSpec((bsz, n // d, d), lambda j: (0, 0, 0)),
        out_shape=jax.ShapeDtypeStruct((bsz, n // d, d), F32),
        compiler_params=pltpu.CompilerParams(vmem_limit_bytes=VMEM_LIMIT),
        name="adaln",
    )(c, w_ada, b_ada.reshape(depth, 1, n))


def _rotary_tables(s):
    half = RET_D // 2
    inv = 1.0 / (ROPE_BASE ** (np.arange(half, dtype=np.float64) / (half - 1)))
    ang = np.arange(s, dtype=np.float64)[:, None] * inv[None, :]
    cos = np.concatenate([np.cos(ang), np.cos(ang)], axis=1)
    sin = np.concatenate([-np.sin(ang), np.sin(ang)], axis=1)
    return cos.astype(np.float32), sin.astype(np.float32)


def _retention_tables(tb):
    log_gamma = np.log(1.0 - 2.0 ** (-5.0 - np.arange(RET_HEADS, dtype=np.float64)))
    idx = np.arange(tb)
    dist = np.abs(idx[:, None] - idx[None, :])
    visible = (idx[None, :] // CHUNK) <= (idx[:, None] // CHUNK)
    scale = RET_D ** -0.5
    dmask = np.exp(log_gamma[:, None, None] * dist[None]) * visible[None] * scale
    qdec = np.exp(log_gamma[:, None] * (idx[None, :] + 1.0)) * scale
    kdec = np.exp(log_gamma[:, None] * (tb - 1.0 - idx[None, :]))
    qdec = np.broadcast_to(qdec[:, :, None], (RET_HEADS, tb, RET_D))
    kdec = np.broadcast_to(kdec[:, :, None], (RET_HEADS, tb, RET_D))
    block_decay = [float(np.exp(lg * tb)) for lg in log_gamma]
    return (dmask.astype(np.float32), np.ascontiguousarray(qdec, np.float32),
            np.ascontiguousarray(kdec, np.float32), block_decay)


def _cumsum_matrix(tb):
    t = np.arange(tb)
    same_chunk = (t[:, None] // CHUNK) == (t[None, :] // CHUNK)
    return (same_chunk & (t[None, :] <= t[:, None])).astype(np.float32)


def _proj_stage(x_ref, mod_ref, cos_ref, sin_ref, w_ref, wtail_ref, gw_ref, gb_ref, slab, la_out):
    x = x_ref[0]
    shift1 = mod_ref[0, 0:1, :]
    scale1 = mod_ref[0, 1:2, :]
    ub = (_ln(x) * (1.0 + scale1) + shift1).astype(BF16)
    yield
    cos = cos_ref[...]
    sin = sin_ref[...]
    group = 256
    for j in range(SLAB // group):
        p = _dot(ub, w_ref[:, j * group:(j + 1) * group])
        if j * group < OFF_RV:
            for h in range(group // RET_D):
                xh = p[:, h * RET_D:(h + 1) * RET_D]
                rot = xh * cos + pltpu.roll(xh, RET_D // 2, 1) * sin
                slab[:, j * group + h * RET_D:j * group + (h + 1) * RET_D] = rot.astype(BF16)
        else:
            slab[:, j * group:(j + 1) * group] = p.astype(BF16)
        yield
    glr = _dot_nt(ub, wtail_ref[...])
    logit = _dot(glr.astype(BF16), gw_ref[...].astype(BF16)) + gb_ref[...]
    log_sig = jnp.minimum(logit, 0.0) - jnp.log1p(jnp.exp(-jnp.abs(logit)))
    la_out[...] = log_sig * (LOG2E / GATE_TAU)
    yield


def _row_of_block(b, block, row):
    n = b.shape[0] // block
    b3 = b.reshape(n, block, b.shape[1])
    return jnp.broadcast_to(b3[:, row:row + 1, :], b3.shape).reshape(b.shape)


def _mix_stage(slab, la_ref, r0, dmask_ref, qdec_ref, kdec_ref, csum_ref, rnw_ref, gnw_ref,
               out_ref, r_state, g_state, block_decay):
    tb = MIX_ROWS
    blk = slice(r0, r0 + tb)
    for h in range(RET_HEADS):
        q = slab[blk, OFF_RQ + h * RET_D:OFF_RQ + (h + 1) * RET_D]
        k = slab[blk, OFF_RK + h * RET_D:OFF_RK + (h + 1) * RET_D]
        v = slab[blk, OFF_RV + h * RET_D:OFF_RV + (h + 1) * RET_D]
        g = slab[blk, OFF_RG + h * RET_D:OFF_RG + (h + 1) * RET_D].astype(F32)
        p = (_dot_nt(q, k) * dmask_ref[h]).astype(BF16)
        r_prev = r_state[h]
        o = _dot(p, v) + _dot(q, r_prev.astype(BF16)) * qdec_ref[h]
        kd = (k.astype(F32) * kdec_ref[h]).astype(BF16)
        r_state[h] = block_decay[h] * r_prev + _dot_tn(kd, v)
        o = o - jnp.mean(o, axis=-1, keepdims=True)
        o = o * lax.rsqrt(jnp.mean(o * o, axis=-1, keepdims=True) + LN_EPS)
        o = o * rnw_ref[:, h * RET_D:(h + 1) * RET_D] * _silu(g)
        out_ref[0, blk, h * RET_D:(h + 1) * RET_D] = o.astype(out_ref.dtype)
        yield

    la = la_ref[blk, :]
    la_hi = la.astype(BF16)
    la_lo = (la - la_hi.astype(F32)).astype(BF16)
    csum = csum_ref[...]
    b = _dot(csum, la_hi) + _dot(csum, la_lo)
    row = lax.broadcasted_iota(jnp.int32, (tb, GLA_KEY_WIDTH), 0)
    psis = [-jnp.abs(b - _row_of_block(b, 2 * c, c - 1)) for c in (32, 16, 8, 4)]
    la_next = pltpu.roll(la, tb - 1, 0)
    la_prev = pltpu.roll(la, 1, 0)
    r4 = row & 3
    psis.append(jnp.where(r4 == 0, la_next, jnp.where(r4 == 1, 0.0, jnp.where(r4 == 2, la, la + la_prev))))
    psis.append(jnp.where((row & 1) == 1, la, 0.0))
    gq = slab[blk, OFF_GQ:OFF_GQ + GLA_KEY_WIDTH] * jnp.asarray(GLA_DK ** -0.5, BF16)
    gk = slab[blk, OFF_GK:OFF_GK + GLA_KEY_WIDTH]
    phis = [jnp.exp2(p).astype(BF16) for p in psis]
    q_lvl = [gq] + [gq * phi for phi in phis]
    phis = [None] + phis
    q_state = gq * jnp.exp2(b).astype(BF16)
    k_state = gk * jnp.exp2(_row_of_block(b, CHUNK, CHUNK - 1) - b).astype(BF16)
    yield

    lane = lax.broadcasted_iota(jnp.int32, (CHUNK, LANES), 1)
    low_half = lane < GLA_DK
    zero_k = jnp.zeros((CHUNK, LANES), BF16)
    zero_s = jnp.zeros((GLA_DV, LANES), BF16)
    t_xor_s = lax.broadcasted_iota(jnp.int32, (CHUNK, GLA_HEADS * CHUNK), 0) ^ (
        lax.broadcasted_iota(jnp.int32, (CHUNK, GLA_HEADS * CHUNK), 1) & (CHUNK - 1))
    lane_s = lax.broadcasted_iota(jnp.int32, (GLA_DV, LANES), 1)
    low_half_s = lane_s < GLA_DK

    def lane_tiles(piece, tile, zero, n_tiles):
        return jnp.concatenate([piece if j == tile else zero for j in range(n_tiles)], axis=1)

    for ch in range(tb // CHUNK):
        rows = slice(ch * CHUNK, (ch + 1) * CHUNK)
        orow = slice(r0 + ch * CHUNK, r0 + (ch + 1) * CHUNK)
        k_heads = [jnp.where(low_half if h % 2 == 0 else ~low_half,
                             gk[rows, (h // 2) * LANES:(h // 2 + 1) * LANES], zero_k)
                   for h in range(GLA_HEADS)]
        a_lvl = []
        for lvl in range(len(q_lvl)):
            pieces = []
            for h in range(GLA_HEADS):
                tile = h // 2
                kh = k_heads[h]
                if phis[lvl] is not None:
                    kh = kh * phis[lvl][rows, tile * LANES:(tile + 1) * LANES]
                pieces.append(lane_tiles(kh, tile, zero_k, 2))
            k_stack = jnp.concatenate(pieces, axis=0)
            a_lvl.append(_dot_nt(q_lvl[lvl][rows], k_stack))
        a = a_lvl[0]
        for i, c in enumerate((1, 2, 4, 8, 16, 32)):
            a = jnp.where(t_xor_s >= c, a_lvl[len(a_lvl) - 1 - i], a)
        a = a.astype(BF16)
        gv = slab[orow, OFF_GV:OFF_GV + GLA_WIDTH]
        v_bd = jnp.concatenate(
            [lane_tiles(gv[:, h * GLA_DV:(h + 1) * GLA_DV], h, jnp.zeros((CHUNK, GLA_DV), BF16), GLA_HEADS)
             for h in range(GLA_HEADS)], axis=0)
        s_bd = jnp.concatenate(
            [lane_tiles(g_state[h].astype(BF16), h // 2, zero_s, 2) for h in range(GLA_HEADS)],
            axis=0)
        o_all = _dot(a, v_bd) + _dot_nt(q_state[rows], s_bd)
        decay = jnp.exp2(b[ch * CHUNK + CHUNK - 1:(ch + 1) * CHUNK, :])
        for h in range(GLA_HEADS):
            tile = h // 2
            upd = _dot_tn(gv[:, h * GLA_DV:(h + 1) * GLA_DV], k_state[rows, tile * LANES:(tile + 1) * LANES])
            upd = jnp.where(low_half_s if h % 2 == 0 else ~low_half_s, upd, 0.0)
            g_state[h] = g_state[h] * decay[:, tile * LANES:(tile + 1) * LANES] + upd
        for h in range(GLA_HEADS):
            o = o_all[:, h * GLA_DV:(h + 1) * GLA_DV]
            o = o * lax.rsqrt(jnp.mean(o * o, axis=-1, keepdims=True) + LN_EPS)
            gg = slab[orow, OFF_GG + h * GLA_DV:OFF_GG + (h + 1) * GLA_DV].astype(F32)
            o = o * gnw_ref[:, h * GLA_DV:(h + 1) * GLA_DV] * _silu(gg)
            out_ref[0, orow, RET_WIDTH + h * GLA_DV:RET_WIDTH + (h + 1) * GLA_DV] = o.astype(out_ref.dtype)
        yield


def _interleave(gen_a, n_a, gen_b, n_b):
    done_a = done_b = 0
    while done_a < n_a or done_b < n_b:
        if done_b >= n_b or (done_a < n_a and (done_a + 0.5) * n_b <= (done_b + 0.5) * n_a):
            next(gen_a)
            done_a += 1
        else:
            next(gen_b)
            done_b += 1
    for g in (gen_a, gen_b):
        assert next(g, "done") == "done", "piece count mismatch"


def _projmix_kernel(x_ref, mod_ref, cos_ref, sin_ref, w_ref, gw_ref, gb_ref,
                    dmask_ref, qdec_ref, kdec_ref, csum_ref, rnw_ref, gnw_ref,
                    wo_ref, w1_ref, w2_ref,
                    out_ref, wo_bf_ref, w1_bf_ref, w2_bf_ref,
                    w_bf, w_tail, slab2, la2, r_state, g_state,
                    *, tb, blocks_per_seq, n_blocks, block_decay):
    j = pl.program_id(0)
    n_proj = 2 + SLAB // 256
    n_mix = (tb // MIX_ROWS) * (RET_HEADS + 1 + MIX_ROWS // CHUNK)

    @pl.when(j == 0)
    def _():
        slab2[1] = jnp.zeros(slab2.shape[1:], slab2.dtype)
        la2[1] = jnp.zeros(la2.shape[1:], la2.dtype)
        for g in range(0, SLAB, 256):
            w_bf[:, g:g + 256] = w_ref[g:g + 256, :].T.astype(BF16)
        w_tail[...] = w_ref[SLAB:SLAB + GATE_RANK, :].astype(BF16)

    wo_bf_ref[...] = wo_ref[...].astype(BF16)
    w1_bf_ref[...] = w1_ref[...].astype(BF16)
    w2_bf_ref[...] = w2_ref[...].astype(BF16)

    @pl.when((j == 0) | ((j - 1) % blocks_per_seq == 0))
    def _():
        r_state[...] = jnp.zeros_like(r_state)
        g_state[...] = jnp.zeros_like(g_state)

    slot = j % 2
    slab_w, la_w = slab2.at[slot], la2.at[slot]
    slab_r, la_r = slab2.at[1 - slot], la2.at[1 - slot]

    def mix():
        for r0 in range(0, tb, MIX_ROWS):
            yield from _mix_stage(slab_r, la_r, r0, dmask_ref, qdec_ref, kdec_ref, csum_ref,
                                  rnw_ref, gnw_ref, out_ref, r_state, g_state, block_decay)

    proj = _proj_stage(x_ref, mod_ref, cos_ref, sin_ref, w_bf, w_tail, gw_ref, gb_ref, slab_w, la_w)
    _interleave(proj, n_proj, mix(), n_mix)


def _layer_row(v):
    return v.reshape(v.shape[0], 1, v.shape[1])


def _projmix(x, mod, cos_t, sin_t, w_in, gate_w, gate_b, ret_norm_w, gla_norm_w,
             w_out, w_ff1, w_ff2, layer, tb):
    bsz, s, d = x.shape
    d_ff = w_ff1.shape[2]
    bps = s // tb
    nblk = bsz * bps
    dmask, qdec, kdec, block_decay = _retention_tables(MIX_ROWS)
    csum = _cumsum_matrix(MIX_ROWS)
    const2 = lambda j: (0, 0)
    const3 = lambda j: (0, 0, 0)
    of_layer = lambda j: (layer, 0, 0)
    proj_blk = lambda j: jnp.minimum(j, nblk - 1)
    mix_blk = lambda j: jnp.maximum(j - 1, 0)
    single = dict(pipeline_mode=pl.Buffered(1))
    wo_rows, w1_rows, w2_rows = d // nblk, d // nblk, d_ff // nblk
    return pl.pallas_call(
        functools.partial(_projmix_kernel, tb=tb, blocks_per_seq=bps, n_blocks=nblk,
                          block_decay=block_decay),
        grid=(nblk + 1,),
        in_specs=[pl.BlockSpec((1, tb, d), lambda j: (proj_blk(j) // bps, proj_blk(j) % bps, 0)),
                  pl.BlockSpec((1, 6, d), lambda j: (proj_blk(j) // bps, 0, 0)),
                  pl.BlockSpec((tb, RET_D), lambda j: (proj_blk(j) % bps, 0)),
                  pl.BlockSpec((tb, RET_D), lambda j: (proj_blk(j) % bps, 0)),
                  pl.BlockSpec((None, SLAB + GATE_RANK, d), of_layer, **single),
                  pl.BlockSpec((None, GATE_RANK, GLA_KEY_WIDTH), of_layer, **single),
                  pl.BlockSpec((None, 1, GLA_KEY_WIDTH), of_layer, **single),
                  pl.BlockSpec(dmask.shape, const3, **single),
                  pl.BlockSpec(qdec.shape, const3, **single),
                  pl.BlockSpec(kdec.shape, const3, **single),
                  pl.BlockSpec(csum.shape, const2, **single),
                  pl.BlockSpec((None, 1, RET_WIDTH), of_layer, **single),
                  pl.BlockSpec((None, 1, GLA_WIDTH), of_layer, **single),
                  pl.BlockSpec((None, wo_rows, d), lambda j: (layer, proj_blk(j), 0)),
                  pl.BlockSpec((None, w1_rows, d_ff), lambda j: (layer, proj_blk(j), 0)),
                  pl.BlockSpec((None, w2_rows, d), lambda j: (layer, proj_blk(j), 0))],
        out_specs=[pl.BlockSpec((1, tb, D_MIX), lambda j: (mix_blk(j) // bps, mix_blk(j) % bps, 0)),
                   pl.BlockSpec((wo_rows, d), lambda j: (proj_blk(j), 0)),
                   pl.BlockSpec((w1_rows, d_ff), lambda j: (proj_blk(j), 0)),
                   pl.BlockSpec((w2_rows, d), lambda j: (proj_blk(j), 0))],
        out_shape=[jax.ShapeDtypeStruct((bsz, s, D_MIX), BF16),
                   jax.ShapeDtypeStruct((d, d), BF16),
                   jax.ShapeDtypeStruct((d, d_ff), BF16),
                   jax.ShapeDtypeStruct((d_ff, d), BF16)],
        scratch_shapes=[pltpu.VMEM((d, SLAB), BF16), pltpu.VMEM((GATE_RANK, d), BF16),
                        pltpu.VMEM((2, tb, SLAB), BF16), pltpu.VMEM((2, tb, GLA_KEY_WIDTH), F32),
                        pltpu.VMEM((RET_HEADS, RET_D, RET_D), F32),
                        pltpu.VMEM((GLA_HEADS, GLA_DV, LANES), F32)],
        compiler_params=pltpu.CompilerParams(
            dimension_semantics=("arbitrary",), vmem_limit_bytes=VMEM_LIMIT),
        name="projmix",
    )(x, mod, cos_t, sin_t, jnp.swapaxes(w_in, 1, 2), gate_w, _layer_row(gate_b),
      jnp.asarray(dmask), jnp.asarray(qdec), jnp.asarray(kdec), jnp.asarray(csum, BF16),
      _layer_row(ret_norm_w), _layer_row(gla_norm_w), w_out, w_ff1, w_ff2)


def _ffn_kernel(x_ref, mix_ref, mod_ref, wo_ref, w1_ref, w2_ref, ln1w_ref, ln1b_ref,
                ln2w_ref, ln2b_ref, out_ref, h_ref, *, alpha):
    gate1 = mod_ref[0, 2:3, :]
    shift2 = mod_ref[0, 3:4, :]
    scale2 = mod_ref[0, 4:5, :]
    gate2 = mod_ref[0, 5:6, :]
    d_ff = w1_ref.shape[1]
    n_groups = 4
    group = d_ff // n_groups
    tb = x_ref.shape[1]
    halves = (slice(0, tb // 2), slice(tb // 2, tb))

    def chain(rows, wait):
        m = _dot(mix_ref[0, rows, :], wo_ref[...])
        yield
        x1 = (_ln(alpha * x_ref[0, rows, :] + gate1 * m, wait.get("ln1")) * ln1w_ref[...]
              + ln1b_ref[...])
        u2f = _ln(x1) * (1.0 + scale2) + shift2
        wait["ln1_done"] = u2f[u2f.shape[0] - 8:, u2f.shape[1] - LANES:]
        u2 = u2f.astype(BF16)
        yield
        for j in range(n_groups):
            hj = jnp.maximum(_dot(u2, w1_ref[:, j * group:(j + 1) * group]), 0.0)
            h_ref[rows, j * group:(j + 1) * group] = (hj * hj).astype(BF16)
            yield
        f = _dot(h_ref[rows, :], w2_ref[...])
        yield
        out_ref[0, rows, :] = _ln(alpha * x1 + gate2 * f) * ln2w_ref[...] + ln2b_ref[...]
        yield

    wait_a, wait_b = {}, {}
    half_a, half_b = chain(halves[0], wait_a), chain(halves[1], wait_b)
    next(half_a)
    next(half_b)
    next(half_a)
    wait_b["ln1"] = wait_a["ln1_done"]
    n_rest = n_groups + 2
    _interleave(half_a, n_rest, half_b, n_rest + 1)


def _ffn(x, mixed, mod, w_out, w1, w2, ln1_w, ln1_b, ln2_w, ln2_b, layer, alpha, tb):
    bsz, s, d = x.shape
    nb = s // tb
    d_ff = w1.shape[1]
    const2 = lambda b, i: (0, 0)
    of_layer = lambda b, i: (layer, 0, 0)
    single = dict(pipeline_mode=pl.Buffered(1))
    return pl.pallas_call(
        functools.partial(_ffn_kernel, alpha=alpha),
        grid=(bsz, nb),
        in_specs=[pl.BlockSpec((1, tb, d), lambda b, i: (b, i, 0)),
                  pl.BlockSpec((1, tb, d), lambda b, i: (b, i, 0)),
                  pl.BlockSpec((1, 6, d), lambda b, i: (b, 0, 0)),
                  pl.BlockSpec((d, d), const2, **single),
                  pl.BlockSpec((d, d_ff), const2, **single),
                  pl.BlockSpec((d_ff, d), const2, **single),
                  pl.BlockSpec((None, 1, d), of_layer, **single),
                  pl.BlockSpec((None, 1, d), of_layer, **single),
                  pl.BlockSpec((None, 1, d), of_layer, **single),
                  pl.BlockSpec((None, 1, d), of_layer, **single)],
        out_specs=pl.BlockSpec((1, tb, d), lambda b, i: (b, i, 0)),
        out_shape=jax.ShapeDtypeStruct((bsz, s, d), x.dtype),
        scratch_shapes=[pltpu.VMEM((tb, d_ff), BF16)],
        compiler_params=pltpu.CompilerParams(
            dimension_semantics=("parallel", "parallel"), vmem_limit_bytes=VMEM_LIMIT),
        name="ffn",
    )(x, mixed, mod, w_out, w1, w2, _layer_row(ln1_w), _layer_row(ln1_b), _layer_row(ln2_w),
      _layer_row(ln2_b))


def kernel(x, c, w_ada, b_ada, w_in, ret_norm_w, gla_gate_w, gla_gate_b, gla_norm_w,
           w_out, ln1_w, ln1_b, w_ff1, w_ff2, ln2_w, ln2_b):
    depth = w_in.shape[0]
    bsz, s, d = x.shape
    alpha = (2.0 * depth) ** 0.25
    tb = min(512, s)
    cos_t, sin_t = (jnp.asarray(t) for t in _rotary_tables(s))
    for l in range(depth):
        mod = _adaln(c, w_ada, b_ada, l)
        mixed, wo_bf, w1_bf, w2_bf = _projmix(x, mod, cos_t, sin_t, w_in, gla_gate_w, gla_gate_b,
                                              ret_norm_w, gla_norm_w, w_out, w_ff1, w_ff2, l, tb)
        x = _ffn(x, mixed, mod, wo_bf, w1_bf, w2_bf, ln1_w, ln1_b, ln2_w, ln2_b, l, alpha,
                 min(1024, s))
    return x
```

```python
import functools
import math

import numpy as np
import jax
import jax.numpy as jnp
from jax import lax
from jax.experimental import pallas as pl
from jax.experimental.pallas import tpu as pltpu

CHUNK = 64
RET_HEADS = 4
RET_D = 128
RET_WIDTH = RET_HEADS * RET_D
GLA_HEADS = 4
GLA_DK = 64
GLA_DV = 128
GLA_KEY_WIDTH = GLA_HEADS * GLA_DK
GLA_WIDTH = GLA_HEADS * GLA_DV
D_MIX = RET_WIDTH + GLA_WIDTH
GATE_RANK = 16
GATE_TAU = 16.0
ROPE_BASE = 10000.0
LN_EPS = 1e-5
LANES = 128
SLAB = 4 * RET_WIDTH + 2 * GLA_KEY_WIDTH + 2 * GLA_WIDTH
OFF_RQ, OFF_RK, OFF_RV, OFF_RG = 0, 512, 1024, 1536
OFF_GQ, OFF_GK, OFF_GV, OFF_GG = 2048, 2304, 2560, 3072
MIX_ROWS = 256
BOUNDED_LOG2_DECAY = 60.0
VMEM_LIMIT = 56 * 1024 * 1024
LOG2E = math.log2(math.e)

BF16 = jnp.bfloat16
F32 = jnp.float32


def _dot(a, b):
    return jnp.dot(a, b, preferred_element_type=F32)


def _dot_nt(a, b):
    return lax.dot_general(a, b, (((1,), (1,)), ((), ())), preferred_element_type=F32)


def _dot_tn(a, b):
    return lax.dot_general(a, b, (((0,), (0,)), ((), ())), preferred_element_type=F32)


def _zero_of(v):
    bits = lax.shift_right_logical(pltpu.bitcast(v, jnp.uint32), jnp.uint32(32))
    return pltpu.bitcast(bits, F32)


def _ln(x, not_before=None):
    mu = jnp.mean(x, axis=-1, keepdims=True)
    if not_before is not None:
        mu = mu + _zero_of(not_before)[0:1, 0:1]
    xc = x - mu
    var = jnp.mean(xc * xc, axis=-1, keepdims=True)
    return xc * lax.rsqrt(var + LN_EPS)


def _silu(x):
    return x * (1.0 / (1.0 + jnp.exp(-x)))


def _adaln_kernel(c_ref, w_ref, b_ref, o_ref):
    c = c_ref[...]
    val = _dot(_silu(c).astype(BF16), w_ref[...].astype(BF16)) + b_ref[...]
    o_ref[:, pl.ds(pl.program_id(0), 1), :] = val[:, None, :]


def _adaln(c, w_ada, b_ada, layer):
    bsz, d = c.shape
    depth, _, n = w_ada.shape
    return pl.pallas_call(
        _adaln_kernel,
        grid=(n // d,),
        in_specs=[pl.BlockSpec((bsz, d), lambda j: (0, 0)),
                  pl.BlockSpec((None, d, d), lambda j: (layer, 0, j)),
                  pl.BlockSpec((None, 1, d), lambda j: (layer, 0, j))],
        out_specs=pl.BlockSpec((bsz, n // d, d), lambda j: (0, 0, 0)),
        out_shape=jax.ShapeDtypeStruct((bsz, n // d, d), F32),
        compiler_params=pltpu.CompilerParams(vmem_limit_bytes=VMEM_LIMIT),
        name="adaln",
    )(c, w_ada, b_ada.reshape(depth, 1, n))


def _rotary_tables(s):
    half = RET_D // 2
    inv = 1.0 / (ROPE_BASE ** (np.arange(half, dtype=np.float64) / (half - 1)))
    ang = np.arange(s, dtype=np.float64)[:, None] * inv[None, :]
    cos = np.concatenate([np.cos(ang), np.cos(ang)], axis=1)
    sin = np.concatenate([-np.sin(ang), np.sin(ang)], axis=1)
    return cos.astype(np.float32), sin.astype(np.float32)


def _retention_tables(tb):
    log_gamma = np.log(1.0 - 2.0 ** (-5.0 - np.arange(RET_HEADS, dtype=np.float64)))
    idx = np.arange(tb)
    dist = np.abs(idx[:, None] - idx[None, :])
    visible = (idx[None, :] // CHUNK) <= (idx[:, None] // CHUNK)
    scale = RET_D ** -0.5
    dmask = np.exp(log_gamma[:, None, None] * dist[None]) * visible[None] * scale
    qdec = np.exp(log_gamma[:, None] * (idx[None, :] + 1.0)) * scale
    kdec = np.exp(log_gamma[:, None] * (tb - 1.0 - idx[None, :]))
    qdec = np.broadcast_to(qdec[:, :, None], (RET_HEADS, tb, RET_D))
    kdec = np.broadcast_to(kdec[:, :, None], (RET_HEADS, tb, RET_D))
    block_decay = [float(np.exp(lg * tb)) for lg in log_gamma]
    return (dmask.astype(np.float32), np.ascontiguousarray(qdec, np.float32),
            np.ascontiguousarray(kdec, np.float32), block_decay)


def _cumsum_matrix(tb):
    t = np.arange(tb)
    same_chunk = (t[:, None] // CHUNK) == (t[None, :] // CHUNK)
    return (same_chunk & (t[None, :] <= t[:, None])).astype(np.float32)


def _proj_stage(x_ref, mod_ref, cos_ref, sin_ref, w_ref, wtail_ref, gw_ref, gb_ref, slab, la_out,
                bounded_ref, slot):
    x = x_ref[0]
    shift1 = mod_ref[0, 0:1, :]
    scale1 = mod_ref[0, 1:2, :]
    ub = (_ln(x) * (1.0 + scale1) + shift1).astype(BF16)
    yield
    cos = cos_ref[...]
    sin = sin_ref[...]
    group = 256
    for j in range(SLAB // group):
        p = _dot(ub, w_ref[:, j * group:(j + 1) * group])
        if j * group < OFF_RV:
            for h in range(group // RET_D):
                xh = p[:, h * RET_D:(h + 1) * RET_D]
                rot = xh * cos + pltpu.roll(xh, RET_D // 2, 1) * sin
                slab[:, j * group + h * RET_D:j * group + (h + 1) * RET_D] = rot.astype(BF16)
        else:
            slab[:, j * group:(j + 1) * group] = p.astype(BF16)
        yield
    glr = _dot_nt(ub, wtail_ref[...])
    logit = _dot(glr.astype(BF16), gw_ref[...].astype(BF16)) + gb_ref[...]
    log_sig = jnp.minimum(logit, 0.0) - jnp.log1p(jnp.exp(-jnp.abs(logit)))
    la = log_sig * (LOG2E / GATE_TAU)
    la_out[...] = la
    bounded_ref[slot] = (jnp.min(la) * CHUNK >= -BOUNDED_LOG2_DECAY).astype(jnp.int32)
    yield


def _row_of_block(b, block, row):
    n = b.shape[0] // block
    b3 = b.reshape(n, block, b.shape[1])
    return jnp.broadcast_to(b3[:, row:row + 1, :], b3.shape).reshape(b.shape)


def _mix_stage(slab, la_ref, r0, dmask_ref, qdec_ref, kdec_ref, csum_ref, rnw_ref, gnw_ref,
               out_ref, r_state, g_state, block_decay, bounded):
    tb = MIX_ROWS
    blk = slice(r0, r0 + tb)
    for h in range(RET_HEADS):
        q = slab[blk, OFF_RQ + h * RET_D:OFF_RQ + (h + 1) * RET_D]
        k = slab[blk, OFF_RK + h * RET_D:OFF_RK + (h + 1) * RET_D]
        v = slab[blk, OFF_RV + h * RET_D:OFF_RV + (h + 1) * RET_D]
        g = slab[blk, OFF_RG + h * RET_D:OFF_RG + (h + 1) * RET_D].astype(F32)
        p = (_dot_nt(q, k) * dmask_ref[h]).astype(BF16)
        r_prev = r_state[h]
        o = _dot(p, v) + _dot(q, r_prev.astype(BF16)) * qdec_ref[h]
        kd = (k.astype(F32) * kdec_ref[h]).astype(BF16)
        r_state[h] = block_decay[h] * r_prev + _dot_tn(kd, v)
        o = o - jnp.mean(o, axis=-1, keepdims=True)
        o = o * lax.rsqrt(jnp.mean(o * o, axis=-1, keepdims=True) + LN_EPS)
        o = o * rnw_ref[:, h * RET_D:(h + 1) * RET_D] * _silu(g)
        out_ref[0, blk, h * RET_D:(h + 1) * RET_D] = o.astype(out_ref.dtype)
        yield

    la = la_ref[blk, :]
    la_hi = la.astype(BF16)
    la_lo = (la - la_hi.astype(F32)).astype(BF16)
    csum = csum_ref[...]
    b = _dot(csum, la_hi) + _dot(csum, la_lo)
    gq = slab[blk, OFF_GQ:OFF_GQ + GLA_KEY_WIDTH] * jnp.asarray(GLA_DK ** -0.5, BF16)
    gk = slab[blk, OFF_GK:OFF_GK + GLA_KEY_WIDTH]
    q_state = gq * jnp.exp2(b).astype(BF16)
    k_state = gk * jnp.exp2(_row_of_block(b, CHUNK, CHUNK - 1) - b).astype(BF16)
    if bounded:
        grow = jnp.exp2(-b).astype(BF16)
        q_lvl = [q_state, gq * grow]
        phis = [grow, jnp.exp2(b).astype(BF16)]
    else:
        row = lax.broadcasted_iota(jnp.int32, (tb, GLA_KEY_WIDTH), 0)
        psis = [-jnp.abs(b - _row_of_block(b, 2 * c, c - 1)) for c in (32, 16, 8, 4)]
        la_next = pltpu.roll(la, tb - 1, 0)
        la_prev = pltpu.roll(la, 1, 0)
        r4 = row & 3
        psis.append(jnp.where(r4 == 0, la_next,
                              jnp.where(r4 == 1, 0.0, jnp.where(r4 == 2, la, la + la_prev))))
        psis.append(jnp.where((row & 1) == 1, la, 0.0))
        phis = [jnp.exp2(p).astype(BF16) for p in psis]
        q_lvl = [gq] + [gq * phi for phi in phis]
        phis = [None] + phis
    yield

    lane = lax.broadcasted_iota(jnp.int32, (CHUNK, LANES), 1)
    low_half = lane < GLA_DK
    zero_k = jnp.zeros((CHUNK, LANES), BF16)
    zero_s = jnp.zeros((GLA_DV, LANES), BF16)
    t_idx = lax.broadcasted_iota(jnp.int32, (CHUNK, GLA_HEADS * CHUNK), 0)
    s_idx = lax.broadcasted_iota(jnp.int32, (CHUNK, GLA_HEADS * CHUNK), 1) & (CHUNK - 1)
    t_xor_s = t_idx ^ s_idx
    lane_s = lax.broadcasted_iota(jnp.int32, (GLA_DV, LANES), 1)
    low_half_s = lane_s < GLA_DK

    def lane_tiles(piece, tile, zero, n_tiles):
        return jnp.concatenate([piece if j == tile else zero for j in range(n_tiles)], axis=1)

    for ch in range(tb // CHUNK):
        rows = slice(ch * CHUNK, (ch + 1) * CHUNK)
        orow = slice(r0 + ch * CHUNK, r0 + (ch + 1) * CHUNK)
        k_heads = [jnp.where(low_half if h % 2 == 0 else ~low_half,
                             gk[rows, (h // 2) * LANES:(h // 2 + 1) * LANES], zero_k)
                   for h in range(GLA_HEADS)]
        a_lvl = []
        for lvl in range(len(q_lvl)):
            pieces = []
            for h in range(GLA_HEADS):
                tile = h // 2
                kh = k_heads[h]
                if phis[lvl] is not None:
                    kh = kh * phis[lvl][rows, tile * LANES:(tile + 1) * LANES]
                pieces.append(lane_tiles(kh, tile, zero_k, 2))
            k_stack = jnp.concatenate(pieces, axis=0)
            a_lvl.append(_dot_nt(q_lvl[lvl][rows], k_stack))
        if bounded:
            a = jnp.where(t_idx >= s_idx, a_lvl[0], a_lvl[1])
        else:
            a = a_lvl[0]
            for i, c in enumerate((1, 2, 4, 8, 16, 32)):
                a = jnp.where(t_xor_s >= c, a_lvl[len(a_lvl) - 1 - i], a)
        a = a.astype(BF16)
        gv = slab[orow, OFF_GV:OFF_GV + GLA_WIDTH]
        v_bd = jnp.concatenate(
            [lane_tiles(gv[:, h * GLA_DV:(h + 1) * GLA_DV], h, jnp.zeros((CHUNK, GLA_DV), BF16), GLA_HEADS)
             for h in range(GLA_HEADS)], axis=0)
        s_bd = jnp.concatenate(
            [lane_tiles(g_state[h].astype(BF16), h // 2, zero_s, 2) for h in range(GLA_HEADS)],
            axis=0)
        o_all = _dot(a, v_bd) + _dot_nt(q_state[rows], s_bd)
        decay = jnp.exp2(b[ch * CHUNK + CHUNK - 1:(ch + 1) * CHUNK, :])
        for h in range(GLA_HEADS):
            tile = h // 2
            upd = _dot_tn(gv[:, h * GLA_DV:(h + 1) * GLA_DV], k_state[rows, tile * LANES:(tile + 1) * LANES])
            upd = jnp.where(low_half_s if h % 2 == 0 else ~low_half_s, upd, 0.0)
            g_state[h] = g_state[h] * decay[:, tile * LANES:(tile + 1) * LANES] + upd
        for h in range(GLA_HEADS):
            o = o_all[:, h * GLA_DV:(h + 1) * GLA_DV]
            o = o * lax.rsqrt(jnp.mean(o * o, axis=-1, keepdims=True) + LN_EPS)
            gg = slab[orow, OFF_GG + h * GLA_DV:OFF_GG + (h + 1) * GLA_DV].astype(F32)
            o = o * gnw_ref[:, h * GLA_DV:(h + 1) * GLA_DV] * _silu(gg)
            out_ref[0, orow, RET_WIDTH + h * GLA_DV:RET_WIDTH + (h + 1) * GLA_DV] = o.astype(out_ref.dtype)
        yield


def _interleave(gen_a, n_a, gen_b, n_b):
    done_a = done_b = 0
    while done_a < n_a or done_b < n_b:
        if done_b >= n_b or (done_a < n_a and (done_a + 0.5) * n_b <= (done_b + 0.5) * n_a):
            next(gen_a)
            done_a += 1
        else:
            next(gen_b)
            done_b += 1
    for g in (gen_a, gen_b):
        assert next(g, "done") == "done", "piece count mismatch"


def _projmix_kernel(x_ref, mod_ref, cos_ref, sin_ref, w_ref, gw_ref, gb_ref,
                    dmask_ref, qdec_ref, kdec_ref, csum_ref, rnw_ref, gnw_ref,
                    wo_ref, w1_ref, w2_ref,
                    out_ref, wo_bf_ref, w1_bf_ref, w2_bf_ref,
                    w_bf, w_tail, slab2, la2, bounded_ref, r_state, g_state,
                    *, tb, blocks_per_seq, n_blocks, block_decay):
    j = pl.program_id(0)
    n_proj = 2 + SLAB // 256
    n_mix = (tb // MIX_ROWS) * (RET_HEADS + 1 + MIX_ROWS // CHUNK)

    @pl.when(j == 0)
    def _():
        for g in range(0, SLAB, 256):
            w_bf[:, g:g + 256] = w_ref[g:g + 256, :].T.astype(BF16)
        w_tail[...] = w_ref[SLAB:SLAB + GATE_RANK, :].astype(BF16)
        bounded_ref[1] = 0

    wo_bf_ref[...] = wo_ref[...].astype(BF16)
    w1_bf_ref[...] = w1_ref[...].astype(BF16)
    w2_bf_ref[...] = w2_ref[...].astype(BF16)

    @pl.when((j - 1) % blocks_per_seq == 0)
    def _():
        r_state[...] = jnp.zeros_like(r_state)
        g_state[...] = jnp.zeros_like(g_state)

    def project(buf):
        return _proj_stage(x_ref, mod_ref, cos_ref, sin_ref, w_bf, w_tail, gw_ref, gb_ref,
                           slab2.at[buf], la2.at[buf], bounded_ref, buf)

    def mix(buf, bounded):
        for r0 in range(0, tb, MIX_ROWS):
            yield from _mix_stage(slab2.at[buf], la2.at[buf], r0, dmask_ref, qdec_ref, kdec_ref,
                                  csum_ref, rnw_ref, gnw_ref, out_ref, r_state, g_state, block_decay,
                                  bounded)

    @pl.when(j == 0)
    def _():
        for _ in project(0):
            pass

    steady = (j > 0) & (j < n_blocks)
    bounded = bounded_ref[1 - j % 2] == 1
    for parity in (0, 1):
        @pl.when(steady & bounded & (j % 2 == parity))
        def _():
            _interleave(project(parity), n_proj, mix(1 - parity, True), n_mix)

    @pl.when(steady & jnp.logical_not(bounded))
    def _():
        _interleave(project(j % 2), n_proj, mix(1 - j % 2, False), n_mix)

    @pl.when(j == n_blocks)
    def _():
        for _ in mix((n_blocks - 1) % 2, False):
            pass


def _layer_row(v):
    return v.reshape(v.shape[0], 1, v.shape[1])


def _projmix(x, mod, cos_t, sin_t, w_in, gate_w, gate_b, ret_norm_w, gla_norm_w,
             w_out, w_ff1, w_ff2, layer, tb):
    bsz, s, d = x.shape
    d_ff = w_ff1.shape[2]
    bps = s // tb
    nblk = bsz * bps
    dmask, qdec, kdec, block_decay = _retention_tables(MIX_ROWS)
    csum = _cumsum_matrix(MIX_ROWS)
    const2 = lambda j: (0, 0)
    const3 = lambda j: (0, 0, 0)
    of_layer = lambda j: (layer, 0, 0)
    proj_blk = lambda j: jnp.minimum(j, nblk - 1)
    mix_blk = lambda j: jnp.maximum(j - 1, 0)
    single = dict(pipeline_mode=pl.Buffered(1))
    wo_rows, w1_rows, w2_rows = d // nblk, d // nblk, d_ff // nblk
    return pl.pallas_call(
        functools.partial(_projmix_kernel, tb=tb, blocks_per_seq=bps, n_blocks=nblk,
                          block_decay=block_decay),
        grid=(nblk + 1,),
        in_specs=[pl.BlockSpec((1, tb, d), lambda j: (proj_blk(j) // bps, proj_blk(j) % bps, 0)),
                  pl.BlockSpec((1, 6, d), lambda j: (proj_blk(j) // bps, 0, 0)),
                  pl.BlockSpec((tb, RET_D), lambda j: (proj_blk(j) % bps, 0)),
                  pl.BlockSpec((tb, RET_D), lambda j: (proj_blk(j) % bps, 0)),
                  pl.BlockSpec((None, SLAB + GATE_RANK, d), of_layer, **single),
                  pl.BlockSpec((None, GATE_RANK, GLA_KEY_WIDTH), of_layer, **single),
                  pl.BlockSpec((None, 1, GLA_KEY_WIDTH), of_layer, **single),
                  pl.BlockSpec(dmask.shape, const3, **single),
                  pl.BlockSpec(qdec.shape, const3, **single),
                  pl.BlockSpec(kdec.shape, const3, **single),
                  pl.BlockSpec(csum.shape, const2, **single),
                  pl.BlockSpec((None, 1, RET_WIDTH), of_layer, **single),
                  pl.BlockSpec((None, 1, GLA_WIDTH), of_layer, **single),
                  pl.BlockSpec((None, wo_rows, d), lambda j: (layer, proj_blk(j), 0)),
                  pl.BlockSpec((None, w1_rows, d_ff), lambda j: (layer, proj_blk(j), 0)),
                  pl.BlockSpec((None, w2_rows, d), lambda j: (layer, proj_blk(j), 0))],
        out_specs=[pl.BlockSpec((1, tb, D_MIX), lambda j: (mix_blk(j) // bps, mix_blk(j) % bps, 0)),
                   pl.BlockSpec((wo_rows, d), lambda j: (proj_blk(j), 0)),
                   pl.BlockSpec((w1_rows, d_ff), lambda j: (proj_blk(j), 0)),
                   pl.BlockSpec((w2_rows, d), lambda j: (proj_blk(j), 0))],
        out_shape=[jax.ShapeDtypeStruct((bsz, s, D_MIX), BF16),
                   jax.ShapeDtypeStruct((d, d), BF16),
                   jax.ShapeDtypeStruct((d, d_ff), BF16),
                   jax.ShapeDtypeStruct((d_ff, d), BF16)],
        scratch_shapes=[pltpu.VMEM((d, SLAB), BF16), pltpu.VMEM((GATE_RANK, d), BF16),
                        pltpu.VMEM((2, tb, SLAB), BF16), pltpu.VMEM((2, tb, GLA_KEY_WIDTH), F32),
                        pltpu.SMEM((2,), jnp.int32),
                        pltpu.VMEM((RET_HEADS, RET_D, RET_D), F32),
                        pltpu.VMEM((GLA_HEADS, GLA_DV, LANES), F32)],
        compiler_params=pltpu.CompilerParams(
            dimension_semantics=("arbitrary",), vmem_limit_bytes=VMEM_LIMIT),
        name="projmix",
    )(x, mod, cos_t, sin_t, jnp.swapaxes(w_in, 1, 2), gate_w, _layer_row(gate_b),
      jnp.asarray(dmask), jnp.asarray(qdec), jnp.asarray(kdec), jnp.asarray(csum, BF16),
      _layer_row(ret_norm_w), _layer_row(gla_norm_w), w_out, w_ff1, w_ff2)


def _ffn_kernel(x_ref, mix_ref, mod_ref, wo_ref, w1_ref, w2_ref, ln1w_ref, ln1b_ref,
                ln2w_ref, ln2b_ref, out_ref, h_ref, *, alpha):
    gate1 = mod_ref[0, 2:3, :]
    shift2 = mod_ref[0, 3:4, :]
    scale2 = mod_ref[0, 4:5, :]
    gate2 = mod_ref[0, 5:6, :]
    d_ff = w1_ref.shape[1]
    n_groups = 4
    group = d_ff // n_groups
    tb = x_ref.shape[1]
    n_streams = 4
    rows_of = [slice(k * tb // n_streams, (k + 1) * tb // n_streams) for k in range(n_streams)]

    def chain(rows):
        m = _dot(mix_ref[0, rows, :], wo_ref[...])
        yield
        x1 = _ln(alpha * x_ref[0, rows, :] + gate1 * m) * ln1w_ref[...] + ln1b_ref[...]
        u2 = (_ln(x1) * (1.0 + scale2) + shift2).astype(BF16)
        yield
        for j in range(n_groups):
            hj = jnp.maximum(_dot(u2, w1_ref[:, j * group:(j + 1) * group]), 0.0)
            h_ref[rows, j * group:(j + 1) * group] = (hj * hj).astype(BF16)
            yield
        f = _dot(h_ref[rows, :], w2_ref[...])
        yield
        out_ref[0, rows, :] = _ln(alpha * x1 + gate2 * f) * ln2w_ref[...] + ln2b_ref[...]
        yield

    n_pieces = n_groups + 4
    streams = [chain(rows) for rows in rows_of]
    for tick in range(n_pieces + n_streams - 1):
        for k, stream in enumerate(streams):
            if 0 <= tick - k < n_pieces:
                next(stream)
    for stream in streams:
        assert next(stream, "done") == "done", "piece count mismatch"


def _ffn(x, mixed, mod, w_out, w1, w2, ln1_w, ln1_b, ln2_w, ln2_b, layer, alpha, tb):
    bsz, s, d = x.shape
    nb = s // tb
    d_ff = w1.shape[1]
    const2 = lambda b, i: (0, 0)
    of_layer = lambda b, i: (layer, 0, 0)
    single = dict(pipeline_mode=pl.Buffered(1))
    return pl.pallas_call(
        functools.partial(_ffn_kernel, alpha=alpha),
        grid=(bsz, nb),
        in_specs=[pl.BlockSpec((1, tb, d), lambda b, i: (b, i, 0)),
                  pl.BlockSpec((1, tb, d), lambda b, i: (b, i, 0)),
                  pl.BlockSpec((1, 6, d), lambda b, i: (b, 0, 0)),
                  pl.BlockSpec((d, d), const2, **single),
                  pl.BlockSpec((d, d_ff), const2, **single),
                  pl.BlockSpec((d_ff, d), const2, **single),
                  pl.BlockSpec((None, 1, d), of_layer, **single),
                  pl.BlockSpec((None, 1, d), of_layer, **single),
                  pl.BlockSpec((None, 1, d), of_layer, **single),
                  pl.BlockSpec((None, 1, d), of_layer, **single)],
        out_specs=pl.BlockSpec((1, tb, d), lambda b, i: (b, i, 0)),
        out_shape=jax.ShapeDtypeStruct((bsz, s, d), x.dtype),
        scratch_shapes=[pltpu.VMEM((tb, d_ff), BF16)],
        compiler_params=pltpu.CompilerParams(
            dimension_semantics=("parallel", "parallel"), vmem_limit_bytes=VMEM_LIMIT),
        name="ffn",
    )(x, mixed, mod, w_out, w1, w2, _layer_row(ln1_w), _layer_row(ln1_b), _layer_row(ln2_w),
      _layer_row(ln2_b))


def kernel(x, c, w_ada, b_ada, w_in, ret_norm_w, gla_gate_w, gla_gate_b, gla_norm_w,
           w_out, ln1_w, ln1_b, w_ff1, w_ff2, ln2_w, ln2_b):
    depth = w_in.shape[0]
    bsz, s, d = x.shape
    alpha = (2.0 * depth) ** 0.25
    tb = min(512, s)
    cos_t, sin_t = (jnp.asarray(t) for t in _rotary_tables(s))
    for l in range(depth):
        mod = _adaln(c, w_ada, b_ada, l)
        mixed, wo_bf, w1_bf, w2_bf = _projmix(x, mod, cos_t, sin_t, w_in, gla_gate_w, gla_gate_b,
                                              ret_norm_w, gla_norm_w, w_out, w_ff1, w_ff2, l, tb)
        x = _ffn(x, mixed, mod, wo_bf, w1_bf, w2_bf, ln1_w, ln1_b, ln2_w, ln2_b, l, alpha,
                 min(1024, s))
    return x
```

```python
import functools
import math

import numpy as np
import jax
import jax.numpy as jnp
from jax import lax
from jax.experimental import pallas as pl
from jax.experimental.pallas import tpu as pltpu

CHUNK = 64
RET_HEADS = 4
RET_D = 128
RET_WIDTH = RET_HEADS * RET_D
GLA_HEADS = 4
GLA_DK = 64
GLA_DV = 128
GLA_KEY_WIDTH = GLA_HEADS * GLA_DK
GLA_WIDTH = GLA_HEADS * GLA_DV
D_MIX = RET_WIDTH + GLA_WIDTH
GATE_RANK = 16
GATE_TAU = 16.0
ROPE_BASE = 10000.0
LN_EPS = 1e-5
LANES = 128
SLAB = 4 * RET_WIDTH + 2 * GLA_KEY_WIDTH + 2 * GLA_WIDTH
OFF_RQ, OFF_RK, OFF_RV, OFF_RG = 0, 512, 1024, 1536
OFF_GQ, OFF_GK, OFF_GV, OFF_GG = 2048, 2304, 2560, 3072
MIX_ROWS = 256
BOUNDED_LOG2_DECAY = 60.0
VMEM_LIMIT = 56 * 1024 * 1024
LOG2E = math.log2(math.e)

BF16 = jnp.bfloat16
F32 = jnp.float32


def _dot(a, b):
    return jnp.dot(a, b, preferred_element_type=F32)


def _dot_nt(a, b):
    return lax.dot_general(a, b, (((1,), (1,)), ((), ())), preferred_element_type=F32)


def _dot_tn(a, b):
    return lax.dot_general(a, b, (((0,), (0,)), ((), ())), preferred_element_type=F32)


def _zero_of(v):
    bits = lax.shift_right_logical(pltpu.bitcast(v, jnp.uint32), jnp.uint32(32))
    return pltpu.bitcast(bits, F32)


def _ln(x, not_before=None):
    mu = jnp.mean(x, axis=-1, keepdims=True)
    if not_before is not None:
        mu = mu + _zero_of(not_before)[0:1, 0:1]
    xc = x - mu
    var = jnp.mean(xc * xc, axis=-1, keepdims=True)
    return xc * lax.rsqrt(var + LN_EPS)


def _silu(x):
    return x * (1.0 / (1.0 + jnp.exp(-x)))


def _adaln_kernel(c_ref, w_ref, b_ref, o_ref):
    c = c_ref[...]
    val = _dot(_silu(c).astype(BF16), w_ref[...].astype(BF16)) + b_ref[...]
    o_ref[:, pl.ds(pl.program_id(0), 1), :] = val[:, None, :]


def _adaln(c, w_ada, b_ada, layer):
    bsz, d = c.shape
    depth, _, n = w_ada.shape
    return pl.pallas_call(
        _adaln_kernel,
        grid=(n // d,),
        in_specs=[pl.BlockSpec((bsz, d), lambda j: (0, 0)),
                  pl.BlockSpec((None, d, d), lambda j: (layer, 0, j)),
                  pl.BlockSpec((None, 1, d), lambda j: (layer, 0, j))],
        out_specs=pl.BlockSpec((bsz, n // d, d), lambda j: (0, 0, 0)),
        out_shape=jax.ShapeDtypeStruct((bsz, n // d, d), F32),
        compiler_params=pltpu.CompilerParams(vmem_limit_bytes=VMEM_LIMIT),
        name="adaln",
    )(c, w_ada, b_ada.reshape(depth, 1, n))


def _rotary_tables(s):
    half = RET_D // 2
    inv = 1.0 / (ROPE_BASE ** (np.arange(half, dtype=np.float64) / (half - 1)))
    ang = np.arange(s, dtype=np.float64)[:, None] * inv[None, :]
    cos = np.concatenate([np.cos(ang), np.cos(ang)], axis=1)
    sin = np.concatenate([-np.sin(ang), np.sin(ang)], axis=1)
    return cos.astype(np.float32), sin.astype(np.float32)


def _retention_tables(tb):
    log_gamma = np.log(1.0 - 2.0 ** (-5.0 - np.arange(RET_HEADS, dtype=np.float64)))
    idx = np.arange(tb)
    dist = np.abs(idx[:, None] - idx[None, :])
    visible = (idx[None, :] // CHUNK) <= (idx[:, None] // CHUNK)
    scale = RET_D ** -0.5
    dmask = np.exp(log_gamma[:, None, None] * dist[None]) * visible[None] * scale
    qdec = np.exp(log_gamma[:, None] * (idx[None, :] + 1.0)) * scale
    kdec = np.exp(log_gamma[:, None] * (tb - 1.0 - idx[None, :]))
    qdec = np.broadcast_to(qdec[:, :, None], (RET_HEADS, tb, RET_D))
    kdec = np.broadcast_to(kdec[:, :, None], (RET_HEADS, tb, RET_D))
    block_decay = [float(np.exp(lg * tb)) for lg in log_gamma]
    return (dmask.astype(np.float32), np.ascontiguousarray(qdec, np.float32),
            np.ascontiguousarray(kdec, np.float32), block_decay)


def _cumsum_matrix(tb):
    t = np.arange(tb)
    same_chunk = (t[:, None] // CHUNK) == (t[None, :] // CHUNK)
    return (same_chunk & (t[None, :] <= t[:, None])).astype(np.float32)


def _proj_stage(x_ref, mod_ref, cos_ref, sin_ref, w_ref, wtail_ref, gw_ref, gb_ref, slab, la_out,
                bounded_ref, slot):
    x = x_ref[0]
    shift1 = mod_ref[0, 0:1, :]
    scale1 = mod_ref[0, 1:2, :]
    ub = (_ln(x) * (1.0 + scale1) + shift1).astype(BF16)
    yield
    cos = cos_ref[...]
    sin = sin_ref[...]
    group = 256
    for j in range(SLAB // group):
        p = _dot(ub, w_ref[:, j * group:(j + 1) * group])
        if j * group < OFF_RV:
            for h in range(group // RET_D):
                xh = p[:, h * RET_D:(h + 1) * RET_D]
                rot = xh * cos + pltpu.roll(xh, RET_D // 2, 1) * sin
                slab[:, j * group + h * RET_D:j * group + (h + 1) * RET_D] = rot.astype(BF16)
        else:
            slab[:, j * group:(j + 1) * group] = p.astype(BF16)
        yield
    glr = _dot_nt(ub, wtail_ref[...])
    logit = _dot(glr.astype(BF16), gw_ref[...].astype(BF16)) + gb_ref[...]
    log_sig = jnp.minimum(logit, 0.0) - jnp.log1p(jnp.exp(-jnp.abs(logit)))
    la = log_sig * (LOG2E / GATE_TAU)
    la_out[...] = la
    bounded_ref[slot] = (jnp.min(la) * CHUNK >= -BOUNDED_LOG2_DECAY).astype(jnp.int32)
    yield


def _row_of_block(b, block, row):
    n = b.shape[0] // block
    b3 = b.reshape(n, block, b.shape[1])
    return jnp.broadcast_to(b3[:, row:row + 1, :], b3.shape).reshape(b.shape)


def _mix_stage(slab, la_ref, r0, dmask_ref, qdec_ref, kdec_ref, csum_ref, rnw_ref, gnw_ref,
               out_ref, r_state, g_state, block_decay, bounded):
    tb = MIX_ROWS
    blk = slice(r0, r0 + tb)
    for h in range(RET_HEADS):
        q = slab[blk, OFF_RQ + h * RET_D:OFF_RQ + (h + 1) * RET_D]
        k = slab[blk, OFF_RK + h * RET_D:OFF_RK + (h + 1) * RET_D]
        v = slab[blk, OFF_RV + h * RET_D:OFF_RV + (h + 1) * RET_D]
        g = slab[blk, OFF_RG + h * RET_D:OFF_RG + (h + 1) * RET_D].astype(F32)
        p = (_dot_nt(q, k) * dmask_ref[h]).astype(BF16)
        r_prev = r_state[h]
        o = _dot(p, v) + _dot(q, r_prev.astype(BF16)) * qdec_ref[h]
        kd = (k.astype(F32) * kdec_ref[h]).astype(BF16)
        r_state[h] = block_decay[h] * r_prev + _dot_tn(kd, v)
        o = o - jnp.mean(o, axis=-1, keepdims=True)
        o = o * lax.rsqrt(jnp.mean(o * o, axis=-1, keepdims=True) + LN_EPS)
        o = o * rnw_ref[:, h * RET_D:(h + 1) * RET_D] * _silu(g)
        out_ref[0, blk, h * RET_D:(h + 1) * RET_D] = o.astype(out_ref.dtype)
        yield

    la = la_ref[blk, :]
    la_hi = la.astype(BF16)
    la_lo = (la - la_hi.astype(F32)).astype(BF16)
    csum = csum_ref[...]
    b = _dot(csum, la_hi) + _dot(csum, la_lo)
    gq = slab[blk, OFF_GQ:OFF_GQ + GLA_KEY_WIDTH] * jnp.asarray(GLA_DK ** -0.5, BF16)
    gk = slab[blk, OFF_GK:OFF_GK + GLA_KEY_WIDTH]
    q_state = gq * jnp.exp2(b).astype(BF16)
    k_state = gk * jnp.exp2(_row_of_block(b, CHUNK, CHUNK - 1) - b).astype(BF16)
    if bounded:
        grow = jnp.exp2(-b).astype(BF16)
        q_lvl = [q_state, gq * grow]
        phis = [grow, jnp.exp2(b).astype(BF16)]
    else:
        row = lax.broadcasted_iota(jnp.int32, (tb, GLA_KEY_WIDTH), 0)
        psis = [-jnp.abs(b - _row_of_block(b, 2 * c, c - 1)) for c in (32, 16, 8, 4)]
        la_next = pltpu.roll(la, tb - 1, 0)
        la_prev = pltpu.roll(la, 1, 0)
        r4 = row & 3
        psis.append(jnp.where(r4 == 0, la_next,
                              jnp.where(r4 == 1, 0.0, jnp.where(r4 == 2, la, la + la_prev))))
        psis.append(jnp.where((row & 1) == 1, la, 0.0))
        phis = [jnp.exp2(p).astype(BF16) for p in psis]
        q_lvl = [gq] + [gq * phi for phi in phis]
        phis = [None] + phis
    yield

    lane = lax.broadcasted_iota(jnp.int32, (CHUNK, LANES), 1)
    low_half = lane < GLA_DK
    zero_k = jnp.zeros((CHUNK, LANES), BF16)
    zero_s = jnp.zeros((GLA_DV, LANES), BF16)
    t_idx = lax.broadcasted_iota(jnp.int32, (CHUNK, GLA_HEADS * CHUNK), 0)
    s_idx = lax.broadcasted_iota(jnp.int32, (CHUNK, GLA_HEADS * CHUNK), 1) & (CHUNK - 1)
    t_xor_s = t_idx ^ s_idx
    lane_s = lax.broadcasted_iota(jnp.int32, (GLA_DV, LANES), 1)
    low_half_s = lane_s < GLA_DK

    def lane_tiles(piece, tile, zero, n_tiles):
        return jnp.concatenate([piece if j == tile else zero for j in range(n_tiles)], axis=1)

    for ch in range(tb // CHUNK):
        rows = slice(ch * CHUNK, (ch + 1) * CHUNK)
        orow = slice(r0 + ch * CHUNK, r0 + (ch + 1) * CHUNK)
        k_heads = [jnp.where(low_half if h % 2 == 0 else ~low_half,
                             gk[rows, (h // 2) * LANES:(h // 2 + 1) * LANES], zero_k)
                   for h in range(GLA_HEADS)]
        a_lvl = []
        for lvl in range(len(q_lvl)):
            pieces = []
            for h in range(GLA_HEADS):
                tile = h // 2
                kh = k_heads[h]
                if phis[lvl] is not None:
                    kh = kh * phis[lvl][rows, tile * LANES:(tile + 1) * LANES]
                pieces.append(lane_tiles(kh, tile, zero_k, 2))
            k_stack = jnp.concatenate(pieces, axis=0)
            a_lvl.append(_dot_nt(q_lvl[lvl][rows], k_stack))
        if bounded:
            a = jnp.where(t_idx >= s_idx, a_lvl[0], a_lvl[1])
        else:
            a = a_lvl[0]
            for i, c in enumerate((1, 2, 4, 8, 16, 32)):
                a = jnp.where(t_xor_s >= c, a_lvl[len(a_lvl) - 1 - i], a)
        a = a.astype(BF16)
        gv = slab[orow, OFF_GV:OFF_GV + GLA_WIDTH]
        v_bd = jnp.concatenate(
            [lane_tiles(gv[:, h * GLA_DV:(h + 1) * GLA_DV], h, jnp.zeros((CHUNK, GLA_DV), BF16), GLA_HEADS)
             for h in range(GLA_HEADS)], axis=0)
        s_bd = jnp.concatenate(
            [lane_tiles(g_state[h].astype(BF16), h // 2, zero_s, 2) for h in range(GLA_HEADS)],
            axis=0)
        o_all = _dot(a, v_bd) + _dot_nt(q_state[rows], s_bd)
        decay = jnp.exp2(b[ch * CHUNK + CHUNK - 1:(ch + 1) * CHUNK, :])
        for h in range(GLA_HEADS):
            tile = h // 2
            upd = _dot_tn(gv[:, h * GLA_DV:(h + 1) * GLA_DV], k_state[rows, tile * LANES:(tile + 1) * LANES])
            upd = jnp.where(low_half_s if h % 2 == 0 else ~low_half_s, upd, 0.0)
            g_state[h] = g_state[h] * decay[:, tile * LANES:(tile + 1) * LANES] + upd
        for h in range(GLA_HEADS):
            o = o_all[:, h * GLA_DV:(h + 1) * GLA_DV]
            o = o * lax.rsqrt(jnp.mean(o * o, axis=-1, keepdims=True) + LN_EPS)
            gg = slab[orow, OFF_GG + h * GLA_DV:OFF_GG + (h + 1) * GLA_DV].astype(F32)
            o = o * gnw_ref[:, h * GLA_DV:(h + 1) * GLA_DV] * _silu(gg)
            out_ref[0, orow, RET_WIDTH + h * GLA_DV:RET_WIDTH + (h + 1) * GLA_DV] = o.astype(out_ref.dtype)
        yield


def _interleave(gen_a, n_a, gen_b, n_b):
    done_a = done_b = 0
    while done_a < n_a or done_b < n_b:
        if done_b >= n_b or (done_a < n_a and (done_a + 0.5) * n_b <= (done_b + 0.5) * n_a):
            next(gen_a)
            done_a += 1
        else:
            next(gen_b)
            done_b += 1
    for g in (gen_a, gen_b):
        assert next(g, "done") == "done", "piece count mismatch"


def _projmix_kernel(x_ref, mod_ref, cos_ref, sin_ref, w_ref, gw_ref, gb_ref,
                    dmask_ref, qdec_ref, kdec_ref, csum_ref, rnw_ref, gnw_ref,
                    wo_ref, w1_ref, w2_ref,
                    out_ref, wo_bf_ref, w1_bf_ref, w2_bf_ref,
                    w_bf, w_tail, slab2, la2, bounded_ref, r_state, g_state,
                    *, tb, blocks_per_seq, n_blocks, block_decay):
    j = pl.program_id(0)
    n_proj = 2 + SLAB // 256
    n_mix = (tb // MIX_ROWS) * (RET_HEADS + 1 + MIX_ROWS // CHUNK)

    @pl.when(j == 0)
    def _():
        for g in range(0, SLAB, 256):
            w_bf[:, g:g + 256] = w_ref[g:g + 256, :].T.astype(BF16)
        w_tail[...] = w_ref[SLAB:SLAB + GATE_RANK, :].astype(BF16)
        bounded_ref[1] = 0

    wo_bf_ref[...] = wo_ref[...].astype(BF16)
    w1_bf_ref[...] = w1_ref[...].astype(BF16)
    w2_bf_ref[...] = w2_ref[...].astype(BF16)

    @pl.when((j - 1) % blocks_per_seq == 0)
    def _():
        r_state[...] = jnp.zeros_like(r_state)
        g_state[...] = jnp.zeros_like(g_state)

    def project(buf):
        return _proj_stage(x_ref, mod_ref, cos_ref, sin_ref, w_bf, w_tail, gw_ref, gb_ref,
                           slab2.at[buf], la2.at[buf], bounded_ref, buf)

    def mix(buf, bounded):
        for r0 in range(0, tb, MIX_ROWS):
            yield from _mix_stage(slab2.at[buf], la2.at[buf], r0, dmask_ref, qdec_ref, kdec_ref,
                                  csum_ref, rnw_ref, gnw_ref, out_ref, r_state, g_state, block_decay,
                                  bounded)

    @pl.when(j == 0)
    def _():
        for _ in project(0):
            pass

    steady = (j > 0) & (j < n_blocks)
    bounded = bounded_ref[1 - j % 2] == 1
    for parity in (0, 1):
        @pl.when(steady & bounded & (j % 2 == parity))
        def _():
            _interleave(project(parity), n_proj, mix(1 - parity, True), n_mix)

    @pl.when(steady & jnp.logical_not(bounded))
    def _():
        _interleave(project(j % 2), n_proj, mix(1 - j % 2, False), n_mix)

    @pl.when(j == n_blocks)
    def _():
        for _ in mix((n_blocks - 1) % 2, False):
            pass


def _layer_row(v):
    return v.reshape(v.shape[0], 1, v.shape[1])


def _projmix(x, mod, cos_t, sin_t, w_in, gate_w, gate_b, ret_norm_w, gla_norm_w,
             w_out, w_ff1, w_ff2, layer, tb):
    bsz, s, d = x.shape
    d_ff = w_ff1.shape[2]
    bps = s // tb
    nblk = bsz * bps
    dmask, qdec, kdec, block_decay = _retention_tables(MIX_ROWS)
    csum = _cumsum_matrix(MIX_ROWS)
    const2 = lambda j: (0, 0)
    const3 = lambda j: (0, 0, 0)
    of_layer = lambda j: (layer, 0, 0)
    proj_blk = lambda j: jnp.minimum(j, nblk - 1)
    mix_blk = lambda j: jnp.maximum(j - 1, 0)
    single = dict(pipeline_mode=pl.Buffered(1))
    wo_rows, w1_rows, w2_rows = d // nblk, d // nblk, d_ff // nblk
    return pl.pallas_call(
        functools.partial(_projmix_kernel, tb=tb, blocks_per_seq=bps, n_blocks=nblk,
                          block_decay=block_decay),
        grid=(nblk + 1,),
        in_specs=[pl.BlockSpec((1, tb, d), lambda j: (proj_blk(j) // bps, proj_blk(j) % bps, 0)),
                  pl.BlockSpec((1, 6, d), lambda j: (proj_blk(j) // bps, 0, 0)),
                  pl.BlockSpec((tb, RET_D), lambda j: (proj_blk(j) % bps, 0)),
                  pl.BlockSpec((tb, RET_D), lambda j: (proj_blk(j) % bps, 0)),
                  pl.BlockSpec((None, SLAB + GATE_RANK, d), of_layer, **single),
                  pl.BlockSpec((None, GATE_RANK, GLA_KEY_WIDTH), of_layer, **single),
                  pl.BlockSpec((None, 1, GLA_KEY_WIDTH), of_layer, **single),
                  pl.BlockSpec(dmask.shape, const3, **single),
                  pl.BlockSpec(qdec.shape, const3, **single),
                  pl.BlockSpec(kdec.shape, const3, **single),
                  pl.BlockSpec(csum.shape, const2, **single),
                  pl.BlockSpec((None, 1, RET_WIDTH), of_layer, **single),
                  pl.BlockSpec((None, 1, GLA_WIDTH), of_layer, **single),
                  pl.BlockSpec((None, wo_rows, d), lambda j: (layer, proj_blk(j), 0)),
                  pl.BlockSpec((None, w1_rows, d_ff), lambda j: (layer, proj_blk(j), 0)),
                  pl.BlockSpec((None, w2_rows, d), lambda j: (layer, proj_blk(j), 0))],
        out_specs=[pl.BlockSpec((1, tb, D_MIX), lambda j: (mix_blk(j) // bps, mix_blk(j) % bps, 0)),
                   pl.BlockSpec((wo_rows, d), lambda j: (proj_blk(j), 0)),
                   pl.BlockSpec((w1_rows, d_ff), lambda j: (proj_blk(j), 0)),
                   pl.BlockSpec((w2_rows, d), lambda j: (proj_blk(j), 0))],
        out_shape=[jax.ShapeDtypeStruct((bsz, s, D_MIX), BF16),
                   jax.ShapeDtypeStruct((d, d), BF16),
                   jax.ShapeDtypeStruct((d, d_ff), BF16),
                   jax.ShapeDtypeStruct((d_ff, d), BF16)],
        scratch_shapes=[pltpu.VMEM((d, SLAB), BF16), pltpu.VMEM((GATE_RANK, d), BF16),
                        pltpu.VMEM((2, tb, SLAB), BF16), pltpu.VMEM((2, tb, GLA_KEY_WIDTH), F32),
                        pltpu.SMEM((2,), jnp.int32),
                        pltpu.VMEM((RET_HEADS, RET_D, RET_D), F32),
                        pltpu.VMEM((GLA_HEADS, GLA_DV, LANES), F32)],
        compiler_params=pltpu.CompilerParams(
            dimension_semantics=("arbitrary",), vmem_limit_bytes=VMEM_LIMIT),
        name="projmix",
    )(x, mod, cos_t, sin_t, jnp.swapaxes(w_in, 1, 2), gate_w, _layer_row(gate_b),
      jnp.asarray(dmask), jnp.asarray(qdec), jnp.asarray(kdec), jnp.asarray(csum, BF16),
      _layer_row(ret_norm_w), _layer_row(gla_norm_w), w_out, w_ff1, w_ff2)


def _ffn_kernel(x_ref, mix_ref, mod_ref, wo_ref, w1_ref, w2_ref, ln1w_ref, ln1b_ref,
                ln2w_ref, ln2b_ref, out_ref, h_ref, *, alpha):
    gate1 = mod_ref[0, 2:3, :]
    shift2 = mod_ref[0, 3:4, :]
    scale2 = mod_ref[0, 4:5, :]
    gate2 = mod_ref[0, 5:6, :]
    d_ff = w1_ref.shape[1]
    n_groups = 4
    group = d_ff // n_groups
    tb = x_ref.shape[1]
    halves = (slice(0, tb // 2), slice(tb // 2, tb))

    def chain(rows, wait):
        m = _dot(mix_ref[0, rows, :], wo_ref[...])
        yield
        x1 = (_ln(alpha * x_ref[0, rows, :] + gate1 * m, wait.get("ln1")) * ln1w_ref[...]
              + ln1b_ref[...])
        u2f = _ln(x1) * (1.0 + scale2) + shift2
        wait["ln1_done"] = u2f[u2f.shape[0] - 8:, u2f.shape[1] - LANES:]
        u2 = u2f.astype(BF16)
        yield
        for j in range(n_groups):
            hj = jnp.maximum(_dot(u2, w1_ref[:, j * group:(j + 1) * group]), 0.0)
            h_ref[rows, j * group:(j + 1) * group] = (hj * hj).astype(BF16)
            yield
        f = _dot(h_ref[rows, :], w2_ref[...])
        yield
        out_ref[0, rows, :] = _ln(alpha * x1 + gate2 * f) * ln2w_ref[...] + ln2b_ref[...]
        yield

    wait_a, wait_b = {}, {}
    half_a, half_b = chain(halves[0], wait_a), chain(halves[1], wait_b)
    next(half_a)
    next(half_b)
    next(half_a)
    wait_b["ln1"] = wait_a["ln1_done"]
    n_rest = n_groups + 2
    _interleave(half_a, n_rest, half_b, n_rest + 1)


def _ffn(x, mixed, mod, w_out, w1, w2, ln1_w, ln1_b, ln2_w, ln2_b, layer, alpha, tb):
    bsz, s, d = x.shape
    nb = s // tb
    d_ff = w1.shape[1]
    const2 = lambda b, i: (0, 0)
    of_layer = lambda b, i: (layer, 0, 0)
    single = dict(pipeline_mode=pl.Buffered(1))
    return pl.pallas_call(
        functools.partial(_ffn_kernel, alpha=alpha),
        grid=(bsz, nb),
        in_specs=[pl.BlockSpec((1, tb, d), lambda b, i: (b, i, 0)),
                  pl.BlockSpec((1, tb, d), lambda b, i: (b, i, 0)),
                  pl.BlockSpec((1, 6, d), lambda b, i: (b, 0, 0)),
                  pl.BlockSpec((d, d), const2, **single),
                  pl.BlockSpec((d, d_ff), const2, **single),
                  pl.BlockSpec((d_ff, d), const2, **single),
                  pl.BlockSpec((None, 1, d), of_layer, **single),
                  pl.BlockSpec((None, 1, d), of_layer, **single),
                  pl.BlockSpec((None, 1, d), of_layer, **single),
                  pl.BlockSpec((None, 1, d), of_layer, **single)],
        out_specs=pl.BlockSpec((1, tb, d), lambda b, i: (b, i, 0)),
        out_shape=jax.ShapeDtypeStruct((bsz, s, d), x.dtype),
        scratch_shapes=[pltpu.VMEM((tb, d_ff), BF16)],
        compiler_params=pltpu.CompilerParams(
            dimension_semantics=("parallel", "parallel"), vmem_limit_bytes=VMEM_LIMIT),
        name="ffn",
    )(x, mixed, mod, w_out, w1, w2, _layer_row(ln1_w), _layer_row(ln1_b), _layer_row(ln2_w),
      _layer_row(ln2_b))


def kernel(x, c, w_ada, b_ada, w_in, ret_norm_w, gla_gate_w, gla_gate_b, gla_norm_w,
           w_out, ln1_w, ln1_b, w_ff1, w_ff2, ln2_w, ln2_b):
    depth = w_in.shape[0]
    bsz, s, d = x.shape
    alpha = (2.0 * depth) ** 0.25
    tb = min(512, s)
    cos_t, sin_t = (jnp.asarray(t) for t in _rotary_tables(s))
    for l in range(depth):
        mod = _adaln(c, w_ada, b_ada, l)
        mixed, wo_bf, w1_bf, w2_bf = _projmix(x, mod, cos_t, sin_t, w_in, gla_gate_w, gla_gate_b,
                                              ret_norm_w, gla_norm_w, w_out, w_ff1, w_ff2, l, tb)
        x = _ffn(x, mixed, mod, wo_bf, w1_bf, w2_bf, ln1_w, ln1_b, ln2_w, ln2_b, l, alpha,
                 min(1024, s))
    return x
```

```python
import functools
import math

import numpy as np
import jax
import jax.numpy as jnp
from jax import lax
from jax.experimental import pallas as pl
from jax.experimental.pallas import tpu as pltpu

CHUNK = 64
RET_HEADS = 4
RET_D = 128
RET_WIDTH = RET_HEADS * RET_D
GLA_HEADS = 4
GLA_DK = 64
GLA_DV = 128
GLA_KEY_WIDTH = GLA_HEADS * GLA_DK
GLA_WIDTH = GLA_HEADS * GLA_DV
D_MIX = RET_WIDTH + GLA_WIDTH
GATE_RANK = 16
GATE_TAU = 16.0
ROPE_BASE = 10000.0
LN_EPS = 1e-5
LANES = 128
SLAB = 4 * RET_WIDTH + 2 * GLA_KEY_WIDTH + 2 * GLA_WIDTH
OFF_RQ, OFF_RK, OFF_RV, OFF_RG = 0, 512, 1024, 1536
OFF_GQ, OFF_GK, OFF_GV, OFF_GG = 2048, 2304, 2560, 3072
MIX_ROWS = 256
BOUNDED_LOG2_DECAY = 60.0
VMEM_LIMIT = 56 * 1024 * 1024
LOG2E = math.log2(math.e)

BF16 = jnp.bfloat16
F32 = jnp.float32


def _dot(a, b):
    return jnp.dot(a, b, preferred_element_type=F32)


def _dot_nt(a, b):
    return lax.dot_general(a, b, (((1,), (1,)), ((), ())), preferred_element_type=F32)


def _dot_tn(a, b):
    return lax.dot_general(a, b, (((0,), (0,)), ((), ())), preferred_element_type=F32)


def _zero_of(v):
    bits = lax.shift_right_logical(pltpu.bitcast(v, jnp.uint32), jnp.uint32(32))
    return pltpu.bitcast(bits, F32)


def _ln(x, not_before=None):
    mu = jnp.mean(x, axis=-1, keepdims=True)
    if not_before is not None:
        mu = mu + _zero_of(not_before)[0:1, 0:1]
    xc = x - mu
    var = jnp.mean(xc * xc, axis=-1, keepdims=True)
    return xc * lax.rsqrt(var + LN_EPS)


def _silu(x):
    return x * (1.0 / (1.0 + jnp.exp(-x)))


def _adaln_kernel(c_ref, w_ref, b_ref, o_ref):
    c = c_ref[...]
    val = _dot(_silu(c).astype(BF16), w_ref[...].astype(BF16)) + b_ref[...]
    o_ref[:, pl.ds(pl.program_id(0), 1), :] = val[:, None, :]


def _adaln(c, w_ada, b_ada, layer):
    bsz, d = c.shape
    depth, _, n = w_ada.shape
    return pl.pallas_call(
        _adaln_kernel,
        grid=(n // d,),
        in_specs=[pl.BlockSpec((bsz, d), lambda j: (0, 0)),
                  pl.BlockSpec((None, d, d), lambda j: (layer, 0, j)),
                  pl.BlockSpec((None, 1, d), lambda j: (layer, 0, j))],
        out_specs=pl.BlockSpec((bsz, n // d, d), lambda j: (0, 0, 0)),
        out_shape=jax.ShapeDtypeStruct((bsz, n // d, d), F32),
        compiler_params=pltpu.CompilerParams(vmem_limit_bytes=VMEM_LIMIT),
        name="adaln",
    )(c, w_ada, b_ada.reshape(depth, 1, n))


def _rotary_tables(s):
    half = RET_D // 2
    inv = 1.0 / (ROPE_BASE ** (np.arange(half, dtype=np.float64) / (half - 1)))
    ang = np.arange(s, dtype=np.float64)[:, None] * inv[None, :]
    cos = np.concatenate([np.cos(ang), np.cos(ang)], axis=1)
    sin = np.concatenate([-np.sin(ang), np.sin(ang)], axis=1)
    return cos.astype(np.float32), sin.astype(np.float32)


def _retention_tables(tb):
    log_gamma = np.log(1.0 - 2.0 ** (-5.0 - np.arange(RET_HEADS, dtype=np.float64)))
    idx = np.arange(tb)
    dist = np.abs(idx[:, None] - idx[None, :])
    visible = (idx[None, :] // CHUNK) <= (idx[:, None] // CHUNK)
    scale = RET_D ** -0.5
    dmask = np.exp(log_gamma[:, None, None] * dist[None]) * visible[None] * scale
    qdec = np.exp(log_gamma[:, None] * (idx[None, :] + 1.0)) * scale
    kdec = np.exp(log_gamma[:, None] * (tb - 1.0 - idx[None, :]))
    qdec = np.broadcast_to(qdec[:, :, None], (RET_HEADS, tb, RET_D))
    kdec = np.broadcast_to(kdec[:, :, None], (RET_HEADS, tb, RET_D))
    block_decay = [float(np.exp(lg * tb)) for lg in log_gamma]
    return (dmask.astype(np.float32), np.ascontiguousarray(qdec, np.float32),
            np.ascontiguousarray(kdec, np.float32), block_decay)


def _cumsum_matrix(tb):
    t = np.arange(tb)
    same_chunk = (t[:, None] // CHUNK) == (t[None, :] // CHUNK)
    return (same_chunk & (t[None, :] <= t[:, None])).astype(np.float32)


def _proj_stage(x_ref, mod_ref, cos_ref, sin_ref, w_ref, wtail_ref, gw_ref, gb_ref, slab, la_out,
                bounded_ref, slot):
    x = x_ref[0]
    shift1 = mod_ref[0, 0:1, :]
    scale1 = mod_ref[0, 1:2, :]
    ub = (_ln(x) * (1.0 + scale1) + shift1).astype(BF16)
    yield
    cos = cos_ref[...]
    sin = sin_ref[...]
    group = 256
    for j in range(SLAB // group):
        p = _dot(ub, w_ref[:, j * group:(j + 1) * group])
        if j * group < OFF_RV:
            for h in range(group // RET_D):
                xh = p[:, h * RET_D:(h + 1) * RET_D]
                rot = xh * cos + pltpu.roll(xh, RET_D // 2, 1) * sin
                slab[:, j * group + h * RET_D:j * group + (h + 1) * RET_D] = rot.astype(BF16)
        else:
            slab[:, j * group:(j + 1) * group] = p.astype(BF16)
        yield
    glr = _dot_nt(ub, wtail_ref[...])
    logit = _dot(glr.astype(BF16), gw_ref[...].astype(BF16)) + gb_ref[...]
    log_sig = jnp.minimum(logit, 0.0) - jnp.log1p(jnp.exp(-jnp.abs(logit)))
    la = log_sig * (LOG2E / GATE_TAU)
    la_out[...] = la
    bounded_ref[slot] = (jnp.min(la) * CHUNK >= -BOUNDED_LOG2_DECAY).astype(jnp.int32)
    yield


def _row_of_block(b, block, row):
    n = b.shape[0] // block
    b3 = b.reshape(n, block, b.shape[1])
    return jnp.broadcast_to(b3[:, row:row + 1, :], b3.shape).reshape(b.shape)


def _mix_stage(slab, la_ref, r0, dmask_ref, qdec_ref, kdec_ref, csum_ref, rnw_ref, gnw_ref,
               out_ref, r_state, g_state, block_decay, bounded):
    tb = MIX_ROWS
    blk = slice(r0, r0 + tb)
    for h in range(RET_HEADS):
        q = slab[blk, OFF_RQ + h * RET_D:OFF_RQ + (h + 1) * RET_D]
        k = slab[blk, OFF_RK + h * RET_D:OFF_RK + (h + 1) * RET_D]
        v = slab[blk, OFF_RV + h * RET_D:OFF_RV + (h + 1) * RET_D]
        g = slab[blk, OFF_RG + h * RET_D:OFF_RG + (h + 1) * RET_D].astype(F32)
        p = (_dot_nt(q, k) * dmask_ref[h]).astype(BF16)
        r_prev = r_state[h]
        o = _dot(p, v) + _dot(q, r_prev.astype(BF16)) * qdec_ref[h]
        kd = (k.astype(F32) * kdec_ref[h]).astype(BF16)
        r_state[h] = block_decay[h] * r_prev + _dot_tn(kd, v)
        o = o - jnp.mean(o, axis=-1, keepdims=True)
        o = o * lax.rsqrt(jnp.mean(o * o, axis=-1, keepdims=True) + LN_EPS)
        o = o * rnw_ref[:, h * RET_D:(h + 1) * RET_D] * _silu(g)
        out_ref[0, blk, h * RET_D:(h + 1) * RET_D] = o.astype(out_ref.dtype)
        yield

    la = la_ref[blk, :]
    la_hi = la.astype(BF16)
    la_lo = (la - la_hi.astype(F32)).astype(BF16)
    csum = csum_ref[...]
    b = _dot(csum, la_hi) + _dot(csum, la_lo)
    gq = slab[blk, OFF_GQ:OFF_GQ + GLA_KEY_WIDTH] * jnp.asarray(GLA_DK ** -0.5, BF16)
    gk = slab[blk, OFF_GK:OFF_GK + GLA_KEY_WIDTH]
    q_state = gq * jnp.exp2(b).astype(BF16)
    k_state = gk * jnp.exp2(_row_of_block(b, CHUNK, CHUNK - 1) - b).astype(BF16)
    if bounded:
        grow = jnp.exp2(-b).astype(BF16)
        q_lvl = [q_state, gq * grow]
        phis = [grow, jnp.exp2(b).astype(BF16)]
    else:
        row = lax.broadcasted_iota(jnp.int32, (tb, GLA_KEY_WIDTH), 0)
        psis = [-jnp.abs(b - _row_of_block(b, 2 * c, c - 1)) for c in (32, 16, 8, 4)]
        la_next = pltpu.roll(la, tb - 1, 0)
        la_prev = pltpu.roll(la, 1, 0)
        r4 = row & 3
        psis.append(jnp.where(r4 == 0, la_next,
                              jnp.where(r4 == 1, 0.0, jnp.where(r4 == 2, la, la + la_prev))))
        psis.append(jnp.where((row & 1) == 1, la, 0.0))
        phis = [jnp.exp2(p).astype(BF16) for p in psis]
        q_lvl = [gq] + [gq * phi for phi in phis]
        phis = [None] + phis
    yield

    lane = lax.broadcasted_iota(jnp.int32, (CHUNK, LANES), 1)
    low_half = lane < GLA_DK
    zero_k = jnp.zeros((CHUNK, LANES), BF16)
    zero_s = jnp.zeros((GLA_DV, LANES), BF16)
    t_idx = lax.broadcasted_iota(jnp.int32, (CHUNK, GLA_HEADS * CHUNK), 0)
    s_idx = lax.broadcasted_iota(jnp.int32, (CHUNK, GLA_HEADS * CHUNK), 1) & (CHUNK - 1)
    t_xor_s = t_idx ^ s_idx
    lane_s = lax.broadcasted_iota(jnp.int32, (GLA_DV, LANES), 1)
    low_half_s = lane_s < GLA_DK

    def lane_tiles(piece, tile, zero, n_tiles):
        return jnp.concatenate([piece if j == tile else zero for j in range(n_tiles)], axis=1)

    for ch in range(tb // CHUNK):
        rows = slice(ch * CHUNK, (ch + 1) * CHUNK)
        orow = slice(r0 + ch * CHUNK, r0 + (ch + 1) * CHUNK)
        k_heads = [jnp.where(low_half if h % 2 == 0 else ~low_half,
                             gk[rows, (h // 2) * LANES:(h // 2 + 1) * LANES], zero_k)
                   for h in range(GLA_HEADS)]
        a_lvl = []
        for lvl in range(len(q_lvl)):
            pieces = []
            for h in range(GLA_HEADS):
                tile = h // 2
                kh = k_heads[h]
                if phis[lvl] is not None:
                    kh = kh * phis[lvl][rows, tile * LANES:(tile + 1) * LANES]
                pieces.append(lane_tiles(kh, tile, zero_k, 2))
            k_stack = jnp.concatenate(pieces, axis=0)
            a_lvl.append(_dot_nt(q_lvl[lvl][rows], k_stack))
        if bounded:
            a = jnp.where(t_idx >= s_idx, a_lvl[0], a_lvl[1])
        else:
            a = a_lvl[0]
            for i, c in enumerate((1, 2, 4, 8, 16, 32)):
                a = jnp.where(t_xor_s >= c, a_lvl[len(a_lvl) - 1 - i], a)
        a = a.astype(BF16)
        gv = slab[orow, OFF_GV:OFF_GV + GLA_WIDTH]
        v_bd = jnp.concatenate(
            [lane_tiles(gv[:, h * GLA_DV:(h + 1) * GLA_DV], h, jnp.zeros((CHUNK, GLA_DV), BF16), GLA_HEADS)
             for h in range(GLA_HEADS)], axis=0)
        s_bd = jnp.concatenate(
            [lane_tiles(g_state[h].astype(BF16), h // 2, zero_s, 2) for h in range(GLA_HEADS)],
            axis=0)
        o_all = _dot(a, v_bd) + _dot_nt(q_state[rows], s_bd)
        decay = jnp.exp2(b[ch * CHUNK + CHUNK - 1:(ch + 1) * CHUNK, :])
        for h in range(GLA_HEADS):
            tile = h // 2
            upd = _dot_tn(gv[:, h * GLA_DV:(h + 1) * GLA_DV], k_state[rows, tile * LANES:(tile + 1) * LANES])
            upd = jnp.where(low_half_s if h % 2 == 0 else ~low_half_s, upd, 0.0)
            g_state[h] = g_state[h] * decay[:, tile * LANES:(tile + 1) * LANES] + upd
        for h in range(GLA_HEADS):
            o = o_all[:, h * GLA_DV:(h + 1) * GLA_DV]
            o = o * lax.rsqrt(jnp.mean(o * o, axis=-1, keepdims=True) + LN_EPS)
            gg = slab[orow, OFF_GG + h * GLA_DV:OFF_GG + (h + 1) * GLA_DV].astype(F32)
            o = o * gnw_ref[:, h * GLA_DV:(h + 1) * GLA_DV] * _silu(gg)
            out_ref[0, orow, RET_WIDTH + h * GLA_DV:RET_WIDTH + (h + 1) * GLA_DV] = o.astype(out_ref.dtype)
        yield


def _interleave(gen_a, n_a, gen_b, n_b):
    done_a = done_b = 0
    while done_a < n_a or done_b < n_b:
        if done_b >= n_b or (done_a < n_a and (done_a + 0.5) * n_b <= (done_b + 0.5) * n_a):
            next(gen_a)
            done_a += 1
        else:
            next(gen_b)
            done_b += 1
    for g in (gen_a, gen_b):
        assert next(g, "done") == "done", "piece count mismatch"


def _projmix_kernel(x_ref, mod_ref, cos_ref, sin_ref, w_ref, gw_ref, gb_ref,
                    dmask_ref, qdec_ref, kdec_ref, csum_ref, rnw_ref, gnw_ref,
                    wo_ref, w1_ref, w2_ref,
                    out_ref, wo_bf_ref, w1_bf_ref, w2_bf_ref,
                    w_bf, w_tail, slab2, la2, bounded_ref, r_state, g_state,
                    *, tb, blocks_per_seq, n_blocks, block_decay):
    j = pl.program_id(0)
    n_proj = 2 + SLAB // 256
    n_mix = (tb // MIX_ROWS) * (RET_HEADS + 1 + MIX_ROWS // CHUNK)

    @pl.when(j == 0)
    def _():
        for g in range(0, SLAB, 256):
            w_bf[:, g:g + 256] = w_ref[g:g + 256, :].T.astype(BF16)
        w_tail[...] = w_ref[SLAB:SLAB + GATE_RANK, :].astype(BF16)
        bounded_ref[1] = 0

    wo_bf_ref[...] = wo_ref[...].astype(BF16)
    w1_bf_ref[...] = w1_ref[...].astype(BF16)
    w2_bf_ref[...] = w2_ref[...].astype(BF16)

    @pl.when((j - 1) % blocks_per_seq == 0)
    def _():
        r_state[...] = jnp.zeros_like(r_state)
        g_state[...] = jnp.zeros_like(g_state)

    def project(buf):
        return _proj_stage(x_ref, mod_ref, cos_ref, sin_ref, w_bf, w_tail, gw_ref, gb_ref,
                           slab2.at[buf], la2.at[buf], bounded_ref, buf)

    def mix(buf, bounded):
        for r0 in range(0, tb, MIX_ROWS):
            yield from _mix_stage(slab2.at[buf], la2.at[buf], r0, dmask_ref, qdec_ref, kdec_ref,
                                  csum_ref, rnw_ref, gnw_ref, out_ref, r_state, g_state, block_decay,
                                  bounded)

    @pl.when(j == 0)
    def _():
        for _ in project(0):
            pass

    steady = (j > 0) & (j < n_blocks)
    bounded = bounded_ref[1 - j % 2] == 1
    for parity in (0, 1):
        @pl.when(steady & bounded & (j % 2 == parity))
        def _():
            _interleave(project(parity), n_proj, mix(1 - parity, True), n_mix)

    @pl.when(steady & jnp.logical_not(bounded))
    def _():
        for _ in project(j % 2):
            pass

    @pl.when(j == n_blocks)
    def _():
        for _ in mix((n_blocks - 1) % 2, False):
            pass


def _layer_row(v):
    return v.reshape(v.shape[0], 1, v.shape[1])


def _projmix(x, mod, cos_t, sin_t, w_in, gate_w, gate_b, ret_norm_w, gla_norm_w,
             w_out, w_ff1, w_ff2, layer, tb):
    bsz, s, d = x.shape
    d_ff = w_ff1.shape[2]
    bps = s // tb
    nblk = bsz * bps
    dmask, qdec, kdec, block_decay = _retention_tables(MIX_ROWS)
    csum = _cumsum_matrix(MIX_ROWS)
    const2 = lambda j: (0, 0)
    const3 = lambda j: (0, 0, 0)
    of_layer = lambda j: (layer, 0, 0)
    proj_blk = lambda j: jnp.minimum(j, nblk - 1)
    mix_blk = lambda j: jnp.maximum(j - 1, 0)
    single = dict(pipeline_mode=pl.Buffered(1))
    wo_rows, w1_rows, w2_rows = d // nblk, d // nblk, d_ff // nblk
    return pl.pallas_call(
        functools.partial(_projmix_kernel, tb=tb, blocks_per_seq=bps, n_blocks=nblk,
                          block_decay=block_decay),
        grid=(nblk + 1,),
        in_specs=[pl.BlockSpec((1, tb, d), lambda j: (proj_blk(j) // bps, proj_blk(j) % bps, 0)),
                  pl.BlockSpec((1, 6, d), lambda j: (proj_blk(j) // bps, 0, 0)),
                  pl.BlockSpec((tb, RET_D), lambda j: (proj_blk(j) % bps, 0)),
                  pl.BlockSpec((tb, RET_D), lambda j: (proj_blk(j) % bps, 0)),
                  pl.BlockSpec((None, SLAB + GATE_RANK, d), of_layer, **single),
                  pl.BlockSpec((None, GATE_RANK, GLA_KEY_WIDTH), of_layer, **single),
                  pl.BlockSpec((None, 1, GLA_KEY_WIDTH), of_layer, **single),
                  pl.BlockSpec(dmask.shape, const3, **single),
                  pl.BlockSpec(qdec.shape, const3, **single),
                  pl.BlockSpec(kdec.shape, const3, **single),
                  pl.BlockSpec(csum.shape, const2, **single),
                  pl.BlockSpec((None, 1, RET_WIDTH), of_layer, **single),
                  pl.BlockSpec((None, 1, GLA_WIDTH), of_layer, **single),
                  pl.BlockSpec((None, wo_rows, d), lambda j: (layer, proj_blk(j), 0)),
                  pl.BlockSpec((None, w1_rows, d_ff), lambda j: (layer, proj_blk(j), 0)),
                  pl.BlockSpec((None, w2_rows, d), lambda j: (layer, proj_blk(j), 0))],
        out_specs=[pl.BlockSpec((1, tb, D_MIX), lambda j: (mix_blk(j) // bps, mix_blk(j) % bps, 0)),
                   pl.BlockSpec((wo_rows, d), lambda j: (proj_blk(j), 0)),
                   pl.BlockSpec((w1_rows, d_ff), lambda j: (proj_blk(j), 0)),
                   pl.BlockSpec((w2_rows, d), lambda j: (proj_blk(j), 0))],
        out_shape=[jax.ShapeDtypeStruct((bsz, s, D_MIX), BF16),
                   jax.ShapeDtypeStruct((d, d), BF16),
                   jax.ShapeDtypeStruct((d, d_ff), BF16),
                   jax.ShapeDtypeStruct((d_ff, d), BF16)],
        scratch_shapes=[pltpu.VMEM((d, SLAB), BF16), pltpu.VMEM((GATE_RANK, d), BF16),
                        pltpu.VMEM((2, tb, SLAB), BF16), pltpu.VMEM((2, tb, GLA_KEY_WIDTH), F32),
                        pltpu.SMEM((2,), jnp.int32),
                        pltpu.VMEM((RET_HEADS, RET_D, RET_D), F32),
                        pltpu.VMEM((GLA_HEADS, GLA_DV, LANES), F32)],
        compiler_params=pltpu.CompilerParams(
            dimension_semantics=("arbitrary",), vmem_limit_bytes=VMEM_LIMIT),
        name="projmix",
    )(x, mod, cos_t, sin_t, jnp.swapaxes(w_in, 1, 2), gate_w, _layer_row(gate_b),
      jnp.asarray(dmask), jnp.asarray(qdec), jnp.asarray(kdec), jnp.asarray(csum, BF16),
      _layer_row(ret_norm_w), _layer_row(gla_norm_w), w_out, w_ff1, w_ff2)


def _ffn_kernel(x_ref, mix_ref, mod_ref, wo_ref, w1_ref, w2_ref, ln1w_ref, ln1b_ref,
                ln2w_ref, ln2b_ref, out_ref, h_ref, *, alpha):
    gate1 = mod_ref[0, 2:3, :]
    shift2 = mod_ref[0, 3:4, :]
    scale2 = mod_ref[0, 4:5, :]
    gate2 = mod_ref[0, 5:6, :]
    d_ff = w1_ref.shape[1]
    n_groups = 4
    group = d_ff // n_groups
    tb = x_ref.shape[1]
    halves = (slice(0, tb // 2), slice(tb // 2, tb))

    def chain(rows, wait):
        m = _dot(mix_ref[0, rows, :], wo_ref[...])
        yield
        x1 = (_ln(alpha * x_ref[0, rows, :] + gate1 * m, wait.get("ln1")) * ln1w_ref[...]
              + ln1b_ref[...])
        u2f = _ln(x1) * (1.0 + scale2) + shift2
        wait["ln1_done"] = u2f[u2f.shape[0] - 8:, u2f.shape[1] - LANES:]
        u2 = u2f.astype(BF16)
        yield
        for j in range(n_groups):
            hj = jnp.maximum(_dot(u2, w1_ref[:, j * group:(j + 1) * group]), 0.0)
            h_ref[rows, j * group:(j + 1) * group] = (hj * hj).astype(BF16)
            yield
        f = _dot(h_ref[rows, :], w2_ref[...])
        yield
        out_ref[0, rows, :] = _ln(alpha * x1 + gate2 * f) * ln2w_ref[...] + ln2b_ref[...]
        yield

    wait_a, wait_b = {}, {}
    half_a, half_b = chain(halves[0], wait_a), chain(halves[1], wait_b)
    next(half_a)
    next(half_b)
    next(half_a)
    wait_b["ln1"] = wait_a["ln1_done"]
    n_rest = n_groups + 2
    _interleave(half_a, n_rest, half_b, n_rest + 1)


def _ffn(x, mixed, mod, w_out, w1, w2, ln1_w, ln1_b, ln2_w, ln2_b, layer, alpha, tb):
    bsz, s, d = x.shape
    nb = s // tb
    d_ff = w1.shape[1]
    const2 = lambda b, i: (0, 0)
    of_layer = lambda b, i: (layer, 0, 0)
    single = dict(pipeline_mode=pl.Buffered(1))
    return pl.pallas_call(
        functools.partial(_ffn_kernel, alpha=alpha),
        grid=(bsz, nb),
        in_specs=[pl.BlockSpec((1, tb, d), lambda b, i: (b, i, 0)),
                  pl.BlockSpec((1, tb, d), lambda b, i: (b, i, 0)),
                  pl.BlockSpec((1, 6, d), lambda b, i: (b, 0, 0)),
                  pl.BlockSpec((d, d), const2, **single),
                  pl.BlockSpec((d, d_ff), const2, **single),
                  pl.BlockSpec((d_ff, d), const2, **single),
                  pl.BlockSpec((None, 1, d), of_layer, **single),
                  pl.BlockSpec((None, 1, d), of_layer, **single),
                  pl.BlockSpec((None, 1, d), of_layer, **single),
                  pl.BlockSpec((None, 1, d), of_layer, **single)],
        out_specs=pl.BlockSpec((1, tb, d), lambda b, i: (b, i, 0)),
        out_shape=jax.ShapeDtypeStruct((bsz, s, d), x.dtype),
        scratch_shapes=[pltpu.VMEM((tb, d_ff), BF16)],
        compiler_params=pltpu.CompilerParams(
            dimension_semantics=("parallel", "parallel"), vmem_limit_bytes=VMEM_LIMIT),
        name="ffn",
    )(x, mixed, mod, w_out, w1, w2, _layer_row(ln1_w), _layer_row(ln1_b), _layer_row(ln2_w),
      _layer_row(ln2_b))


def kernel(x, c, w_ada, b_ada, w_in, ret_norm_w, gla_gate_w, gla_gate_b, gla_norm_w,
           w_out, ln1_w, ln1_b, w_ff1, w_ff2, ln2_w, ln2_b):
    depth = w_in.shape[0]
    bsz, s, d = x.shape
    alpha = (2.0 * depth) ** 0.25
    tb = min(512, s)
    cos_t, sin_t = (jnp.asarray(t) for t in _rotary_tables(s))
    for l in range(depth):
        mod = _adaln(c, w_ada, b_ada, l)
        mixed, wo_bf, w1_bf, w2_bf = _projmix(x, mod, cos_t, sin_t, w_in, gla_gate_w, gla_gate_b,
                                              ret_norm_w, gla_norm_w, w_out, w_ff1, w_ff2, l, tb)
        x = _ffn(x, mixed, mod, wo_bf, w1_bf, w2_bf, ln1_w, ln1_b, ln2_w, ln2_b, l, alpha,
                 min(1024, s))
    return x
```

```python
import functools
import math

import numpy as np
import jax
import jax.numpy as jnp
from jax import lax
from jax.experimental import pallas as pl
from jax.experimental.pallas import tpu as pltpu

CHUNK = 64
RET_HEADS = 4
RET_D = 128
RET_WIDTH = RET_HEADS * RET_D
GLA_HEADS = 4
GLA_DK = 64
GLA_DV = 128
GLA_KEY_WIDTH = GLA_HEADS * GLA_DK
GLA_WIDTH = GLA_HEADS * GLA_DV
D_MIX = RET_WIDTH + GLA_WIDTH
GATE_RANK = 16
GATE_TAU = 16.0
ROPE_BASE = 10000.0
LN_EPS = 1e-5
LANES = 128
SLAB = 4 * RET_WIDTH + 2 * GLA_KEY_WIDTH + 2 * GLA_WIDTH
OFF_RQ, OFF_RK, OFF_RV, OFF_RG = 0, 512, 1024, 1536
OFF_GQ, OFF_GK, OFF_GV, OFF_GG = 2048, 2304, 2560, 3072
MIX_ROWS = 256
BOUNDED_LOG2_DECAY = 96.0
BOUNDED_SPAN = 2 * CHUNK
VMEM_LIMIT = 56 * 1024 * 1024
LOG2E = math.log2(math.e)

BF16 = jnp.bfloat16
F32 = jnp.float32


def _dot(a, b):
    return jnp.dot(a, b, preferred_element_type=F32)


def _dot_nt(a, b):
    return lax.dot_general(a, b, (((1,), (1,)), ((), ())), preferred_element_type=F32)


def _dot_tn(a, b):
    return lax.dot_general(a, b, (((0,), (0,)), ((), ())), preferred_element_type=F32)


def _zero_of(v):
    bits = lax.shift_right_logical(pltpu.bitcast(v, jnp.uint32), jnp.uint32(32))
    return pltpu.bitcast(bits, F32)


def _ln(x, not_before=None):
    mu = jnp.mean(x, axis=-1, keepdims=True)
    if not_before is not None:
        mu = mu + _zero_of(not_before)[0:1, 0:1]
    xc = x - mu
    var = jnp.mean(xc * xc, axis=-1, keepdims=True)
    return xc * lax.rsqrt(var + LN_EPS)


def _silu(x):
    return x * (1.0 / (1.0 + jnp.exp(-x)))


def _adaln_kernel(c_ref, w_ref, b_ref, o_ref):
    c = c_ref[...]
    val = _dot(_silu(c).astype(BF16), w_ref[...].astype(BF16)) + b_ref[...]
    o_ref[:, pl.ds(pl.program_id(0), 1), :] = val[:, None, :]


def _adaln(c, w_ada, b_ada, layer):
    bsz, d = c.shape
    depth, _, n = w_ada.shape
    return pl.pallas_call(
        _adaln_kernel,
        grid=(n // d,),
        in_specs=[pl.BlockSpec((bsz, d), lambda j: (0, 0)),
                  pl.BlockSpec((None, d, d), lambda j: (layer, 0, j)),
                  pl.BlockSpec((None, 1, d), lambda j: (layer, 0, j))],
        out_specs=pl.BlockSpec((bsz, n // d, d), lambda j: (0, 0, 0)),
        out_shape=jax.ShapeDtypeStruct((bsz, n // d, d), F32),
        compiler_params=pltpu.CompilerParams(vmem_limit_bytes=VMEM_LIMIT),
        name="adaln",
    )(c, w_ada, b_ada.reshape(depth, 1, n))


def _rotary_tables(s):
    half = RET_D // 2
    inv = 1.0 / (ROPE_BASE ** (np.arange(half, dtype=np.float64) / (half - 1)))
    ang = np.arange(s, dtype=np.float64)[:, None] * inv[None, :]
    cos = np.concatenate([np.cos(ang), np.cos(ang)], axis=1)
    sin = np.concatenate([-np.sin(ang), np.sin(ang)], axis=1)
    return cos.astype(np.float32), sin.astype(np.float32)


def _retention_tables(tb):
    log_gamma = np.log(1.0 - 2.0 ** (-5.0 - np.arange(RET_HEADS, dtype=np.float64)))
    idx = np.arange(tb)
    dist = np.abs(idx[:, None] - idx[None, :])
    visible = (idx[None, :] // CHUNK) <= (idx[:, None] // CHUNK)
    scale = RET_D ** -0.5
    dmask = np.exp(log_gamma[:, None, None] * dist[None]) * visible[None] * scale
    qdec = np.exp(log_gamma[:, None] * (idx[None, :] + 1.0)) * scale
    kdec = np.exp(log_gamma[:, None] * (tb - 1.0 - idx[None, :]))
    qdec = np.broadcast_to(qdec[:, :, None], (RET_HEADS, tb, RET_D))
    kdec = np.broadcast_to(kdec[:, :, None], (RET_HEADS, tb, RET_D))
    block_decay = [float(np.exp(lg * tb)) for lg in log_gamma]
    return (dmask.astype(np.float32), np.ascontiguousarray(qdec, np.float32),
            np.ascontiguousarray(kdec, np.float32), block_decay)


def _cumsum_matrix(tb):
    t = np.arange(tb)
    same_chunk = (t[:, None] // CHUNK) == (t[None, :] // CHUNK)
    return (same_chunk & (t[None, :] <= t[:, None])).astype(np.float32)


def _proj_stage(x_ref, mod_ref, cos_ref, sin_ref, w_ref, wtail_ref, gw_ref, gb_ref, slab, la_out,
                bounded_ref, slot):
    x = x_ref[0]
    shift1 = mod_ref[0, 0:1, :]
    scale1 = mod_ref[0, 1:2, :]
    ub = (_ln(x) * (1.0 + scale1) + shift1).astype(BF16)
    yield
    cos = cos_ref[...]
    sin = sin_ref[...]
    group = 256
    for j in range(SLAB // group):
        p = _dot(ub, w_ref[:, j * group:(j + 1) * group])
        if j * group < OFF_RV:
            for h in range(group // RET_D):
                xh = p[:, h * RET_D:(h + 1) * RET_D]
                rot = xh * cos + pltpu.roll(xh, RET_D // 2, 1) * sin
                slab[:, j * group + h * RET_D:j * group + (h + 1) * RET_D] = rot.astype(BF16)
        else:
            slab[:, j * group:(j + 1) * group] = p.astype(BF16)
        yield
    glr = _dot_nt(ub, wtail_ref[...])
    logit = _dot(glr.astype(BF16), gw_ref[...].astype(BF16)) + gb_ref[...]
    log_sig = jnp.minimum(logit, 0.0) - jnp.log1p(jnp.exp(-jnp.abs(logit)))
    la = log_sig * (LOG2E / GATE_TAU)
    la_out[...] = la
    bounded_ref[slot] = (jnp.min(la) * BOUNDED_SPAN >= -BOUNDED_LOG2_DECAY).astype(jnp.int32)
    yield


def _row_of_block(b, block, row):
    n = b.shape[0] // block
    b3 = b.reshape(n, block, b.shape[1])
    return jnp.broadcast_to(b3[:, row:row + 1, :], b3.shape).reshape(b.shape)


def _mix_stage(slab, la_ref, r0, dmask_ref, qdec_ref, kdec_ref, csum_ref, rnw_ref, gnw_ref,
               out_ref, r_state, g_state, block_decay, bounded):
    tb = MIX_ROWS
    blk = slice(r0, r0 + tb)
    for h in range(RET_HEADS):
        q = slab[blk, OFF_RQ + h * RET_D:OFF_RQ + (h + 1) * RET_D]
        k = slab[blk, OFF_RK + h * RET_D:OFF_RK + (h + 1) * RET_D]
        v = slab[blk, OFF_RV + h * RET_D:OFF_RV + (h + 1) * RET_D]
        g = slab[blk, OFF_RG + h * RET_D:OFF_RG + (h + 1) * RET_D].astype(F32)
        p = (_dot_nt(q, k) * dmask_ref[h]).astype(BF16)
        r_prev = r_state[h]
        o = _dot(p, v) + _dot(q, r_prev.astype(BF16)) * qdec_ref[h]
        kd = (k.astype(F32) * kdec_ref[h]).astype(BF16)
        r_state[h] = block_decay[h] * r_prev + _dot_tn(kd, v)
        o = o - jnp.mean(o, axis=-1, keepdims=True)
        o = o * lax.rsqrt(jnp.mean(o * o, axis=-1, keepdims=True) + LN_EPS)
        o = o * rnw_ref[:, h * RET_D:(h + 1) * RET_D] * _silu(g)
        out_ref[0, blk, h * RET_D:(h + 1) * RET_D] = o.astype(out_ref.dtype)
        yield

    la = la_ref[blk, :]
    la_hi = la.astype(BF16)
    la_lo = (la - la_hi.astype(F32)).astype(BF16)
    csum = csum_ref[...]
    b = _dot(csum, la_hi) + _dot(csum, la_lo)
    gq = slab[blk, OFF_GQ:OFF_GQ + GLA_KEY_WIDTH] * jnp.asarray(GLA_DK ** -0.5, BF16)
    gk = slab[blk, OFF_GK:OFF_GK + GLA_KEY_WIDTH]
    row = lax.broadcasted_iota(jnp.int32, (tb, GLA_KEY_WIDTH), 0)
    span = BOUNDED_SPAN if bounded else CHUNK
    if bounded:
        for c in range(1, span // CHUNK):
            carry = _row_of_block(b, span, c * CHUNK - 1)
            b = b + jnp.where((row & (span - 1)) >= c * CHUNK, carry, 0.0)
    q_state = gq * jnp.exp2(b).astype(BF16)
    k_state = gk * jnp.exp2(_row_of_block(b, span, span - 1) - b).astype(BF16)
    if bounded:
        grow = jnp.exp2(-b).astype(BF16)
        q_lvl = [q_state, gq * grow]
        phis = [grow, jnp.exp2(b).astype(BF16)]
    else:
        psis = [-jnp.abs(b - _row_of_block(b, 2 * c, c - 1)) for c in (32, 16, 8, 4)]
        la_next = pltpu.roll(la, tb - 1, 0)
        la_prev = pltpu.roll(la, 1, 0)
        r4 = row & 3
        psis.append(jnp.where(r4 == 0, la_next,
                              jnp.where(r4 == 1, 0.0, jnp.where(r4 == 2, la, la + la_prev))))
        psis.append(jnp.where((row & 1) == 1, la, 0.0))
        phis = [jnp.exp2(p).astype(BF16) for p in psis]
        q_lvl = [gq] + [gq * phi for phi in phis]
        phis = [None] + phis
    yield

    lane = lax.broadcasted_iota(jnp.int32, (span, LANES), 1)
    low_half = lane < GLA_DK
    zero_k = jnp.zeros((span, LANES), BF16)
    zero_s = jnp.zeros((GLA_DV, LANES), BF16)
    t_idx = lax.broadcasted_iota(jnp.int32, (span, GLA_HEADS * span), 0)
    s_idx = lax.broadcasted_iota(jnp.int32, (span, GLA_HEADS * span), 1) & (span - 1)
    t_xor_s = t_idx ^ s_idx
    lane_s = lax.broadcasted_iota(jnp.int32, (GLA_DV, LANES), 1)
    low_half_s = lane_s < GLA_DK

    def lane_tiles(piece, tile, zero, n_tiles):
        return jnp.concatenate([piece if j == tile else zero for j in range(n_tiles)], axis=1)

    for ch in range(tb // span):
        rows = slice(ch * span, (ch + 1) * span)
        orow = slice(r0 + ch * span, r0 + (ch + 1) * span)
        k_heads = [jnp.where(low_half if h % 2 == 0 else ~low_half,
                             gk[rows, (h // 2) * LANES:(h // 2 + 1) * LANES], zero_k)
                   for h in range(GLA_HEADS)]
        a_lvl = []
        for lvl in range(len(q_lvl)):
            pieces = []
            for h in range(GLA_HEADS):
                tile = h // 2
                kh = k_heads[h]
                if phis[lvl] is not None:
                    kh = kh * phis[lvl][rows, tile * LANES:(tile + 1) * LANES]
                pieces.append(lane_tiles(kh, tile, zero_k, 2))
            k_stack = jnp.concatenate(pieces, axis=0)
            a_lvl.append(_dot_nt(q_lvl[lvl][rows], k_stack))
        if bounded:
            same_chunk = (t_idx // CHUNK) == (s_idx // CHUNK)
            a = jnp.where(t_idx >= s_idx, a_lvl[0], jnp.where(same_chunk, a_lvl[1], 0.0))
        else:
            a = a_lvl[0]
            for i, c in enumerate((1, 2, 4, 8, 16, 32)):
                a = jnp.where(t_xor_s >= c, a_lvl[len(a_lvl) - 1 - i], a)
        a = a.astype(BF16)
        gv = slab[orow, OFF_GV:OFF_GV + GLA_WIDTH]
        v_bd = jnp.concatenate(
            [lane_tiles(gv[:, h * GLA_DV:(h + 1) * GLA_DV], h, jnp.zeros((span, GLA_DV), BF16), GLA_HEADS)
             for h in range(GLA_HEADS)], axis=0)
        s_bd = jnp.concatenate(
            [lane_tiles(g_state[h].astype(BF16), h // 2, zero_s, 2) for h in range(GLA_HEADS)],
            axis=0)
        o_all = _dot(a, v_bd) + _dot_nt(q_state[rows], s_bd)
        decay = jnp.exp2(b[(ch + 1) * span - 1:(ch + 1) * span, :])
        for h in range(GLA_HEADS):
            tile = h // 2
            upd = _dot_tn(gv[:, h * GLA_DV:(h + 1) * GLA_DV], k_state[rows, tile * LANES:(tile + 1) * LANES])
            upd = jnp.where(low_half_s if h % 2 == 0 else ~low_half_s, upd, 0.0)
            g_state[h] = g_state[h] * decay[:, tile * LANES:(tile + 1) * LANES] + upd
        for h in range(GLA_HEADS):
            o = o_all[:, h * GLA_DV:(h + 1) * GLA_DV]
            o = o * lax.rsqrt(jnp.mean(o * o, axis=-1, keepdims=True) + LN_EPS)
            gg = slab[orow, OFF_GG + h * GLA_DV:OFF_GG + (h + 1) * GLA_DV].astype(F32)
            o = o * gnw_ref[:, h * GLA_DV:(h + 1) * GLA_DV] * _silu(gg)
            out_ref[0, orow, RET_WIDTH + h * GLA_DV:RET_WIDTH + (h + 1) * GLA_DV] = o.astype(out_ref.dtype)
        yield


def _interleave(gen_a, n_a, gen_b, n_b):
    done_a = done_b = 0
    while done_a < n_a or done_b < n_b:
        if done_b >= n_b or (done_a < n_a and (done_a + 0.5) * n_b <= (done_b + 0.5) * n_a):
            next(gen_a)
            done_a += 1
        else:
            next(gen_b)
            done_b += 1
    for g in (gen_a, gen_b):
        assert next(g, "done") == "done", "piece count mismatch"


def _projmix_kernel(x_ref, mod_ref, cos_ref, sin_ref, w_ref, gw_ref, gb_ref,
                    dmask_ref, qdec_ref, kdec_ref, csum_ref, rnw_ref, gnw_ref,
                    wo_ref, w1_ref, w2_ref,
                    out_ref, wo_bf_ref, w1_bf_ref, w2_bf_ref,
                    w_bf, w_tail, slab2, la2, bounded_ref, r_state, g_state,
                    *, tb, blocks_per_seq, n_blocks, block_decay):
    j = pl.program_id(0)
    n_proj = 2 + SLAB // 256
    n_mix = {bounded: (tb // MIX_ROWS) * (RET_HEADS + 1 + MIX_ROWS // (BOUNDED_SPAN if bounded else CHUNK))
             for bounded in (True, False)}

    @pl.when(j == 0)
    def _():
        for g in range(0, SLAB, 256):
            w_bf[:, g:g + 256] = w_ref[g:g + 256, :].T.astype(BF16)
        w_tail[...] = w_ref[SLAB:SLAB + GATE_RANK, :].astype(BF16)
        bounded_ref[1] = 0

    wo_bf_ref[...] = wo_ref[...].astype(BF16)
    w1_bf_ref[...] = w1_ref[...].astype(BF16)
    w2_bf_ref[...] = w2_ref[...].astype(BF16)

    @pl.when((j - 1) % blocks_per_seq == 0)
    def _():
        r_state[...] = jnp.zeros_like(r_state)
        g_state[...] = jnp.zeros_like(g_state)

    def project(buf):
        return _proj_stage(x_ref, mod_ref, cos_ref, sin_ref, w_bf, w_tail, gw_ref, gb_ref,
                           slab2.at[buf], la2.at[buf], bounded_ref, buf)

    def mix(buf, bounded):
        for r0 in range(0, tb, MIX_ROWS):
            yield from _mix_stage(slab2.at[buf], la2.at[buf], r0, dmask_ref, qdec_ref, kdec_ref,
                                  csum_ref, rnw_ref, gnw_ref, out_ref, r_state, g_state, block_decay,
                                  bounded)

    @pl.when(j == 0)
    def _():
        for _ in project(0):
            pass

    steady = (j > 0) & (j < n_blocks)
    bounded = bounded_ref[1 - j % 2] == 1
    for parity in (0, 1):
        @pl.when(steady & bounded & (j % 2 == parity))
        def _():
            _interleave(project(parity), n_proj, mix(1 - parity, True), n_mix[True])

    @pl.when(steady & jnp.logical_not(bounded))
    def _():
        _interleave(project(j % 2), n_proj, mix(1 - j % 2, False), n_mix[False])

    @pl.when(j == n_blocks)
    def _():
        for _ in mix((n_blocks - 1) % 2, False):
            pass


def _layer_row(v):
    return v.reshape(v.shape[0], 1, v.shape[1])


def _projmix(x, mod, cos_t, sin_t, w_in, gate_w, gate_b, ret_norm_w, gla_norm_w,
             w_out, w_ff1, w_ff2, layer, tb):
    bsz, s, d = x.shape
    d_ff = w_ff1.shape[2]
    bps = s // tb
    nblk = bsz * bps
    dmask, qdec, kdec, block_decay = _retention_tables(MIX_ROWS)
    csum = _cumsum_matrix(MIX_ROWS)
    const2 = lambda j: (0, 0)
    const3 = lambda j: (0, 0, 0)
    of_layer = lambda j: (layer, 0, 0)
    proj_blk = lambda j: jnp.minimum(j, nblk - 1)
    mix_blk = lambda j: jnp.maximum(j - 1, 0)
    single = dict(pipeline_mode=pl.Buffered(1))
    wo_rows, w1_rows, w2_rows = d // nblk, d // nblk, d_ff // nblk
    return pl.pallas_call(
        functools.partial(_projmix_kernel, tb=tb, blocks_per_seq=bps, n_blocks=nblk,
                          block_decay=block_decay),
        grid=(nblk + 1,),
        in_specs=[pl.BlockSpec((1, tb, d), lambda j: (proj_blk(j) // bps, proj_blk(j) % bps, 0)),
                  pl.BlockSpec((1, 6, d), lambda j: (proj_blk(j) // bps, 0, 0)),
                  pl.BlockSpec((tb, RET_D), lambda j: (proj_blk(j) % bps, 0)),
                  pl.BlockSpec((tb, RET_D), lambda j: (proj_blk(j) % bps, 0)),
                  pl.BlockSpec((None, SLAB + GATE_RANK, d), of_layer, **single),
                  pl.BlockSpec((None, GATE_RANK, GLA_KEY_WIDTH), of_layer, **single),
                  pl.BlockSpec((None, 1, GLA_KEY_WIDTH), of_layer, **single),
                  pl.BlockSpec(dmask.shape, const3, **single),
                  pl.BlockSpec(qdec.shape, const3, **single),
                  pl.BlockSpec(kdec.shape, const3, **single),
                  pl.BlockSpec(csum.shape, const2, **single),
                  pl.BlockSpec((None, 1, RET_WIDTH), of_layer, **single),
                  pl.BlockSpec((None, 1, GLA_WIDTH), of_layer, **single),
                  pl.BlockSpec((None, wo_rows, d), lambda j: (layer, proj_blk(j), 0)),
                  pl.BlockSpec((None, w1_rows, d_ff), lambda j: (layer, proj_blk(j), 0)),
                  pl.BlockSpec((None, w2_rows, d), lambda j: (layer, proj_blk(j), 0))],
        out_specs=[pl.BlockSpec((1, tb, D_MIX), lambda j: (mix_blk(j) // bps, mix_blk(j) % bps, 0)),
                   pl.BlockSpec((wo_rows, d), lambda j: (proj_blk(j), 0)),
                   pl.BlockSpec((w1_rows, d_ff), lambda j: (proj_blk(j), 0)),
                   pl.BlockSpec((w2_rows, d), lambda j: (proj_blk(j), 0))],
        out_shape=[jax.ShapeDtypeStruct((bsz, s, D_MIX), BF16),
                   jax.ShapeDtypeStruct((d, d), BF16),
                   jax.ShapeDtypeStruct((d, d_ff), BF16),
                   jax.ShapeDtypeStruct((d_ff, d), BF16)],
        scratch_shapes=[pltpu.VMEM((d, SLAB), BF16), pltpu.VMEM((GATE_RANK, d), BF16),
                        pltpu.VMEM((2, tb, SLAB), BF16), pltpu.VMEM((2, tb, GLA_KEY_WIDTH), F32),
                        pltpu.SMEM((2,), jnp.int32),
                        pltpu.VMEM((RET_HEADS, RET_D, RET_D), F32),
                        pltpu.VMEM((GLA_HEADS, GLA_DV, LANES), F32)],
        compiler_params=pltpu.CompilerParams(
            dimension_semantics=("arbitrary",), vmem_limit_bytes=VMEM_LIMIT),
        name="projmix",
    )(x, mod, cos_t, sin_t, jnp.swapaxes(w_in, 1, 2), gate_w, _layer_row(gate_b),
      jnp.asarray(dmask), jnp.asarray(qdec), jnp.asarray(kdec), jnp.asarray(csum, BF16),
      _layer_row(ret_norm_w), _layer_row(gla_norm_w), w_out, w_ff1, w_ff2)


def _ffn_kernel(x_ref, mix_ref, mod_ref, wo_ref, w1_ref, w2_ref, ln1w_ref, ln1b_ref,
                ln2w_ref, ln2b_ref, out_ref, h_ref, *, alpha):
    gate1 = mod_ref[0, 2:3, :]
    shift2 = mod_ref[0, 3:4, :]
    scale2 = mod_ref[0, 4:5, :]
    gate2 = mod_ref[0, 5:6, :]
    d_ff = w1_ref.shape[1]
    n_groups = 4
    group = d_ff // n_groups
    tb = x_ref.shape[1]
    halves = (slice(0, tb // 2), slice(tb // 2, tb))

    def chain(rows, wait):
        m = _dot(mix_ref[0, rows, :], wo_ref[...])
        yield
        x1 = (_ln(alpha * x_ref[0, rows, :] + gate1 * m, wait.get("ln1")) * ln1w_ref[...]
              + ln1b_ref[...])
        u2f = _ln(x1) * (1.0 + scale2) + shift2
        wait["ln1_done"] = u2f[u2f.shape[0] - 8:, u2f.shape[1] - LANES:]
        u2 = u2f.astype(BF16)
        yield
        for j in range(n_groups):
            hj = jnp.maximum(_dot(u2, w1_ref[:, j * group:(j + 1) * group]), 0.0)
            h_ref[rows, j * group:(j + 1) * group] = (hj * hj).astype(BF16)
            yield
        f = _dot(h_ref[rows, :], w2_ref[...])
        yield
        out_ref[0, rows, :] = _ln(alpha * x1 + gate2 * f) * ln2w_ref[...] + ln2b_ref[...]
        yield

    wait_a, wait_b = {}, {}
    half_a, half_b = chain(halves[0], wait_a), chain(halves[1], wait_b)
    next(half_a)
    next(half_b)
    next(half_a)
    wait_b["ln1"] = wait_a["ln1_done"]
    n_rest = n_groups + 2
    _interleave(half_a, n_rest, half_b, n_rest + 1)


def _ffn(x, mixed, mod, w_out, w1, w2, ln1_w, ln1_b, ln2_w, ln2_b, layer, alpha, tb):
    bsz, s, d = x.shape
    nb = s // tb
    d_ff = w1.shape[1]
    const2 = lambda b, i: (0, 0)
    of_layer = lambda b, i: (layer, 0, 0)
    single = dict(pipeline_mode=pl.Buffered(1))
    return pl.pallas_call(
        functools.partial(_ffn_kernel, alpha=alpha),
        grid=(bsz, nb),
        in_specs=[pl.BlockSpec((1, tb, d), lambda b, i: (b, i, 0)),
                  pl.BlockSpec((1, tb, d), lambda b, i: (b, i, 0)),
                  pl.BlockSpec((1, 6, d), lambda b, i: (b, 0, 0)),
                  pl.BlockSpec((d, d), const2, **single),
                  pl.BlockSpec((d, d_ff), const2, **single),
                  pl.BlockSpec((d_ff, d), const2, **single),
                  pl.BlockSpec((None, 1, d), of_layer, **single),
                  pl.BlockSpec((None, 1, d), of_layer, **single),
                  pl.BlockSpec((None, 1, d), of_layer, **single),
                  pl.BlockSpec((None, 1, d), of_layer, **single)],
        out_specs=pl.BlockSpec((1, tb, d), lambda b, i: (b, i, 0)),
        out_shape=jax.ShapeDtypeStruct((bsz, s, d), x.dtype),
        scratch_shapes=[pltpu.VMEM((tb, d_ff), BF16)],
        compiler_params=pltpu.CompilerParams(
            dimension_semantics=("parallel", "parallel"), vmem_limit_bytes=VMEM_LIMIT),
        name="ffn",
    )(x, mixed, mod, w_out, w1, w2, _layer_row(ln1_w), _layer_row(ln1_b), _layer_row(ln2_w),
      _layer_row(ln2_b))


def kernel(x, c, w_ada, b_ada, w_in, ret_norm_w, gla_gate_w, gla_gate_b, gla_norm_w,
           w_out, ln1_w, ln1_b, w_ff1, w_ff2, ln2_w, ln2_b):
    depth = w_in.shape[0]
    bsz, s, d = x.shape
    alpha = (2.0 * depth) ** 0.25
    tb = min(512, s)
    cos_t, sin_t = (jnp.asarray(t) for t in _rotary_tables(s))
    for l in range(depth):
        mod = _adaln(c, w_ada, b_ada, l)
        mixed, wo_bf, w1_bf, w2_bf = _projmix(x, mod, cos_t, sin_t, w_in, gla_gate_w, gla_gate_b,
                                              ret_norm_w, gla_norm_w, w_out, w_ff1, w_ff2, l, tb)
        x = _ffn(x, mixed, mod, wo_bf, w1_bf, w2_bf, ln1_w, ln1_b, ln2_w, ln2_b, l, alpha,
                 min(1024, s))
    return x
```

```python
import functools
import math

import numpy as np
import jax
import jax.numpy as jnp
from jax import lax
from jax.experimental import pallas as pl
from jax.experimental.pallas import tpu as pltpu

CHUNK = 64
RET_HEADS = 4
RET_D = 128
RET_WIDTH = RET_HEADS * RET_D
GLA_HEADS = 4
GLA_DK = 64
GLA_DV = 128
GLA_KEY_WIDTH = GLA_HEADS * GLA_DK
GLA_WIDTH = GLA_HEADS * GLA_DV
D_MIX = RET_WIDTH + GLA_WIDTH
GATE_RANK = 16
GATE_TAU = 16.0
ROPE_BASE = 10000.0
LN_EPS = 1e-5
LANES = 128
MXU_WIDTH = 256
PROJMIX_ROWS = 512
FFN_ROWS = 1024
SLAB = 4 * RET_WIDTH + 2 * GLA_KEY_WIDTH + 2 * GLA_WIDTH
OFF_RQ, OFF_RK, OFF_RV, OFF_RG = 0, 512, 1024, 1536
OFF_GQ, OFF_GK, OFF_GV, OFF_GG = 2048, 2304, 2560, 3072
MIX_ROWS = 256
BOUNDED_LOG2_DECAY = 96.0
BOUNDED_SPAN = 2 * CHUNK
VMEM_LIMIT = 56 * 1024 * 1024
LOG2E = math.log2(math.e)

BF16 = jnp.bfloat16
F32 = jnp.float32


def _dot(a, b):
    return jnp.dot(a, b, preferred_element_type=F32)


def _dot_nt(a, b):
    return lax.dot_general(a, b, (((1,), (1,)), ((), ())), preferred_element_type=F32)


def _dot_tn(a, b):
    return lax.dot_general(a, b, (((0,), (0,)), ((), ())), preferred_element_type=F32)


def _zero_of(v):
    bits = lax.shift_right_logical(pltpu.bitcast(v, jnp.uint32), jnp.uint32(32))
    return pltpu.bitcast(bits, F32)


def _ln(x, not_before=None):
    mu = jnp.mean(x, axis=-1, keepdims=True)
    if not_before is not None:
        mu = mu + _zero_of(not_before)[0:1, 0:1]
    xc = x - mu
    var = jnp.mean(xc * xc, axis=-1, keepdims=True)
    return xc * lax.rsqrt(var + LN_EPS)


def _silu(x):
    return x * (1.0 / (1.0 + jnp.exp(-x)))


def _adaln_kernel(c_ref, w_ref, b_ref, o_ref):
    c = c_ref[...]
    val = _dot(_silu(c).astype(BF16), w_ref[...].astype(BF16)) + b_ref[...]
    o_ref[:, pl.ds(pl.program_id(0), 1), :] = val[:, None, :]


def _adaln(c, w_ada, b_ada, layer):
    bsz, d = c.shape
    depth, _, n = w_ada.shape
    return pl.pallas_call(
        _adaln_kernel,
        grid=(n // d,),
        in_specs=[pl.BlockSpec((bsz, d), lambda j: (0, 0)),
                  pl.BlockSpec((None, d, d), lambda j: (layer, 0, j)),
                  pl.BlockSpec((None, 1, d), lambda j: (layer, 0, j))],
        out_specs=pl.BlockSpec((bsz, n // d, d), lambda j: (0, 0, 0)),
        out_shape=jax.ShapeDtypeStruct((bsz, n // d, d), F32),
        compiler_params=pltpu.CompilerParams(vmem_limit_bytes=VMEM_LIMIT),
        name="adaln",
    )(c, w_ada, b_ada.reshape(depth, 1, n))


def _rotary_tables(s):
    half = RET_D // 2
    inv = 1.0 / (ROPE_BASE ** (np.arange(half, dtype=np.float64) / (half - 1)))
    ang = np.arange(s, dtype=np.float64)[:, None] * inv[None, :]
    cos = np.concatenate([np.cos(ang), np.cos(ang)], axis=1)
    sin = np.concatenate([-np.sin(ang), np.sin(ang)], axis=1)
    return cos.astype(np.float32), sin.astype(np.float32)


def _retention_tables(tb):
    log_gamma = np.log(1.0 - 2.0 ** (-5.0 - np.arange(RET_HEADS, dtype=np.float64)))
    idx = np.arange(tb)
    dist = np.abs(idx[:, None] - idx[None, :])
    visible = (idx[None, :] // CHUNK) <= (idx[:, None] // CHUNK)
    scale = RET_D ** -0.5
    dmask = np.exp(log_gamma[:, None, None] * dist[None]) * visible[None] * scale
    qdec = np.exp(log_gamma[:, None] * (idx[None, :] + 1.0)) * scale
    kdec = np.exp(log_gamma[:, None] * (tb - 1.0 - idx[None, :]))
    qdec = np.broadcast_to(qdec[:, :, None], (RET_HEADS, tb, RET_D))
    kdec = np.broadcast_to(kdec[:, :, None], (RET_HEADS, tb, RET_D))
    block_decay = [float(np.exp(lg * tb)) for lg in log_gamma]
    return (dmask.astype(np.float32), np.ascontiguousarray(qdec, np.float32),
            np.ascontiguousarray(kdec, np.float32), block_decay)


def _cumsum_matrix(tb):
    t = np.arange(tb)
    same_chunk = (t[:, None] // CHUNK) == (t[None, :] // CHUNK)
    return (same_chunk & (t[None, :] <= t[:, None])).astype(np.float32)


def _proj_stage(x_ref, mod_ref, cos_ref, sin_ref, w_ref, wtail_ref, gw_ref, gb_ref, slab, la_out,
                bounded_ref, slot):
    x = x_ref[0]
    shift1 = mod_ref[0, 0:1, :]
    scale1 = mod_ref[0, 1:2, :]
    ub = (_ln(x) * (1.0 + scale1) + shift1).astype(BF16)
    yield
    cos = cos_ref[...]
    sin = sin_ref[...]
    group = MXU_WIDTH
    for j in range(SLAB // group):
        p = _dot(ub, w_ref[:, j * group:(j + 1) * group])
        if j * group < OFF_RV:
            for h in range(group // RET_D):
                xh = p[:, h * RET_D:(h + 1) * RET_D]
                rot = xh * cos + pltpu.roll(xh, RET_D // 2, 1) * sin
                slab[:, j * group + h * RET_D:j * group + (h + 1) * RET_D] = rot.astype(BF16)
        else:
            slab[:, j * group:(j + 1) * group] = p.astype(BF16)
        yield
    glr = _dot_nt(ub, wtail_ref[...])
    logit = _dot(glr.astype(BF16), gw_ref[...].astype(BF16)) + gb_ref[...]
    log_sig = jnp.minimum(logit, 0.0) - jnp.log(1.0 + jnp.exp(-jnp.abs(logit)))
    la = log_sig * (LOG2E / GATE_TAU)
    la_out[...] = la
    bounded_ref[slot] = (jnp.min(la) * BOUNDED_SPAN >= -BOUNDED_LOG2_DECAY).astype(jnp.int32)
    yield


def _row_of_block(b, block, row):
    n = b.shape[0] // block
    b3 = b.reshape(n, block, b.shape[1])
    return jnp.broadcast_to(b3[:, row:row + 1, :], b3.shape).reshape(b.shape)


def _mix_stage(slab, la_ref, r0, dmask_ref, qdec_ref, kdec_ref, csum_ref, rnw_ref, gnw_ref,
               out_ref, r_state, g_state, block_decay, bounded):
    tb = MIX_ROWS
    blk = slice(r0, r0 + tb)
    for h in range(RET_HEADS):
        q = slab[blk, OFF_RQ + h * RET_D:OFF_RQ + (h + 1) * RET_D]
        k = slab[blk, OFF_RK + h * RET_D:OFF_RK + (h + 1) * RET_D]
        v = slab[blk, OFF_RV + h * RET_D:OFF_RV + (h + 1) * RET_D]
        g = slab[blk, OFF_RG + h * RET_D:OFF_RG + (h + 1) * RET_D].astype(F32)
        p = (_dot_nt(q, k) * dmask_ref[h]).astype(BF16)
        r_prev = r_state[h]
        o = _dot(p, v) + _dot(q, r_prev.astype(BF16)) * qdec_ref[h]
        kd = (k.astype(F32) * kdec_ref[h]).astype(BF16)
        r_state[h] = block_decay[h] * r_prev + _dot_tn(kd, v)
        o = o - jnp.mean(o, axis=-1, keepdims=True)
        o = o * lax.rsqrt(jnp.mean(o * o, axis=-1, keepdims=True) + LN_EPS)
        o = o * rnw_ref[:, h * RET_D:(h + 1) * RET_D] * _silu(g)
        out_ref[0, blk, h * RET_D:(h + 1) * RET_D] = o.astype(out_ref.dtype)
        yield

    la = la_ref[blk, :]
    la_hi = la.astype(BF16)
    la_lo = (la - la_hi.astype(F32)).astype(BF16)
    csum = csum_ref[...]
    b = _dot(csum, la_hi) + _dot(csum, la_lo)
    gq = slab[blk, OFF_GQ:OFF_GQ + GLA_KEY_WIDTH] * jnp.asarray(GLA_DK ** -0.5, BF16)
    gk = slab[blk, OFF_GK:OFF_GK + GLA_KEY_WIDTH]
    row = lax.broadcasted_iota(jnp.int32, (tb, GLA_KEY_WIDTH), 0)
    span = BOUNDED_SPAN if bounded else CHUNK
    if bounded:
        for c in range(1, span // CHUNK):
            carry = _row_of_block(b, span, c * CHUNK - 1)
            b = b + jnp.where((row & (span - 1)) >= c * CHUNK, carry, 0.0)
    q_state = gq * jnp.exp2(b).astype(BF16)
    k_state = gk * jnp.exp2(_row_of_block(b, span, span - 1) - b).astype(BF16)
    if bounded:
        grow = jnp.exp2(-b).astype(BF16)
        q_lvl = [q_state, gq * grow]
        phis = [grow, jnp.exp2(b).astype(BF16)]
    else:
        psis = [-jnp.abs(b - _row_of_block(b, 2 * c, c - 1)) for c in (32, 16, 8, 4)]
        la_next = pltpu.roll(la, tb - 1, 0)
        la_prev = pltpu.roll(la, 1, 0)
        r4 = row & 3
        psis.append(jnp.where(r4 == 0, la_next,
                              jnp.where(r4 == 1, 0.0, jnp.where(r4 == 2, la, la + la_prev))))
        psis.append(jnp.where((row & 1) == 1, la, 0.0))
        phis = [jnp.exp2(p).astype(BF16) for p in psis]
        q_lvl = [gq] + [gq * phi for phi in phis]
        phis = [None] + phis
    yield

    lane = lax.broadcasted_iota(jnp.int32, (span, LANES), 1)
    low_half = lane < GLA_DK
    zero_k = jnp.zeros((span, LANES), BF16)
    zero_s = jnp.zeros((GLA_DV, LANES), BF16)
    t_idx = lax.broadcasted_iota(jnp.int32, (span, GLA_HEADS * span), 0)
    s_idx = lax.broadcasted_iota(jnp.int32, (span, GLA_HEADS * span), 1) & (span - 1)
    t_xor_s = t_idx ^ s_idx
    lane_s = lax.broadcasted_iota(jnp.int32, (GLA_DV, LANES), 1)
    low_half_s = lane_s < GLA_DK

    def lane_tiles(piece, tile, zero, n_tiles):
        return jnp.concatenate([piece if j == tile else zero for j in range(n_tiles)], axis=1)

    for ch in range(tb // span):
        rows = slice(ch * span, (ch + 1) * span)
        orow = slice(r0 + ch * span, r0 + (ch + 1) * span)
        k_heads = [jnp.where(low_half if h % 2 == 0 else ~low_half,
                             gk[rows, (h // 2) * LANES:(h // 2 + 1) * LANES], zero_k)
                   for h in range(GLA_HEADS)]
        a_lvl = []
        for lvl in range(len(q_lvl)):
            pieces = []
            for h in range(GLA_HEADS):
                tile = h // 2
                kh = k_heads[h]
                if phis[lvl] is not None:
                    kh = kh * phis[lvl][rows, tile * LANES:(tile + 1) * LANES]
                pieces.append(lane_tiles(kh, tile, zero_k, 2))
            k_stack = jnp.concatenate(pieces, axis=0)
            a_lvl.append(_dot_nt(q_lvl[lvl][rows], k_stack))
        if bounded:
            same_chunk = (t_idx // CHUNK) == (s_idx // CHUNK)
            a = jnp.where(t_idx >= s_idx, a_lvl[0], jnp.where(same_chunk, a_lvl[1], 0.0))
        else:
            a = a_lvl[0]
            for i, c in enumerate((1, 2, 4, 8, 16, 32)):
                a = jnp.where(t_xor_s >= c, a_lvl[len(a_lvl) - 1 - i], a)
        a = a.astype(BF16)
        gv = slab[orow, OFF_GV:OFF_GV + GLA_WIDTH]
        v_bd = jnp.concatenate(
            [lane_tiles(gv[:, h * GLA_DV:(h + 1) * GLA_DV], h, jnp.zeros((span, GLA_DV), BF16), GLA_HEADS)
             for h in range(GLA_HEADS)], axis=0)
        s_bd = jnp.concatenate(
            [lane_tiles(g_state[h].astype(BF16), h // 2, zero_s, 2) for h in range(GLA_HEADS)],
            axis=0)
        o_all = _dot(a, v_bd) + _dot_nt(q_state[rows], s_bd)
        decay = jnp.exp2(b[(ch + 1) * span - 1:(ch + 1) * span, :])
        for h in range(GLA_HEADS):
            tile = h // 2
            upd = _dot_tn(gv[:, h * GLA_DV:(h + 1) * GLA_DV], k_state[rows, tile * LANES:(tile + 1) * LANES])
            upd = jnp.where(low_half_s if h % 2 == 0 else ~low_half_s, upd, 0.0)
            g_state[h] = g_state[h] * decay[:, tile * LANES:(tile + 1) * LANES] + upd
        for h in range(GLA_HEADS):
            o = o_all[:, h * GLA_DV:(h + 1) * GLA_DV]
            o = o * lax.rsqrt(jnp.mean(o * o, axis=-1, keepdims=True) + LN_EPS)
            gg = slab[orow, OFF_GG + h * GLA_DV:OFF_GG + (h + 1) * GLA_DV].astype(F32)
            o = o * gnw_ref[:, h * GLA_DV:(h + 1) * GLA_DV] * _silu(gg)
            out_ref[0, orow, RET_WIDTH + h * GLA_DV:RET_WIDTH + (h + 1) * GLA_DV] = o.astype(out_ref.dtype)
        yield


def _interleave(gen_a, n_a, gen_b, n_b):
    done_a = done_b = 0
    while done_a < n_a or done_b < n_b:
        if done_b >= n_b or (done_a < n_a and (done_a + 0.5) * n_b <= (done_b + 0.5) * n_a):
            next(gen_a)
            done_a += 1
        else:
            next(gen_b)
            done_b += 1
    for g in (gen_a, gen_b):
        assert next(g, "done") == "done", "piece count mismatch"


def _projmix_kernel(x_ref, mod_ref, cos_ref, sin_ref, w_ref, gw_ref, gb_ref,
                    dmask_ref, qdec_ref, kdec_ref, csum_ref, rnw_ref, gnw_ref,
                    wo_ref, w1_ref, w2_ref,
                    out_ref, wo_bf_ref, w1_bf_ref, w2_bf_ref,
                    w_bf, w_tail, slab2, la2, bounded_ref, r_state, g_state,
                    *, tb, blocks_per_seq, n_blocks, block_decay):
    j = pl.program_id(0)
    n_proj = 2 + SLAB // MXU_WIDTH
    n_mix = {bounded: (tb // MIX_ROWS) * (RET_HEADS + 1 + MIX_ROWS // (BOUNDED_SPAN if bounded else CHUNK))
             for bounded in (True, False)}

    @pl.when(j == 0)
    def _():
        for g in range(0, SLAB, MXU_WIDTH):
            w_bf[:, g:g + MXU_WIDTH] = w_ref[g:g + MXU_WIDTH, :].T.astype(BF16)
        w_tail[...] = w_ref[SLAB:SLAB + GATE_RANK, :].astype(BF16)
        bounded_ref[1] = 0

    wo_bf_ref[...] = wo_ref[...].astype(BF16)
    w1_bf_ref[...] = w1_ref[...].astype(BF16)
    w2_bf_ref[...] = w2_ref[...].astype(BF16)

    @pl.when((j - 1) % blocks_per_seq == 0)
    def _():
        r_state[...] = jnp.zeros_like(r_state)
        g_state[...] = jnp.zeros_like(g_state)

    def project(buf):
        return _proj_stage(x_ref, mod_ref, cos_ref, sin_ref, w_bf, w_tail, gw_ref, gb_ref,
                           slab2.at[buf], la2.at[buf], bounded_ref, buf)

    def mix(buf, bounded):
        for r0 in range(0, tb, MIX_ROWS):
            yield from _mix_stage(slab2.at[buf], la2.at[buf], r0, dmask_ref, qdec_ref, kdec_ref,
                                  csum_ref, rnw_ref, gnw_ref, out_ref, r_state, g_state, block_decay,
                                  bounded)

    @pl.when(j == 0)
    def _():
        for _ in project(0):
            pass

    steady = (j > 0) & (j < n_blocks)
    bounded = bounded_ref[1 - j % 2] == 1
    for parity in (0, 1):
        @pl.when(steady & bounded & (j % 2 == parity))
        def _():
            _interleave(project(parity), n_proj, mix(1 - parity, True), n_mix[True])

    @pl.when(steady & jnp.logical_not(bounded))
    def _():
        _interleave(project(j % 2), n_proj, mix(1 - j % 2, False), n_mix[False])

    @pl.when(j == n_blocks)
    def _():
        for _ in mix((n_blocks - 1) % 2, False):
            pass


def _layer_row(v):
    return v.reshape(v.shape[0], 1, v.shape[1])


def _projmix(x, mod, cos_t, sin_t, w_in, gate_w, gate_b, ret_norm_w, gla_norm_w,
             w_out, w_ff1, w_ff2, layer, tb):
    bsz, s, d = x.shape
    d_ff = w_ff1.shape[2]
    bps = s // tb
    nblk = bsz * bps
    dmask, qdec, kdec, block_decay = _retention_tables(MIX_ROWS)
    csum = _cumsum_matrix(MIX_ROWS)
    const2 = lambda j: (0, 0)
    const3 = lambda j: (0, 0, 0)
    of_layer = lambda j: (layer, 0, 0)
    proj_blk = lambda j: jnp.minimum(j, nblk - 1)
    mix_blk = lambda j: jnp.maximum(j - 1, 0)
    single = dict(pipeline_mode=pl.Buffered(1))
    wo_rows, w1_rows, w2_rows = d // nblk, d // nblk, d_ff // nblk
    return pl.pallas_call(
        functools.partial(_projmix_kernel, tb=tb, blocks_per_seq=bps, n_blocks=nblk,
                          block_decay=block_decay),
        grid=(nblk + 1,),
        in_specs=[pl.BlockSpec((1, tb, d), lambda j: (proj_blk(j) // bps, proj_blk(j) % bps, 0)),
                  pl.BlockSpec((1, 6, d), lambda j: (proj_blk(j) // bps, 0, 0)),
                  pl.BlockSpec((tb, RET_D), lambda j: (proj_blk(j) % bps, 0)),
                  pl.BlockSpec((tb, RET_D), lambda j: (proj_blk(j) % bps, 0)),
                  pl.BlockSpec((None, SLAB + GATE_RANK, d), of_layer, **single),
                  pl.BlockSpec((None, GATE_RANK, GLA_KEY_WIDTH), of_layer, **single),
                  pl.BlockSpec((None, 1, GLA_KEY_WIDTH), of_layer, **single),
                  pl.BlockSpec(dmask.shape, const3, **single),
                  pl.BlockSpec(qdec.shape, const3, **single),
                  pl.BlockSpec(kdec.shape, const3, **single),
                  pl.BlockSpec(csum.shape, const2, **single),
                  pl.BlockSpec((None, 1, RET_WIDTH), of_layer, **single),
                  pl.BlockSpec((None, 1, GLA_WIDTH), of_layer, **single),
                  pl.BlockSpec((None, wo_rows, d), lambda j: (layer, proj_blk(j), 0)),
                  pl.BlockSpec((None, w1_rows, d_ff), lambda j: (layer, proj_blk(j), 0)),
                  pl.BlockSpec((None, w2_rows, d), lambda j: (layer, proj_blk(j), 0))],
        out_specs=[pl.BlockSpec((1, tb, D_MIX), lambda j: (mix_blk(j) // bps, mix_blk(j) % bps, 0)),
                   pl.BlockSpec((wo_rows, d), lambda j: (proj_blk(j), 0)),
                   pl.BlockSpec((w1_rows, d_ff), lambda j: (proj_blk(j), 0)),
                   pl.BlockSpec((w2_rows, d), lambda j: (proj_blk(j), 0))],
        out_shape=[jax.ShapeDtypeStruct((bsz, s, D_MIX), BF16),
                   jax.ShapeDtypeStruct((d, d), BF16),
                   jax.ShapeDtypeStruct((d, d_ff), BF16),
                   jax.ShapeDtypeStruct((d_ff, d), BF16)],
        scratch_shapes=[pltpu.VMEM((d, SLAB), BF16), pltpu.VMEM((GATE_RANK, d), BF16),
                        pltpu.VMEM((2, tb, SLAB), BF16), pltpu.VMEM((2, tb, GLA_KEY_WIDTH), F32),
                        pltpu.SMEM((2,), jnp.int32),
                        pltpu.VMEM((RET_HEADS, RET_D, RET_D), F32),
                        pltpu.VMEM((GLA_HEADS, GLA_DV, LANES), F32)],
        compiler_params=pltpu.CompilerParams(
            dimension_semantics=("arbitrary",), vmem_limit_bytes=VMEM_LIMIT),
        name="projmix",
    )(x, mod, cos_t, sin_t, jnp.swapaxes(w_in, 1, 2), gate_w, _layer_row(gate_b),
      jnp.asarray(dmask), jnp.asarray(qdec), jnp.asarray(kdec), jnp.asarray(csum, BF16),
      _layer_row(ret_norm_w), _layer_row(gla_norm_w), w_out, w_ff1, w_ff2)


def _ffn_kernel(x_ref, mix_ref, mod_ref, wo_ref, w1_ref, w2_ref, ln1w_ref, ln1b_ref,
                ln2w_ref, ln2b_ref, out_ref, h_ref, *, alpha):
    gate1 = mod_ref[0, 2:3, :]
    shift2 = mod_ref[0, 3:4, :]
    scale2 = mod_ref[0, 4:5, :]
    gate2 = mod_ref[0, 5:6, :]
    d_ff = w1_ref.shape[1]
    n_groups = 4
    group = d_ff // n_groups
    tb = x_ref.shape[1]
    halves = (slice(0, tb // 2), slice(tb // 2, tb))

    def chain(rows, wait):
        m = _dot(mix_ref[0, rows, :], wo_ref[...])
        yield
        x1 = (_ln(alpha * x_ref[0, rows, :] + gate1 * m, wait.get("ln1")) * ln1w_ref[...]
              + ln1b_ref[...])
        u2f = _ln(x1) * (1.0 + scale2) + shift2
        wait["ln1_done"] = u2f[u2f.shape[0] - 8:, u2f.shape[1] - LANES:]
        u2 = u2f.astype(BF16)
        yield
        for j in range(n_groups):
            hj = jnp.maximum(_dot(u2, w1_ref[:, j * group:(j + 1) * group]), 0.0)
            h_ref[rows, j * group:(j + 1) * group] = (hj * hj).astype(BF16)
            yield
        f = _dot(h_ref[rows, :], w2_ref[...])
        yield
        out_ref[0, rows, :] = _ln(alpha * x1 + gate2 * f) * ln2w_ref[...] + ln2b_ref[...]
        yield

    wait_a, wait_b = {}, {}
    half_a, half_b = chain(halves[0], wait_a), chain(halves[1], wait_b)
    next(half_a)
    next(half_b)
    next(half_a)
    wait_b["ln1"] = wait_a["ln1_done"]
    n_rest = n_groups + 2
    _interleave(half_a, n_rest, half_b, n_rest + 1)


def _ffn(x, mixed, mod, w_out, w1, w2, ln1_w, ln1_b, ln2_w, ln2_b, layer, alpha, tb):
    bsz, s, d = x.shape
    nb = s // tb
    d_ff = w1.shape[1]
    const2 = lambda b, i: (0, 0)
    of_layer = lambda b, i: (layer, 0, 0)
    single = dict(pipeline_mode=pl.Buffered(1))
    return pl.pallas_call(
        functools.partial(_ffn_kernel, alpha=alpha),
        grid=(bsz, nb),
        in_specs=[pl.BlockSpec((1, tb, d), lambda b, i: (b, i, 0)),
                  pl.BlockSpec((1, tb, d), lambda b, i: (b, i, 0)),
                  pl.BlockSpec((1, 6, d), lambda b, i: (b, 0, 0)),
                  pl.BlockSpec((d, d), const2, **single),
                  pl.BlockSpec((d, d_ff), const2, **single),
                  pl.BlockSpec((d_ff, d), const2, **single),
                  pl.BlockSpec((None, 1, d), of_layer, **single),
                  pl.BlockSpec((None, 1, d), of_layer, **single),
                  pl.BlockSpec((None, 1, d), of_layer, **single),
                  pl.BlockSpec((None, 1, d), of_layer, **single)],
        out_specs=pl.BlockSpec((1, tb, d), lambda b, i: (b, i, 0)),
        out_shape=jax.ShapeDtypeStruct((bsz, s, d), x.dtype),
        scratch_shapes=[pltpu.VMEM((tb, d_ff), BF16)],
        compiler_params=pltpu.CompilerParams(
            dimension_semantics=("parallel", "parallel"), vmem_limit_bytes=VMEM_LIMIT),
        name="ffn",
    )(x, mixed, mod, w_out, w1, w2, _layer_row(ln1_w), _layer_row(ln1_b), _layer_row(ln2_w),
      _layer_row(ln2_b))


def kernel(x, c, w_ada, b_ada, w_in, ret_norm_w, gla_gate_w, gla_gate_b, gla_norm_w,
           w_out, ln1_w, ln1_b, w_ff1, w_ff2, ln2_w, ln2_b):
    depth = w_in.shape[0]
    bsz, s, d = x.shape
    alpha = (2.0 * depth) ** 0.25
    tb = min(PROJMIX_ROWS, s)
    cos_t, sin_t = (jnp.asarray(t) for t in _rotary_tables(s))
    for l in range(depth):
        mod = _adaln(c, w_ada, b_ada, l)
        mixed, wo_bf, w1_bf, w2_bf = _projmix(x, mod, cos_t, sin_t, w_in, gla_gate_w, gla_gate_b,
                                              ret_norm_w, gla_norm_w, w_out, w_ff1, w_ff2, l, tb)
        x = _ffn(x, mixed, mod, wo_bf, w1_bf, w2_bf, ln1_w, ln1_b, ln2_w, ln2_b, l, alpha,
                 min(FFN_ROWS, s))
    return x
```

```python
import functools
import math

import numpy as np
import jax
import jax.numpy as jnp
from jax import lax
from jax.experimental import pallas as pl
from jax.experimental.pallas import tpu as pltpu

CHUNK = 64
RET_HEADS = 4
RET_D = 128
RET_WIDTH = RET_HEADS * RET_D
GLA_HEADS = 4
GLA_DK = 64
GLA_DV = 128
GLA_KEY_WIDTH = GLA_HEADS * GLA_DK
GLA_WIDTH = GLA_HEADS * GLA_DV
D_MIX = RET_WIDTH + GLA_WIDTH
GATE_RANK = 16
GATE_TAU = 16.0
ROPE_BASE = 10000.0
LN_EPS = 1e-5
LANES = 128
MXU_WIDTH = 256
PROJMIX_ROWS = 512
FFN_ROWS = 1024
SLAB = 4 * RET_WIDTH + 2 * GLA_KEY_WIDTH + 2 * GLA_WIDTH
OFF_RQ, OFF_RK, OFF_RV, OFF_RG = 0, 512, 1024, 1536
OFF_GQ, OFF_GK, OFF_GV, OFF_GG = 2048, 2304, 2560, 3072
MIX_ROWS = 256
BOUNDED_LOG2_DECAY = 96.0
BOUNDED_SPAN = 2 * CHUNK
VMEM_LIMIT = 56 * 1024 * 1024
LOG2E = math.log2(math.e)

BF16 = jnp.bfloat16
F32 = jnp.float32


def _dot(a, b):
    return jnp.dot(a, b, preferred_element_type=F32)


def _dot_nt(a, b):
    return lax.dot_general(a, b, (((1,), (1,)), ((), ())), preferred_element_type=F32)


def _dot_tn(a, b):
    return lax.dot_general(a, b, (((0,), (0,)), ((), ())), preferred_element_type=F32)


def _zero_of(v):
    bits = lax.shift_right_logical(pltpu.bitcast(v, jnp.uint32), jnp.uint32(32))
    return pltpu.bitcast(bits, F32)


def _ln(x, not_before=None):
    mu = jnp.mean(x, axis=-1, keepdims=True)
    if not_before is not None:
        mu = mu + _zero_of(not_before)[0:1, 0:1]
    xc = x - mu
    var = jnp.mean(xc * xc, axis=-1, keepdims=True)
    return xc * lax.rsqrt(var + LN_EPS)


def _silu(x):
    return x * (1.0 / (1.0 + jnp.exp(-x)))


def _adaln_kernel(c_ref, w_ref, b_ref, o_ref, *, per_step):
    c = c_ref[...]
    d = c.shape[1]
    val = _dot(_silu(c).astype(BF16), w_ref[...].astype(BF16)) + b_ref[...]
    for t in range(per_step):
        o_ref[:, pl.ds(pl.program_id(0) * per_step + t, 1), :] = val[:, None, t * d:(t + 1) * d]


def _adaln(c, w_ada, b_ada, layer):
    bsz, d = c.shape
    depth, _, n = w_ada.shape
    per_step = 3
    return pl.pallas_call(
        functools.partial(_adaln_kernel, per_step=per_step),
        grid=(n // (per_step * d),),
        in_specs=[pl.BlockSpec((bsz, d), lambda j: (0, 0)),
                  pl.BlockSpec((None, d, per_step * d), lambda j: (layer, 0, j)),
                  pl.BlockSpec((None, 1, per_step * d), lambda j: (layer, 0, j))],
        out_specs=pl.BlockSpec((bsz, n // d, d), lambda j: (0, 0, 0)),
        out_shape=jax.ShapeDtypeStruct((bsz, n // d, d), F32),
        compiler_params=pltpu.CompilerParams(vmem_limit_bytes=VMEM_LIMIT),
        name="adaln",
    )(c, w_ada, b_ada.reshape(depth, 1, n))


def _rotary_tables(s):
    half = RET_D // 2
    inv = 1.0 / (ROPE_BASE ** (np.arange(half, dtype=np.float64) / (half - 1)))
    ang = np.arange(s, dtype=np.float64)[:, None] * inv[None, :]
    cos = np.concatenate([np.cos(ang), np.cos(ang)], axis=1)
    sin = np.concatenate([-np.sin(ang), np.sin(ang)], axis=1)
    return cos.astype(np.float32), sin.astype(np.float32)


def _retention_tables(tb):
    log_gamma = np.log(1.0 - 2.0 ** (-5.0 - np.arange(RET_HEADS, dtype=np.float64)))
    idx = np.arange(tb)
    dist = np.abs(idx[:, None] - idx[None, :])
    visible = (idx[None, :] // CHUNK) <= (idx[:, None] // CHUNK)
    scale = RET_D ** -0.5
    dmask = np.exp(log_gamma[:, None, None] * dist[None]) * visible[None] * scale
    qdec = np.exp(log_gamma[:, None] * (idx[None, :] + 1.0)) * scale
    kdec = np.exp(log_gamma[:, None] * (tb - 1.0 - idx[None, :]))
    qdec = np.broadcast_to(qdec[:, :, None], (RET_HEADS, tb, RET_D))
    kdec = np.broadcast_to(kdec[:, :, None], (RET_HEADS, tb, RET_D))
    block_decay = [float(np.exp(lg * tb)) for lg in log_gamma]
    return (dmask.astype(np.float32), np.ascontiguousarray(qdec, np.float32),
            np.ascontiguousarray(kdec, np.float32), block_decay)


def _cumsum_matrix(tb):
    t = np.arange(tb)
    same_chunk = (t[:, None] // CHUNK) == (t[None, :] // CHUNK)
    return (same_chunk & (t[None, :] <= t[:, None])).astype(np.float32)


def _proj_stage(x_ref, mod_ref, cos_ref, sin_ref, w_ref, wtail_ref, gw_ref, gb_ref, slab, la_out,
                bounded_ref, slot):
    x = x_ref[0]
    shift1 = mod_ref[0, 0:1, :]
    scale1 = mod_ref[0, 1:2, :]
    ub = (_ln(x) * (1.0 + scale1) + shift1).astype(BF16)
    yield
    cos = cos_ref[...]
    sin = sin_ref[...]
    group = MXU_WIDTH
    for j in range(SLAB // group):
        p = _dot(ub, w_ref[:, j * group:(j + 1) * group])
        if j * group < OFF_RV:
            for h in range(group // RET_D):
                xh = p[:, h * RET_D:(h + 1) * RET_D]
                rot = xh * cos + pltpu.roll(xh, RET_D // 2, 1) * sin
                slab[:, j * group + h * RET_D:j * group + (h + 1) * RET_D] = rot.astype(BF16)
        else:
            slab[:, j * group:(j + 1) * group] = p.astype(BF16)
        yield
    glr = _dot_nt(ub, wtail_ref[...])
    logit = _dot(glr.astype(BF16), gw_ref[...].astype(BF16)) + gb_ref[...]
    log_sig = jnp.minimum(logit, 0.0) - jnp.log(1.0 + jnp.exp(-jnp.abs(logit)))
    la = log_sig * (LOG2E / GATE_TAU)
    la_out[...] = la
    bounded_ref[slot] = (jnp.min(la) * BOUNDED_SPAN >= -BOUNDED_LOG2_DECAY).astype(jnp.int32)
    yield


def _row_of_block(b, block, row):
    n = b.shape[0] // block
    b3 = b.reshape(n, block, b.shape[1])
    return jnp.broadcast_to(b3[:, row:row + 1, :], b3.shape).reshape(b.shape)


def _mix_stage(slab, la_ref, r0, dmask_ref, qdec_ref, kdec_ref, csum_ref, rnw_ref, gnw_ref,
               out_ref, r_state, g_state, block_decay, bounded):
    tb = MIX_ROWS
    blk = slice(r0, r0 + tb)
    for h in range(RET_HEADS):
        q = slab[blk, OFF_RQ + h * RET_D:OFF_RQ + (h + 1) * RET_D]
        k = slab[blk, OFF_RK + h * RET_D:OFF_RK + (h + 1) * RET_D]
        v = slab[blk, OFF_RV + h * RET_D:OFF_RV + (h + 1) * RET_D]
        g = slab[blk, OFF_RG + h * RET_D:OFF_RG + (h + 1) * RET_D].astype(F32)
        p = (_dot_nt(q, k) * dmask_ref[h]).astype(BF16)
        r_prev = r_state[h]
        o = _dot(p, v) + _dot(q, r_prev.astype(BF16)) * qdec_ref[h]
        kd = (k.astype(F32) * kdec_ref[h]).astype(BF16)
        r_state[h] = block_decay[h] * r_prev + _dot_tn(kd, v)
        o = o - jnp.mean(o, axis=-1, keepdims=True)
        o = o * lax.rsqrt(jnp.mean(o * o, axis=-1, keepdims=True) + LN_EPS)
        o = o * rnw_ref[:, h * RET_D:(h + 1) * RET_D] * _silu(g)
        out_ref[0, blk, h * RET_D:(h + 1) * RET_D] = o.astype(out_ref.dtype)
        yield

    la = la_ref[blk, :]
    la_hi = la.astype(BF16)
    la_lo = (la - la_hi.astype(F32)).astype(BF16)
    csum = csum_ref[...]
    b = _dot(csum, la_hi) + _dot(csum, la_lo)
    gq = slab[blk, OFF_GQ:OFF_GQ + GLA_KEY_WIDTH] * jnp.asarray(GLA_DK ** -0.5, BF16)
    gk = slab[blk, OFF_GK:OFF_GK + GLA_KEY_WIDTH]
    row = lax.broadcasted_iota(jnp.int32, (tb, GLA_KEY_WIDTH), 0)
    span = BOUNDED_SPAN if bounded else CHUNK
    if bounded:
        for c in range(1, span // CHUNK):
            carry = _row_of_block(b, span, c * CHUNK - 1)
            b = b + jnp.where((row & (span - 1)) >= c * CHUNK, carry, 0.0)
    q_state = gq * jnp.exp2(b).astype(BF16)
    k_state = gk * jnp.exp2(_row_of_block(b, span, span - 1) - b).astype(BF16)
    if bounded:
        grow = jnp.exp2(-b).astype(BF16)
        q_lvl = [q_state, gq * grow]
        phis = [grow, jnp.exp2(b).astype(BF16)]
    else:
        psis = [-jnp.abs(b - _row_of_block(b, 2 * c, c - 1)) for c in (32, 16, 8, 4)]
        la_next = pltpu.roll(la, tb - 1, 0)
        la_prev = pltpu.roll(la, 1, 0)
        r4 = row & 3
        psis.append(jnp.where(r4 == 0, la_next,
                              jnp.where(r4 == 1, 0.0, jnp.where(r4 == 2, la, la + la_prev))))
        psis.append(jnp.where((row & 1) == 1, la, 0.0))
        phis = [jnp.exp2(p).astype(BF16) for p in psis]
        q_lvl = [gq] + [gq * phi for phi in phis]
        phis = [None] + phis
    yield

    lane = lax.broadcasted_iota(jnp.int32, (span, LANES), 1)
    low_half = lane < GLA_DK
    zero_k = jnp.zeros((span, LANES), BF16)
    zero_s = jnp.zeros((GLA_DV, LANES), BF16)
    t_idx = lax.broadcasted_iota(jnp.int32, (span, GLA_HEADS * span), 0)
    s_idx = lax.broadcasted_iota(jnp.int32, (span, GLA_HEADS * span), 1) & (span - 1)
    t_xor_s = t_idx ^ s_idx
    lane_s = lax.broadcasted_iota(jnp.int32, (GLA_DV, LANES), 1)
    low_half_s = lane_s < GLA_DK

    def lane_tiles(piece, tile, zero, n_tiles):
        return jnp.concatenate([piece if j == tile else zero for j in range(n_tiles)], axis=1)

    for ch in range(tb // span):
        rows = slice(ch * span, (ch + 1) * span)
        orow = slice(r0 + ch * span, r0 + (ch + 1) * span)
        k_heads = [jnp.where(low_half if h % 2 == 0 else ~low_half,
                             gk[rows, (h // 2) * LANES:(h // 2 + 1) * LANES], zero_k)
                   for h in range(GLA_HEADS)]
        a_lvl = []
        for lvl in range(len(q_lvl)):
            pieces = []
            for h in range(GLA_HEADS):
                tile = h // 2
                kh = k_heads[h]
                if phis[lvl] is not None:
                    kh = kh * phis[lvl][rows, tile * LANES:(tile + 1) * LANES]
                pieces.append(lane_tiles(kh, tile, zero_k, 2))
            k_stack = jnp.concatenate(pieces, axis=0)
            a_lvl.append(_dot_nt(q_lvl[lvl][rows], k_stack))
        if bounded:
            same_chunk = (t_idx // CHUNK) == (s_idx // CHUNK)
            a = jnp.where(t_idx >= s_idx, a_lvl[0], jnp.where(same_chunk, a_lvl[1], 0.0))
        else:
            a = a_lvl[0]
            for i, c in enumerate((1, 2, 4, 8, 16, 32)):
                a = jnp.where(t_xor_s >= c, a_lvl[len(a_lvl) - 1 - i], a)
        a = a.astype(BF16)
        gv = slab[orow, OFF_GV:OFF_GV + GLA_WIDTH]
        v_bd = jnp.concatenate(
            [lane_tiles(gv[:, h * GLA_DV:(h + 1) * GLA_DV], h, jnp.zeros((span, GLA_DV), BF16), GLA_HEADS)
             for h in range(GLA_HEADS)], axis=0)
        s_bd = jnp.concatenate(
            [lane_tiles(g_state[h].astype(BF16), h // 2, zero_s, 2) for h in range(GLA_HEADS)],
            axis=0)
        o_all = _dot(a, v_bd) + _dot_nt(q_state[rows], s_bd)
        decay = jnp.exp2(b[(ch + 1) * span - 1:(ch + 1) * span, :])
        for h in range(GLA_HEADS):
            tile = h // 2
            upd = _dot_tn(gv[:, h * GLA_DV:(h + 1) * GLA_DV], k_state[rows, tile * LANES:(tile + 1) * LANES])
            upd = jnp.where(low_half_s if h % 2 == 0 else ~low_half_s, upd, 0.0)
            g_state[h] = g_state[h] * decay[:, tile * LANES:(tile + 1) * LANES] + upd
        for h in range(GLA_HEADS):
            o = o_all[:, h * GLA_DV:(h + 1) * GLA_DV]
            o = o * lax.rsqrt(jnp.mean(o * o, axis=-1, keepdims=True) + LN_EPS)
            gg = slab[orow, OFF_GG + h * GLA_DV:OFF_GG + (h + 1) * GLA_DV].astype(F32)
            o = o * gnw_ref[:, h * GLA_DV:(h + 1) * GLA_DV] * _silu(gg)
            out_ref[0, orow, RET_WIDTH + h * GLA_DV:RET_WIDTH + (h + 1) * GLA_DV] = o.astype(out_ref.dtype)
        yield


def _interleave(gen_a, n_a, gen_b, n_b):
    done_a = done_b = 0
    while done_a < n_a or done_b < n_b:
        if done_b >= n_b or (done_a < n_a and (done_a + 0.5) * n_b <= (done_b + 0.5) * n_a):
            next(gen_a)
            done_a += 1
        else:
            next(gen_b)
            done_b += 1
    for g in (gen_a, gen_b):
        assert next(g, "done") == "done", "piece count mismatch"


def _projmix_kernel(x_ref, mod_ref, cos_ref, sin_ref, w_ref, gw_ref, gb_ref,
                    dmask_ref, qdec_ref, kdec_ref, csum_ref, rnw_ref, gnw_ref,
                    wo_ref, w1_ref, w2_ref,
                    out_ref, wo_bf_ref, w1_bf_ref, w2_bf_ref,
                    w_bf, w_tail, slab2, la2, bounded_ref, r_state, g_state,
                    *, tb, blocks_per_seq, n_blocks, block_decay):
    j = pl.program_id(0)
    n_proj = 2 + SLAB // MXU_WIDTH
    n_mix = {bounded: (tb // MIX_ROWS) * (RET_HEADS + 1 + MIX_ROWS // (BOUNDED_SPAN if bounded else CHUNK))
             for bounded in (True, False)}

    @pl.when(j == 0)
    def _():
        for g in range(0, SLAB, MXU_WIDTH):
            w_bf[:, g:g + MXU_WIDTH] = w_ref[g:g + MXU_WIDTH, :].T.astype(BF16)
        w_tail[...] = w_ref[SLAB:SLAB + GATE_RANK, :].astype(BF16)
        bounded_ref[1] = 0

    wo_bf_ref[...] = wo_ref[...].astype(BF16)
    w1_bf_ref[...] = w1_ref[...].astype(BF16)
    w2_bf_ref[...] = w2_ref[...].astype(BF16)

    @pl.when((j - 1) % blocks_per_seq == 0)
    def _():
        r_state[...] = jnp.zeros_like(r_state)
        g_state[...] = jnp.zeros_like(g_state)

    def project(buf):
        return _proj_stage(x_ref, mod_ref, cos_ref, sin_ref, w_bf, w_tail, gw_ref, gb_ref,
                           slab2.at[buf], la2.at[buf], bounded_ref, buf)

    def mix(buf, bounded):
        for r0 in range(0, tb, MIX_ROWS):
            yield from _mix_stage(slab2.at[buf], la2.at[buf], r0, dmask_ref, qdec_ref, kdec_ref,
                                  csum_ref, rnw_ref, gnw_ref, out_ref, r_state, g_state, block_decay,
                                  bounded)

    @pl.when(j == 0)
    def _():
        for _ in project(0):
            pass

    steady = (j > 0) & (j < n_blocks)
    bounded = bounded_ref[1 - j % 2] == 1
    for parity in (0, 1):
        @pl.when(steady & bounded & (j % 2 == parity))
        def _():
            _interleave(project(parity), n_proj, mix(1 - parity, True), n_mix[True])

    @pl.when(steady & jnp.logical_not(bounded))
    def _():
        _interleave(project(j % 2), n_proj, mix(1 - j % 2, False), n_mix[False])

    @pl.when(j == n_blocks)
    def _():
        for _ in mix((n_blocks - 1) % 2, False):
            pass


def _layer_row(v):
    return v.reshape(v.shape[0], 1, v.shape[1])


def _projmix(x, mod, cos_t, sin_t, w_in, gate_w, gate_b, ret_norm_w, gla_norm_w,
             w_out, w_ff1, w_ff2, layer, tb):
    bsz, s, d = x.shape
    d_ff = w_ff1.shape[2]
    bps = s // tb
    nblk = bsz * bps
    dmask, qdec, kdec, block_decay = _retention_tables(MIX_ROWS)
    csum = _cumsum_matrix(MIX_ROWS)
    const2 = lambda j: (0, 0)
    const3 = lambda j: (0, 0, 0)
    of_layer = lambda j: (layer, 0, 0)
    proj_blk = lambda j: jnp.minimum(j, nblk - 1)
    mix_blk = lambda j: jnp.maximum(j - 1, 0)
    single = dict(pipeline_mode=pl.Buffered(1))
    wo_rows, w1_rows, w2_rows = d // nblk, d // nblk, d_ff // nblk
    return pl.pallas_call(
        functools.partial(_projmix_kernel, tb=tb, blocks_per_seq=bps, n_blocks=nblk,
                          block_decay=block_decay),
        grid=(nblk + 1,),
        in_specs=[pl.BlockSpec((1, tb, d), lambda j: (proj_blk(j) // bps, proj_blk(j) % bps, 0)),
                  pl.BlockSpec((1, 6, d), lambda j: (proj_blk(j) // bps, 0, 0)),
                  pl.BlockSpec((tb, RET_D), lambda j: (proj_blk(j) % bps, 0)),
                  pl.BlockSpec((tb, RET_D), lambda j: (proj_blk(j) % bps, 0)),
                  pl.BlockSpec((None, SLAB + GATE_RANK, d), of_layer, **single),
                  pl.BlockSpec((None, GATE_RANK, GLA_KEY_WIDTH), of_layer, **single),
                  pl.BlockSpec((None, 1, GLA_KEY_WIDTH), of_layer, **single),
                  pl.BlockSpec(dmask.shape, const3, **single),
                  pl.BlockSpec(qdec.shape, const3, **single),
                  pl.BlockSpec(kdec.shape, const3, **single),
                  pl.BlockSpec(csum.shape, const2, **single),
                  pl.BlockSpec((None, 1, RET_WIDTH), of_layer, **single),
                  pl.BlockSpec((None, 1, GLA_WIDTH), of_layer, **single),
                  pl.BlockSpec((None, wo_rows, d), lambda j: (layer, proj_blk(j), 0)),
                  pl.BlockSpec((None, w1_rows, d_ff), lambda j: (layer, proj_blk(j), 0)),
                  pl.BlockSpec((None, w2_rows, d), lambda j: (layer, proj_blk(j), 0))],
        out_specs=[pl.BlockSpec((1, tb, D_MIX), lambda j: (mix_blk(j) // bps, mix_blk(j) % bps, 0)),
                   pl.BlockSpec((wo_rows, d), lambda j: (proj_blk(j), 0)),
                   pl.BlockSpec((w1_rows, d_ff), lambda j: (proj_blk(j), 0)),
                   pl.BlockSpec((w2_rows, d), lambda j: (proj_blk(j), 0))],
        out_shape=[jax.ShapeDtypeStruct((bsz, s, D_MIX), BF16),
                   jax.ShapeDtypeStruct((d, d), BF16),
                   jax.ShapeDtypeStruct((d, d_ff), BF16),
                   jax.ShapeDtypeStruct((d_ff, d), BF16)],
        scratch_shapes=[pltpu.VMEM((d, SLAB), BF16), pltpu.VMEM((GATE_RANK, d), BF16),
                        pltpu.VMEM((2, tb, SLAB), BF16), pltpu.VMEM((2, tb, GLA_KEY_WIDTH), F32),
                        pltpu.SMEM((2,), jnp.int32),
                        pltpu.VMEM((RET_HEADS, RET_D, RET_D), F32),
                        pltpu.VMEM((GLA_HEADS, GLA_DV, LANES), F32)],
        compiler_params=pltpu.CompilerParams(
            dimension_semantics=("arbitrary",), vmem_limit_bytes=VMEM_LIMIT),
        name="projmix",
    )(x, mod, cos_t, sin_t, jnp.swapaxes(w_in, 1, 2), gate_w, _layer_row(gate_b),
      jnp.asarray(dmask), jnp.asarray(qdec), jnp.asarray(kdec), jnp.asarray(csum, BF16),
      _layer_row(ret_norm_w), _layer_row(gla_norm_w), w_out, w_ff1, w_ff2)


def _ffn_kernel(x_ref, mix_ref, mod_ref, wo_ref, w1_ref, w2_ref, ln1w_ref, ln1b_ref,
                ln2w_ref, ln2b_ref, out_ref, h_ref, *, alpha):
    gate1 = mod_ref[0, 2:3, :]
    shift2 = mod_ref[0, 3:4, :]
    scale2 = mod_ref[0, 4:5, :]
    gate2 = mod_ref[0, 5:6, :]
    d_ff = w1_ref.shape[1]
    n_groups = 4
    group = d_ff // n_groups
    tb = x_ref.shape[1]
    halves = (slice(0, tb // 2), slice(tb // 2, tb))

    def chain(rows, wait):
        m = _dot(mix_ref[0, rows, :], wo_ref[...])
        yield
        x1 = (_ln(alpha * x_ref[0, rows, :] + gate1 * m, wait.get("ln1")) * ln1w_ref[...]
              + ln1b_ref[...])
        u2f = _ln(x1) * (1.0 + scale2) + shift2
        wait["ln1_done"] = u2f[u2f.shape[0] - 8:, u2f.shape[1] - LANES:]
        u2 = u2f.astype(BF16)
        yield
        for j in range(n_groups):
            hj = jnp.maximum(_dot(u2, w1_ref[:, j * group:(j + 1) * group]), 0.0)
            h_ref[rows, j * group:(j + 1) * group] = (hj * hj).astype(BF16)
            yield
        f = _dot(h_ref[rows, :], w2_ref[...])
        yield
        out_ref[0, rows, :] = _ln(alpha * x1 + gate2 * f) * ln2w_ref[...] + ln2b_ref[...]
        yield

    wait_a, wait_b = {}, {}
    half_a, half_b = chain(halves[0], wait_a), chain(halves[1], wait_b)
    next(half_a)
    next(half_b)
    next(half_a)
    wait_b["ln1"] = wait_a["ln1_done"]
    n_rest = n_groups + 2
    _interleave(half_a, n_rest, half_b, n_rest + 1)


def _ffn(x, mixed, mod, w_out, w1, w2, ln1_w, ln1_b, ln2_w, ln2_b, layer, alpha, tb):
    bsz, s, d = x.shape
    nb = s // tb
    d_ff = w1.shape[1]
    const2 = lambda b, i: (0, 0)
    of_layer = lambda b, i: (layer, 0, 0)
    single = dict(pipeline_mode=pl.Buffered(1))
    return pl.pallas_call(
        functools.partial(_ffn_kernel, alpha=alpha),
        grid=(bsz, nb),
        in_specs=[pl.BlockSpec((1, tb, d), lambda b, i: (b, i, 0)),
                  pl.BlockSpec((1, tb, d), lambda b, i: (b, i, 0)),
                  pl.BlockSpec((1, 6, d), lambda b, i: (b, 0, 0)),
                  pl.BlockSpec((d, d), const2, **single),
                  pl.BlockSpec((d, d_ff), const2, **single),
                  pl.BlockSpec((d_ff, d), const2, **single),
                  pl.BlockSpec((None, 1, d), of_layer, **single),
                  pl.BlockSpec((None, 1, d), of_layer, **single),
                  pl.BlockSpec((None, 1, d), of_layer, **single),
                  pl.BlockSpec((None, 1, d), of_layer, **single)],
        out_specs=pl.BlockSpec((1, tb, d), lambda b, i: (b, i, 0)),
        out_shape=jax.ShapeDtypeStruct((bsz, s, d), x.dtype),
        scratch_shapes=[pltpu.VMEM((tb, d_ff), BF16)],
        compiler_params=pltpu.CompilerParams(
            dimension_semantics=("parallel", "parallel"), vmem_limit_bytes=VMEM_LIMIT),
        name="ffn",
    )(x, mixed, mod, w_out, w1, w2, _layer_row(ln1_w), _layer_row(ln1_b), _layer_row(ln2_w),
      _layer_row(ln2_b))


def kernel(x, c, w_ada, b_ada, w_in, ret_norm_w, gla_gate_w, gla_gate_b, gla_norm_w,
           w_out, ln1_w, ln1_b, w_ff1, w_ff2, ln2_w, ln2_b):
    depth = w_in.shape[0]
    bsz, s, d = x.shape
    alpha = (2.0 * depth) ** 0.25
    tb = min(PROJMIX_ROWS, s)
    cos_t, sin_t = (jnp.asarray(t) for t in _rotary_tables(s))
    for l in range(depth):
        mod = _adaln(c, w_ada, b_ada, l)
        mixed, wo_bf, w1_bf, w2_bf = _projmix(x, mod, cos_t, sin_t, w_in, gla_gate_w, gla_gate_b,
                                              ret_norm_w, gla_norm_w, w_out, w_ff1, w_ff2, l, tb)
        x = _ffn(x, mixed, mod, wo_bf, w1_bf, w2_bf, ln1_w, ln1_b, ln2_w, ln2_b, l, alpha,
                 min(FFN_ROWS, s))
    return x
```

```python
import functools
import math

import numpy as np
import jax
import jax.numpy as jnp
from jax import lax
from jax.experimental import pallas as pl
from jax.experimental.pallas import tpu as pltpu

CHUNK = 64
RET_HEADS = 4
RET_D = 128
RET_WIDTH = RET_HEADS * RET_D
GLA_HEADS = 4
GLA_DK = 64
GLA_DV = 128
GLA_KEY_WIDTH = GLA_HEADS * GLA_DK
GLA_WIDTH = GLA_HEADS * GLA_DV
D_MIX = RET_WIDTH + GLA_WIDTH
GATE_RANK = 16
GATE_TAU = 16.0
ROPE_BASE = 10000.0
LN_EPS = 1e-5
LANES = 128
MXU_WIDTH = 256
N_MOD = 6
N_EARLY_MOD = 2
PROJMIX_ROWS = 512
FFN_ROWS = 1024
SLAB = 4 * RET_WIDTH + 2 * GLA_KEY_WIDTH + 2 * GLA_WIDTH
OFF_RQ, OFF_RK, OFF_RV, OFF_RG = 0, 512, 1024, 1536
OFF_GQ, OFF_GK, OFF_GV, OFF_GG = 2048, 2304, 2560, 3072
MIX_ROWS = 256
BOUNDED_LOG2_DECAY = 96.0
BOUNDED_SPAN = 2 * CHUNK
VMEM_LIMIT = 56 * 1024 * 1024
LOG2E = math.log2(math.e)

BF16 = jnp.bfloat16
F32 = jnp.float32


def _dot(a, b):
    return jnp.dot(a, b, preferred_element_type=F32)


def _dot_nt(a, b):
    return lax.dot_general(a, b, (((1,), (1,)), ((), ())), preferred_element_type=F32)


def _dot_tn(a, b):
    return lax.dot_general(a, b, (((0,), (0,)), ((), ())), preferred_element_type=F32)


def _zero_of(v):
    bits = lax.shift_right_logical(pltpu.bitcast(v, jnp.uint32), jnp.uint32(32))
    return pltpu.bitcast(bits, F32)


def _ln(x, not_before=None):
    mu = jnp.mean(x, axis=-1, keepdims=True)
    if not_before is not None:
        mu = mu + _zero_of(not_before)[0:1, 0:1]
    xc = x - mu
    var = jnp.mean(xc * xc, axis=-1, keepdims=True)
    return xc * lax.rsqrt(var + LN_EPS)


def _silu(x):
    return x * (1.0 / (1.0 + jnp.exp(-x)))


def _adaln_kernel(c_ref, w_ref, b_ref, o_ref, *, per_step):
    c = c_ref[...]
    d = c.shape[1]
    val = _dot(_silu(c).astype(BF16), w_ref[...].astype(BF16)) + b_ref[...]
    for t in range(per_step):
        o_ref[:, pl.ds(pl.program_id(0) * per_step + t, 1), :] = val[:, None, t * d:(t + 1) * d]


def _adaln(c, w_ada, b_ada, layer):
    bsz, d = c.shape
    depth, _, n = w_ada.shape
    return pl.pallas_call(
        functools.partial(_adaln_kernel, per_step=N_EARLY_MOD),
        grid=(1,),
        in_specs=[pl.BlockSpec((bsz, d), lambda j: (0, 0)),
                  pl.BlockSpec((None, d, N_EARLY_MOD * d), lambda j: (layer, 0, 0)),
                  pl.BlockSpec((None, 1, N_EARLY_MOD * d), lambda j: (layer, 0, 0))],
        out_specs=pl.BlockSpec((bsz, N_EARLY_MOD, d), lambda j: (0, 0, 0)),
        out_shape=jax.ShapeDtypeStruct((bsz, N_EARLY_MOD, d), F32),
        compiler_params=pltpu.CompilerParams(vmem_limit_bytes=VMEM_LIMIT),
        name="adaln",
    )(c, w_ada, b_ada.reshape(depth, 1, n))


def _rotary_tables(s):
    half = RET_D // 2
    inv = 1.0 / (ROPE_BASE ** (np.arange(half, dtype=np.float64) / (half - 1)))
    ang = np.arange(s, dtype=np.float64)[:, None] * inv[None, :]
    cos = np.concatenate([np.cos(ang), np.cos(ang)], axis=1)
    sin = np.concatenate([-np.sin(ang), np.sin(ang)], axis=1)
    return cos.astype(np.float32), sin.astype(np.float32)


def _retention_tables(tb):
    log_gamma = np.log(1.0 - 2.0 ** (-5.0 - np.arange(RET_HEADS, dtype=np.float64)))
    idx = np.arange(tb)
    dist = np.abs(idx[:, None] - idx[None, :])
    visible = (idx[None, :] // CHUNK) <= (idx[:, None] // CHUNK)
    scale = RET_D ** -0.5
    dmask = np.exp(log_gamma[:, None, None] * dist[None]) * visible[None] * scale
    qdec = np.exp(log_gamma[:, None] * (idx[None, :] + 1.0)) * scale
    kdec = np.exp(log_gamma[:, None] * (tb - 1.0 - idx[None, :]))
    qdec = np.broadcast_to(qdec[:, :, None], (RET_HEADS, tb, RET_D))
    kdec = np.broadcast_to(kdec[:, :, None], (RET_HEADS, tb, RET_D))
    block_decay = [float(np.exp(lg * tb)) for lg in log_gamma]
    return (dmask.astype(np.float32), np.ascontiguousarray(qdec, np.float32),
            np.ascontiguousarray(kdec, np.float32), block_decay)


def _cumsum_matrix(tb):
    t = np.arange(tb)
    same_chunk = (t[:, None] // CHUNK) == (t[None, :] // CHUNK)
    return (same_chunk & (t[None, :] <= t[:, None])).astype(np.float32)


def _proj_stage(x_ref, mod_ref, cos_ref, sin_ref, w_ref, wtail_ref, gw_ref, gb_ref, slab, la_out,
                bounded_ref, slot):
    x = x_ref[0]
    shift1 = mod_ref[0, 0:1, :]
    scale1 = mod_ref[0, 1:2, :]
    ub = (_ln(x) * (1.0 + scale1) + shift1).astype(BF16)
    yield
    cos = cos_ref[...]
    sin = sin_ref[...]
    group = MXU_WIDTH
    for j in range(SLAB // group):
        p = _dot(ub, w_ref[:, j * group:(j + 1) * group])
        if j * group < OFF_RV:
            for h in range(group // RET_D):
                xh = p[:, h * RET_D:(h + 1) * RET_D]
                rot = xh * cos + pltpu.roll(xh, RET_D // 2, 1) * sin
                slab[:, j * group + h * RET_D:j * group + (h + 1) * RET_D] = rot.astype(BF16)
        else:
            slab[:, j * group:(j + 1) * group] = p.astype(BF16)
        yield
    glr = _dot_nt(ub, wtail_ref[...])
    logit = _dot(glr.astype(BF16), gw_ref[...].astype(BF16)) + gb_ref[...]
    log_sig = jnp.minimum(logit, 0.0) - jnp.log(1.0 + jnp.exp(-jnp.abs(logit)))
    la = log_sig * (LOG2E / GATE_TAU)
    la_out[...] = la
    bounded_ref[slot] = (jnp.min(la) * BOUNDED_SPAN >= -BOUNDED_LOG2_DECAY).astype(jnp.int32)
    yield


def _row_of_block(b, block, row):
    n = b.shape[0] // block
    b3 = b.reshape(n, block, b.shape[1])
    return jnp.broadcast_to(b3[:, row:row + 1, :], b3.shape).reshape(b.shape)


def _mix_stage(slab, la_ref, r0, dmask_ref, qdec_ref, kdec_ref, csum_ref, rnw_ref, gnw_ref,
               out_ref, r_state, g_state, block_decay, bounded):
    tb = MIX_ROWS
    blk = slice(r0, r0 + tb)
    for h in range(RET_HEADS):
        q = slab[blk, OFF_RQ + h * RET_D:OFF_RQ + (h + 1) * RET_D]
        k = slab[blk, OFF_RK + h * RET_D:OFF_RK + (h + 1) * RET_D]
        v = slab[blk, OFF_RV + h * RET_D:OFF_RV + (h + 1) * RET_D]
        g = slab[blk, OFF_RG + h * RET_D:OFF_RG + (h + 1) * RET_D].astype(F32)
        p = (_dot_nt(q, k) * dmask_ref[h]).astype(BF16)
        r_prev = r_state[h]
        o = _dot(p, v) + _dot(q, r_prev.astype(BF16)) * qdec_ref[h]
        kd = (k.astype(F32) * kdec_ref[h]).astype(BF16)
        r_state[h] = block_decay[h] * r_prev + _dot_tn(kd, v)
        o = o - jnp.mean(o, axis=-1, keepdims=True)
        o = o * lax.rsqrt(jnp.mean(o * o, axis=-1, keepdims=True) + LN_EPS)
        o = o * rnw_ref[:, h * RET_D:(h + 1) * RET_D] * _silu(g)
        out_ref[0, blk, h * RET_D:(h + 1) * RET_D] = o.astype(out_ref.dtype)
        yield

    la = la_ref[blk, :]
    la_hi = la.astype(BF16)
    la_lo = (la - la_hi.astype(F32)).astype(BF16)
    csum = csum_ref[...]
    b = _dot(csum, la_hi) + _dot(csum, la_lo)
    gq = slab[blk, OFF_GQ:OFF_GQ + GLA_KEY_WIDTH] * jnp.asarray(GLA_DK ** -0.5, BF16)
    gk = slab[blk, OFF_GK:OFF_GK + GLA_KEY_WIDTH]
    row = lax.broadcasted_iota(jnp.int32, (tb, GLA_KEY_WIDTH), 0)
    span = BOUNDED_SPAN if bounded else CHUNK
    if bounded:
        for c in range(1, span // CHUNK):
            carry = _row_of_block(b, span, c * CHUNK - 1)
            b = b + jnp.where((row & (span - 1)) >= c * CHUNK, carry, 0.0)
    q_state = gq * jnp.exp2(b).astype(BF16)
    k_state = gk * jnp.exp2(_row_of_block(b, span, span - 1) - b).astype(BF16)
    if bounded:
        grow = jnp.exp2(-b).astype(BF16)
        q_lvl = [q_state, gq * grow]
        phis = [grow, jnp.exp2(b).astype(BF16)]
    else:
        psis = [-jnp.abs(b - _row_of_block(b, 2 * c, c - 1)) for c in (32, 16, 8, 4)]
        la_next = pltpu.roll(la, tb - 1, 0)
        la_prev = pltpu.roll(la, 1, 0)
        r4 = row & 3
        psis.append(jnp.where(r4 == 0, la_next,
                              jnp.where(r4 == 1, 0.0, jnp.where(r4 == 2, la, la + la_prev))))
        psis.append(jnp.where((row & 1) == 1, la, 0.0))
        phis = [jnp.exp2(p).astype(BF16) for p in psis]
        q_lvl = [gq] + [gq * phi for phi in phis]
        phis = [None] + phis
    yield

    lane = lax.broadcasted_iota(jnp.int32, (span, LANES), 1)
    low_half = lane < GLA_DK
    zero_k = jnp.zeros((span, LANES), BF16)
    zero_s = jnp.zeros((GLA_DV, LANES), BF16)
    t_idx = lax.broadcasted_iota(jnp.int32, (span, GLA_HEADS * span), 0)
    s_idx = lax.broadcasted_iota(jnp.int32, (span, GLA_HEADS * span), 1) & (span - 1)
    t_xor_s = t_idx ^ s_idx
    lane_s = lax.broadcasted_iota(jnp.int32, (GLA_DV, LANES), 1)
    low_half_s = lane_s < GLA_DK

    def lane_tiles(piece, tile, zero, n_tiles):
        return jnp.concatenate([piece if j == tile else zero for j in range(n_tiles)], axis=1)

    for ch in range(tb // span):
        rows = slice(ch * span, (ch + 1) * span)
        orow = slice(r0 + ch * span, r0 + (ch + 1) * span)
        k_heads = [jnp.where(low_half if h % 2 == 0 else ~low_half,
                             gk[rows, (h // 2) * LANES:(h // 2 + 1) * LANES], zero_k)
                   for h in range(GLA_HEADS)]
        a_lvl = []
        for lvl in range(len(q_lvl)):
            pieces = []
            for h in range(GLA_HEADS):
                tile = h // 2
                kh = k_heads[h]
                if phis[lvl] is not None:
                    kh = kh * phis[lvl][rows, tile * LANES:(tile + 1) * LANES]
                pieces.append(lane_tiles(kh, tile, zero_k, 2))
            k_stack = jnp.concatenate(pieces, axis=0)
            a_lvl.append(_dot_nt(q_lvl[lvl][rows], k_stack))
        if bounded:
            same_chunk = (t_idx // CHUNK) == (s_idx // CHUNK)
            a = jnp.where(t_idx >= s_idx, a_lvl[0], jnp.where(same_chunk, a_lvl[1], 0.0))
        else:
            a = a_lvl[0]
            for i, c in enumerate((1, 2, 4, 8, 16, 32)):
                a = jnp.where(t_xor_s >= c, a_lvl[len(a_lvl) - 1 - i], a)
        a = a.astype(BF16)
        gv = slab[orow, OFF_GV:OFF_GV + GLA_WIDTH]
        v_bd = jnp.concatenate(
            [lane_tiles(gv[:, h * GLA_DV:(h + 1) * GLA_DV], h, jnp.zeros((span, GLA_DV), BF16), GLA_HEADS)
             for h in range(GLA_HEADS)], axis=0)
        s_bd = jnp.concatenate(
            [lane_tiles(g_state[h].astype(BF16), h // 2, zero_s, 2) for h in range(GLA_HEADS)],
            axis=0)
        o_all = _dot(a, v_bd) + _dot_nt(q_state[rows], s_bd)
        decay = jnp.exp2(b[(ch + 1) * span - 1:(ch + 1) * span, :])
        for h in range(GLA_HEADS):
            tile = h // 2
            upd = _dot_tn(gv[:, h * GLA_DV:(h + 1) * GLA_DV], k_state[rows, tile * LANES:(tile + 1) * LANES])
            upd = jnp.where(low_half_s if h % 2 == 0 else ~low_half_s, upd, 0.0)
            g_state[h] = g_state[h] * decay[:, tile * LANES:(tile + 1) * LANES] + upd
        for h in range(GLA_HEADS):
            o = o_all[:, h * GLA_DV:(h + 1) * GLA_DV]
            o = o * lax.rsqrt(jnp.mean(o * o, axis=-1, keepdims=True) + LN_EPS)
            gg = slab[orow, OFF_GG + h * GLA_DV:OFF_GG + (h + 1) * GLA_DV].astype(F32)
            o = o * gnw_ref[:, h * GLA_DV:(h + 1) * GLA_DV] * _silu(gg)
            out_ref[0, orow, RET_WIDTH + h * GLA_DV:RET_WIDTH + (h + 1) * GLA_DV] = o.astype(out_ref.dtype)
        yield


def _interleave(gen_a, n_a, gen_b, n_b):
    done_a = done_b = 0
    while done_a < n_a or done_b < n_b:
        if done_b >= n_b or (done_a < n_a and (done_a + 0.5) * n_b <= (done_b + 0.5) * n_a):
            next(gen_a)
            done_a += 1
        else:
            next(gen_b)
            done_b += 1
    for g in (gen_a, gen_b):
        assert next(g, "done") == "done", "piece count mismatch"


def _projmix_kernel(x_ref, mod_ref, cos_ref, sin_ref, w_ref, gw_ref, gb_ref,
                    dmask_ref, qdec_ref, kdec_ref, csum_ref, rnw_ref, gnw_ref,
                    wo_ref, w1_ref, w2_ref, c_ref, wa_ref, ba_ref,
                    out_ref, wo_bf_ref, w1_bf_ref, w2_bf_ref, late_mod_ref,
                    w_bf, w_tail, slab2, la2, bounded_ref, r_state, g_state,
                    *, tb, blocks_per_seq, n_blocks, block_decay):
    j = pl.program_id(0)
    n_proj = 2 + SLAB // MXU_WIDTH
    n_mix = {bounded: (tb // MIX_ROWS) * (RET_HEADS + 1 + MIX_ROWS // (BOUNDED_SPAN if bounded else CHUNK))
             for bounded in (True, False)}

    @pl.when(j == 0)
    def _():
        for g in range(0, SLAB, MXU_WIDTH):
            w_bf[:, g:g + MXU_WIDTH] = w_ref[g:g + MXU_WIDTH, :].T.astype(BF16)
        w_tail[...] = w_ref[SLAB:SLAB + GATE_RANK, :].astype(BF16)
        bounded_ref[1] = 0

    wo_bf_ref[...] = wo_ref[...].astype(BF16)
    w1_bf_ref[...] = w1_ref[...].astype(BF16)
    w2_bf_ref[...] = w2_ref[...].astype(BF16)
    late_mod_ref[...] = (_dot(_silu(c_ref[...]).astype(BF16), wa_ref[...].astype(BF16))
                         + ba_ref[...])

    @pl.when((j - 1) % blocks_per_seq == 0)
    def _():
        r_state[...] = jnp.zeros_like(r_state)
        g_state[...] = jnp.zeros_like(g_state)

    def project(buf):
        return _proj_stage(x_ref, mod_ref, cos_ref, sin_ref, w_bf, w_tail, gw_ref, gb_ref,
                           slab2.at[buf], la2.at[buf], bounded_ref, buf)

    def mix(buf, bounded):
        for r0 in range(0, tb, MIX_ROWS):
            yield from _mix_stage(slab2.at[buf], la2.at[buf], r0, dmask_ref, qdec_ref, kdec_ref,
                                  csum_ref, rnw_ref, gnw_ref, out_ref, r_state, g_state, block_decay,
                                  bounded)

    @pl.when(j == 0)
    def _():
        for _ in project(0):
            pass

    steady = (j > 0) & (j < n_blocks)
    bounded = bounded_ref[1 - j % 2] == 1
    for parity in (0, 1):
        @pl.when(steady & bounded & (j % 2 == parity))
        def _():
            _interleave(project(parity), n_proj, mix(1 - parity, True), n_mix[True])

    @pl.when(steady & jnp.logical_not(bounded))
    def _():
        _interleave(project(j % 2), n_proj, mix(1 - j % 2, False), n_mix[False])

    @pl.when(j == n_blocks)
    def _():
        for _ in mix((n_blocks - 1) % 2, False):
            pass


def _layer_row(v):
    return v.reshape(v.shape[0], 1, v.shape[1])


def _projmix(x, mod, cos_t, sin_t, w_in, gate_w, gate_b, ret_norm_w, gla_norm_w,
             w_out, w_ff1, w_ff2, c, w_ada, b_ada, layer, tb):
    bsz, s, d = x.shape
    d_ff = w_ff1.shape[2]
    bps = s // tb
    nblk = bsz * bps
    late_cols = (N_MOD - N_EARLY_MOD) * d // nblk
    late_first = N_EARLY_MOD * d // late_cols
    dmask, qdec, kdec, block_decay = _retention_tables(MIX_ROWS)
    csum = _cumsum_matrix(MIX_ROWS)
    const2 = lambda j: (0, 0)
    const3 = lambda j: (0, 0, 0)
    of_layer = lambda j: (layer, 0, 0)
    proj_blk = lambda j: jnp.minimum(j, nblk - 1)
    mix_blk = lambda j: jnp.maximum(j - 1, 0)
    single = dict(pipeline_mode=pl.Buffered(1))
    wo_rows, w1_rows, w2_rows = d // nblk, d // nblk, d_ff // nblk
    return pl.pallas_call(
        functools.partial(_projmix_kernel, tb=tb, blocks_per_seq=bps, n_blocks=nblk,
                          block_decay=block_decay),
        grid=(nblk + 1,),
        in_specs=[pl.BlockSpec((1, tb, d), lambda j: (proj_blk(j) // bps, proj_blk(j) % bps, 0)),
                  pl.BlockSpec((1, N_EARLY_MOD, d), lambda j: (proj_blk(j) // bps, 0, 0)),
                  pl.BlockSpec((tb, RET_D), lambda j: (proj_blk(j) % bps, 0)),
                  pl.BlockSpec((tb, RET_D), lambda j: (proj_blk(j) % bps, 0)),
                  pl.BlockSpec((None, SLAB + GATE_RANK, d), of_layer, **single),
                  pl.BlockSpec((None, GATE_RANK, GLA_KEY_WIDTH), of_layer, **single),
                  pl.BlockSpec((None, 1, GLA_KEY_WIDTH), of_layer, **single),
                  pl.BlockSpec(dmask.shape, const3, **single),
                  pl.BlockSpec(qdec.shape, const3, **single),
                  pl.BlockSpec(kdec.shape, const3, **single),
                  pl.BlockSpec(csum.shape, const2, **single),
                  pl.BlockSpec((None, 1, RET_WIDTH), of_layer, **single),
                  pl.BlockSpec((None, 1, GLA_WIDTH), of_layer, **single),
                  pl.BlockSpec((None, wo_rows, d), lambda j: (layer, proj_blk(j), 0)),
                  pl.BlockSpec((None, w1_rows, d_ff), lambda j: (layer, proj_blk(j), 0)),
                  pl.BlockSpec((None, w2_rows, d), lambda j: (layer, proj_blk(j), 0)),
                  pl.BlockSpec((bsz, d), const2, **single),
                  pl.BlockSpec((None, d, late_cols), lambda j: (layer, 0, late_first + proj_blk(j))),
                  pl.BlockSpec((None, 1, late_cols), lambda j: (layer, 0, late_first + proj_blk(j)))],
        out_specs=[pl.BlockSpec((1, tb, D_MIX), lambda j: (mix_blk(j) // bps, mix_blk(j) % bps, 0)),
                   pl.BlockSpec((wo_rows, d), lambda j: (proj_blk(j), 0)),
                   pl.BlockSpec((w1_rows, d_ff), lambda j: (proj_blk(j), 0)),
                   pl.BlockSpec((w2_rows, d), lambda j: (proj_blk(j), 0)),
                   pl.BlockSpec((bsz, late_cols), lambda j: (0, proj_blk(j)))],
        out_shape=[jax.ShapeDtypeStruct((bsz, s, D_MIX), BF16),
                   jax.ShapeDtypeStruct((d, d), BF16),
                   jax.ShapeDtypeStruct((d, d_ff), BF16),
                   jax.ShapeDtypeStruct((d_ff, d), BF16),
                   jax.ShapeDtypeStruct((bsz, (N_MOD - N_EARLY_MOD) * d), F32)],
        scratch_shapes=[pltpu.VMEM((d, SLAB), BF16), pltpu.VMEM((GATE_RANK, d), BF16),
                        pltpu.VMEM((2, tb, SLAB), BF16), pltpu.VMEM((2, tb, GLA_KEY_WIDTH), F32),
                        pltpu.SMEM((2,), jnp.int32),
                        pltpu.VMEM((RET_HEADS, RET_D, RET_D), F32),
                        pltpu.VMEM((GLA_HEADS, GLA_DV, LANES), F32)],
        compiler_params=pltpu.CompilerParams(
            dimension_semantics=("arbitrary",), vmem_limit_bytes=VMEM_LIMIT),
        name="projmix",
    )(x, mod, cos_t, sin_t, jnp.swapaxes(w_in, 1, 2), gate_w, _layer_row(gate_b),
      jnp.asarray(dmask), jnp.asarray(qdec), jnp.asarray(kdec), jnp.asarray(csum, BF16),
      _layer_row(ret_norm_w), _layer_row(gla_norm_w), w_out, w_ff1, w_ff2,
      c, w_ada, _layer_row(b_ada))


def _ffn_kernel(x_ref, mix_ref, mod_ref, wo_ref, w1_ref, w2_ref, ln1w_ref, ln1b_ref,
                ln2w_ref, ln2b_ref, out_ref, h_ref, *, alpha):
    seq = pl.ds(pl.program_id(0), 1)
    d = x_ref.shape[2]
    gate1 = mod_ref[seq, 0:d]
    shift2 = mod_ref[seq, d:2 * d]
    scale2 = mod_ref[seq, 2 * d:3 * d]
    gate2 = mod_ref[seq, 3 * d:4 * d]
    d_ff = w1_ref.shape[1]
    n_groups = 4
    group = d_ff // n_groups
    tb = x_ref.shape[1]
    halves = (slice(0, tb // 2), slice(tb // 2, tb))

    def chain(rows, wait):
        m = _dot(mix_ref[0, rows, :], wo_ref[...])
        yield
        x1 = (_ln(alpha * x_ref[0, rows, :] + gate1 * m, wait.get("ln1")) * ln1w_ref[...]
              + ln1b_ref[...])
        u2f = _ln(x1) * (1.0 + scale2) + shift2
        wait["ln1_done"] = u2f[u2f.shape[0] - 8:, u2f.shape[1] - LANES:]
        u2 = u2f.astype(BF16)
        yield
        for j in range(n_groups):
            hj = jnp.maximum(_dot(u2, w1_ref[:, j * group:(j + 1) * group]), 0.0)
            h_ref[rows, j * group:(j + 1) * group] = (hj * hj).astype(BF16)
            yield
        f = _dot(h_ref[rows, :], w2_ref[...])
        yield
        out_ref[0, rows, :] = _ln(alpha * x1 + gate2 * f) * ln2w_ref[...] + ln2b_ref[...]
        yield

    wait_a, wait_b = {}, {}
    half_a, half_b = chain(halves[0], wait_a), chain(halves[1], wait_b)
    next(half_a)
    next(half_b)
    next(half_a)
    wait_b["ln1"] = wait_a["ln1_done"]
    n_rest = n_groups + 2
    _interleave(half_a, n_rest, half_b, n_rest + 1)


def _ffn(x, mixed, mod, w_out, w1, w2, ln1_w, ln1_b, ln2_w, ln2_b, layer, alpha, tb):
    bsz, s, d = x.shape
    nb = s // tb
    d_ff = w1.shape[1]
    const2 = lambda b, i: (0, 0)
    of_layer = lambda b, i: (layer, 0, 0)
    single = dict(pipeline_mode=pl.Buffered(1))
    return pl.pallas_call(
        functools.partial(_ffn_kernel, alpha=alpha),
        grid=(bsz, nb),
        in_specs=[pl.BlockSpec((1, tb, d), lambda b, i: (b, i, 0)),
                  pl.BlockSpec((1, tb, d), lambda b, i: (b, i, 0)),
                  pl.BlockSpec(mod.shape, const2, **single),
                  pl.BlockSpec((d, d), const2, **single),
                  pl.BlockSpec((d, d_ff), const2, **single),
                  pl.BlockSpec((d_ff, d), const2, **single),
                  pl.BlockSpec((None, 1, d), of_layer, **single),
                  pl.BlockSpec((None, 1, d), of_layer, **single),
                  pl.BlockSpec((None, 1, d), of_layer, **single),
                  pl.BlockSpec((None, 1, d), of_layer, **single)],
        out_specs=pl.BlockSpec((1, tb, d), lambda b, i: (b, i, 0)),
        out_shape=jax.ShapeDtypeStruct((bsz, s, d), x.dtype),
        scratch_shapes=[pltpu.VMEM((tb, d_ff), BF16)],
        compiler_params=pltpu.CompilerParams(
            dimension_semantics=("parallel", "parallel"), vmem_limit_bytes=VMEM_LIMIT),
        name="ffn",
    )(x, mixed, mod, w_out, w1, w2, _layer_row(ln1_w), _layer_row(ln1_b), _layer_row(ln2_w),
      _layer_row(ln2_b))


def kernel(x, c, w_ada, b_ada, w_in, ret_norm_w, gla_gate_w, gla_gate_b, gla_norm_w,
           w_out, ln1_w, ln1_b, w_ff1, w_ff2, ln2_w, ln2_b):
    depth = w_in.shape[0]
    bsz, s, d = x.shape
    alpha = (2.0 * depth) ** 0.25
    tb = min(PROJMIX_ROWS, s)
    cos_t, sin_t = (jnp.asarray(t) for t in _rotary_tables(s))
    for l in range(depth):
        mod = _adaln(c, w_ada, b_ada, l)
        mixed, wo_bf, w1_bf, w2_bf, late_mod = _projmix(
            x, mod, cos_t, sin_t, w_in, gla_gate_w, gla_gate_b, ret_norm_w, gla_norm_w,
            w_out, w_ff1, w_ff2, c, w_ada, b_ada, l, tb)
        x = _ffn(x, mixed, late_mod, wo_bf, w1_bf, w2_bf, ln1_w, ln1_b, ln2_w, ln2_b, l, alpha,
                 min(FFN_ROWS, s))
    return x
```

```python
import functools
import math

import numpy as np
import jax
import jax.numpy as jnp
from jax import lax
from jax.experimental import pallas as pl
from jax.experimental.pallas import tpu as pltpu

CHUNK = 64
RET_HEADS = 4
RET_D = 128
RET_WIDTH = RET_HEADS * RET_D
GLA_HEADS = 4
GLA_DK = 64
GLA_DV = 128
GLA_KEY_WIDTH = GLA_HEADS * GLA_DK
GLA_WIDTH = GLA_HEADS * GLA_DV
D_MIX = RET_WIDTH + GLA_WIDTH
GATE_RANK = 16
GATE_TAU = 16.0
ROPE_BASE = 10000.0
LN_EPS = 1e-5
LANES = 128
MXU_WIDTH = 256
N_MOD = 6
N_EARLY_MOD = 2
PROJMIX_ROWS = 512
FFN_ROWS = 1024
SLAB = 4 * RET_WIDTH + 2 * GLA_KEY_WIDTH + 2 * GLA_WIDTH
OFF_RQ, OFF_RK, OFF_RV, OFF_RG = 0, 512, 1024, 1536
OFF_GQ, OFF_GK, OFF_GV, OFF_GG = 2048, 2304, 2560, 3072
MIX_ROWS = 256
BOUNDED_LOG2_DECAY = 96.0
BOUNDED_SPAN = 2 * CHUNK
VMEM_LIMIT = 56 * 1024 * 1024
LOG2E = math.log2(math.e)

BF16 = jnp.bfloat16
F32 = jnp.float32


def _dot(a, b):
    return jnp.dot(a, b, preferred_element_type=F32)


def _dot_nt(a, b):
    return lax.dot_general(a, b, (((1,), (1,)), ((), ())), preferred_element_type=F32)


def _dot_tn(a, b):
    return lax.dot_general(a, b, (((0,), (0,)), ((), ())), preferred_element_type=F32)


def _zero_of(v):
    bits = lax.shift_right_logical(pltpu.bitcast(v, jnp.uint32), jnp.uint32(32))
    return pltpu.bitcast(bits, F32)


def _ln(x, not_before=None):
    mu = jnp.mean(x, axis=-1, keepdims=True)
    if not_before is not None:
        mu = mu + _zero_of(not_before)[0:1, 0:1]
    xc = x - mu
    var = jnp.mean(xc * xc, axis=-1, keepdims=True)
    return xc * lax.rsqrt(var + LN_EPS)


def _silu(x):
    return x * (1.0 / (1.0 + jnp.exp(-x)))


def _adaln_kernel(c_ref, w_ref, b_ref, o_ref, *, per_step):
    c = c_ref[...]
    d = c.shape[1]
    val = _dot(_silu(c).astype(BF16), w_ref[...].astype(BF16)) + b_ref[...]
    for t in range(per_step):
        o_ref[:, pl.ds(pl.program_id(0) * per_step + t, 1), :] = val[:, None, t * d:(t + 1) * d]


def _adaln(c, w_ada, b_ada, layer):
    bsz, d = c.shape
    depth, _, n = w_ada.shape
    return pl.pallas_call(
        functools.partial(_adaln_kernel, per_step=N_EARLY_MOD),
        grid=(1,),
        in_specs=[pl.BlockSpec((bsz, d), lambda j: (0, 0)),
                  pl.BlockSpec((None, d, N_EARLY_MOD * d), lambda j: (layer, 0, 0)),
                  pl.BlockSpec((None, 1, N_EARLY_MOD * d), lambda j: (layer, 0, 0))],
        out_specs=pl.BlockSpec((bsz, N_EARLY_MOD, d), lambda j: (0, 0, 0)),
        out_shape=jax.ShapeDtypeStruct((bsz, N_EARLY_MOD, d), F32),
        compiler_params=pltpu.CompilerParams(vmem_limit_bytes=VMEM_LIMIT),
        name="adaln",
    )(c, w_ada, b_ada.reshape(depth, 1, n))


def _rotary_tables(s):
    half = RET_D // 2
    inv = 1.0 / (ROPE_BASE ** (np.arange(half, dtype=np.float64) / (half - 1)))
    ang = np.arange(s, dtype=np.float64)[:, None] * inv[None, :]
    cos = np.concatenate([np.cos(ang), np.cos(ang)], axis=1)
    sin = np.concatenate([-np.sin(ang), np.sin(ang)], axis=1)
    return cos.astype(np.float32), sin.astype(np.float32)


def _retention_tables(tb):
    log_gamma = np.log(1.0 - 2.0 ** (-5.0 - np.arange(RET_HEADS, dtype=np.float64)))
    idx = np.arange(tb)
    dist = np.abs(idx[:, None] - idx[None, :])
    visible = (idx[None, :] // CHUNK) <= (idx[:, None] // CHUNK)
    scale = RET_D ** -0.5
    dmask = np.exp(log_gamma[:, None, None] * dist[None]) * visible[None] * scale
    qdec = np.exp(log_gamma[:, None] * (idx[None, :] + 1.0)) * scale
    kdec = np.exp(log_gamma[:, None] * (tb - 1.0 - idx[None, :]))
    qdec = np.broadcast_to(qdec[:, :, None], (RET_HEADS, tb, RET_D))
    kdec = np.broadcast_to(kdec[:, :, None], (RET_HEADS, tb, RET_D))
    block_decay = [float(np.exp(lg * tb)) for lg in log_gamma]
    return (dmask.astype(np.float32), np.ascontiguousarray(qdec, np.float32),
            np.ascontiguousarray(kdec, np.float32), block_decay)


def _cumsum_matrix(tb):
    t = np.arange(tb)
    same_chunk = (t[:, None] // CHUNK) == (t[None, :] // CHUNK)
    return (same_chunk & (t[None, :] <= t[:, None])).astype(np.float32)


def _proj_stage(x_ref, mod_ref, cos_ref, sin_ref, w_ref, wtail_ref, gw_ref, gb_ref, slab, la_out,
                bounded_ref, slot):
    x = x_ref[0]
    shift1 = mod_ref[0, 0:1, :]
    scale1 = mod_ref[0, 1:2, :]
    ub = (_ln(x) * (1.0 + scale1) + shift1).astype(BF16)
    yield
    cos = cos_ref[...]
    sin = sin_ref[...]
    group = MXU_WIDTH
    for j in range(SLAB // group):
        p = _dot(ub, w_ref[:, j * group:(j + 1) * group])
        if j * group < OFF_RV:
            for h in range(group // RET_D):
                xh = p[:, h * RET_D:(h + 1) * RET_D]
                rot = xh * cos + pltpu.roll(xh, RET_D // 2, 1) * sin
                slab[:, j * group + h * RET_D:j * group + (h + 1) * RET_D] = rot.astype(BF16)
        else:
            slab[:, j * group:(j + 1) * group] = p.astype(BF16)
        yield
    glr = _dot_nt(ub, wtail_ref[...])
    logit = _dot(glr.astype(BF16), gw_ref[...].astype(BF16)) + gb_ref[...]
    log_sig = jnp.minimum(logit, 0.0) - jnp.log(1.0 + jnp.exp(-jnp.abs(logit)))
    la = log_sig * (LOG2E / GATE_TAU)
    la_out[...] = la
    bounded_ref[slot] = (jnp.min(la) * BOUNDED_SPAN >= -BOUNDED_LOG2_DECAY).astype(jnp.int32)
    yield


def _row_of_block(b, block, row):
    n = b.shape[0] // block
    b3 = b.reshape(n, block, b.shape[1])
    return jnp.broadcast_to(b3[:, row:row + 1, :], b3.shape).reshape(b.shape)


def _mix_stage(slab, la_ref, r0, dmask_ref, qdec_ref, kdec_ref, csum_ref, rnw_ref, gnw_ref,
               out_ref, r_state, g_state, block_decay, bounded):
    tb = MIX_ROWS
    blk = slice(r0, r0 + tb)
    for h in range(RET_HEADS):
        q = slab[blk, OFF_RQ + h * RET_D:OFF_RQ + (h + 1) * RET_D]
        k = slab[blk, OFF_RK + h * RET_D:OFF_RK + (h + 1) * RET_D]
        v = slab[blk, OFF_RV + h * RET_D:OFF_RV + (h + 1) * RET_D]
        g = slab[blk, OFF_RG + h * RET_D:OFF_RG + (h + 1) * RET_D].astype(F32)
        p = (_dot_nt(q, k) * dmask_ref[h]).astype(BF16)
        r_prev = r_state[h]
        o = _dot(p, v) + _dot(q, r_prev.astype(BF16)) * qdec_ref[h]
        kd = (k.astype(F32) * kdec_ref[h]).astype(BF16)
        r_state[h] = block_decay[h] * r_prev + _dot_tn(kd, v)
        o = o - jnp.mean(o, axis=-1, keepdims=True)
        o = o * lax.rsqrt(jnp.mean(o * o, axis=-1, keepdims=True) + LN_EPS)
        o = o * rnw_ref[:, h * RET_D:(h + 1) * RET_D] * _silu(g)
        out_ref[0, blk, h * RET_D:(h + 1) * RET_D] = o.astype(out_ref.dtype)
        yield

    la = la_ref[blk, :]
    la_hi = la.astype(BF16)
    la_lo = (la - la_hi.astype(F32)).astype(BF16)
    csum = csum_ref[...]
    b = _dot(csum, la_hi) + _dot(csum, la_lo)
    gq = slab[blk, OFF_GQ:OFF_GQ + GLA_KEY_WIDTH] * jnp.asarray(GLA_DK ** -0.5, BF16)
    gk = slab[blk, OFF_GK:OFF_GK + GLA_KEY_WIDTH]
    row = lax.broadcasted_iota(jnp.int32, (tb, GLA_KEY_WIDTH), 0)
    span = BOUNDED_SPAN if bounded else CHUNK
    if bounded:
        for c in range(1, span // CHUNK):
            carry = _row_of_block(b, span, c * CHUNK - 1)
            b = b + jnp.where((row & (span - 1)) >= c * CHUNK, carry, 0.0)
    q_state = gq * jnp.exp2(b).astype(BF16)
    k_state = gk * jnp.exp2(_row_of_block(b, span, span - 1) - b).astype(BF16)
    if bounded:
        grow = jnp.exp2(-b).astype(BF16)
        q_lvl = [q_state, gq * grow]
        phis = [grow, jnp.exp2(b).astype(BF16)]
    else:
        psis = [-jnp.abs(b - _row_of_block(b, 2 * c, c - 1)) for c in (32, 16, 8, 4)]
        la_next = pltpu.roll(la, tb - 1, 0)
        la_prev = pltpu.roll(la, 1, 0)
        r4 = row & 3
        psis.append(jnp.where(r4 == 0, la_next,
                              jnp.where(r4 == 1, 0.0, jnp.where(r4 == 2, la, la + la_prev))))
        psis.append(jnp.where((row & 1) == 1, la, 0.0))
        phis = [jnp.exp2(p).astype(BF16) for p in psis]
        q_lvl = [gq] + [gq * phi for phi in phis]
        phis = [None] + phis
    yield

    lane = lax.broadcasted_iota(jnp.int32, (span, LANES), 1)
    low_half = lane < GLA_DK
    zero_k = jnp.zeros((span, LANES), BF16)
    zero_s = jnp.zeros((GLA_DV, LANES), BF16)
    t_idx = lax.broadcasted_iota(jnp.int32, (span, GLA_HEADS * span), 0)
    s_idx = lax.broadcasted_iota(jnp.int32, (span, GLA_HEADS * span), 1) & (span - 1)
    t_xor_s = t_idx ^ s_idx
    lane_s = lax.broadcasted_iota(jnp.int32, (GLA_DV, LANES), 1)
    low_half_s = lane_s < GLA_DK

    def lane_tiles(piece, tile, zero, n_tiles):
        return jnp.concatenate([piece if j == tile else zero for j in range(n_tiles)], axis=1)

    for ch in range(tb // span):
        rows = slice(ch * span, (ch + 1) * span)
        orow = slice(r0 + ch * span, r0 + (ch + 1) * span)
        k_heads = [jnp.where(low_half if h % 2 == 0 else ~low_half,
                             gk[rows, (h // 2) * LANES:(h // 2 + 1) * LANES], zero_k)
                   for h in range(GLA_HEADS)]
        a_lvl = []
        for lvl in range(len(q_lvl)):
            pieces = []
            for h in range(GLA_HEADS):
                tile = h // 2
                kh = k_heads[h]
                if phis[lvl] is not None:
                    kh = kh * phis[lvl][rows, tile * LANES:(tile + 1) * LANES]
                pieces.append(lane_tiles(kh, tile, zero_k, 2))
            k_stack = jnp.concatenate(pieces, axis=0)
            a_lvl.append(_dot_nt(q_lvl[lvl][rows], k_stack))
        if bounded:
            same_chunk = (t_idx // CHUNK) == (s_idx // CHUNK)
            a = jnp.where(t_idx >= s_idx, a_lvl[0], jnp.where(same_chunk, a_lvl[1], 0.0))
        else:
            a = a_lvl[0]
            for i, c in enumerate((1, 2, 4, 8, 16, 32)):
                a = jnp.where(t_xor_s >= c, a_lvl[len(a_lvl) - 1 - i], a)
        a = a.astype(BF16)
        gv = slab[orow, OFF_GV:OFF_GV + GLA_WIDTH]
        v_bd = jnp.concatenate(
            [lane_tiles(gv[:, h * GLA_DV:(h + 1) * GLA_DV], h, jnp.zeros((span, GLA_DV), BF16), GLA_HEADS)
             for h in range(GLA_HEADS)], axis=0)
        s_bd = jnp.concatenate(
            [lane_tiles(g_state[h].astype(BF16), h // 2, zero_s, 2) for h in range(GLA_HEADS)],
            axis=0)
        o_all = _dot(a, v_bd) + _dot_nt(q_state[rows], s_bd)
        decay = jnp.exp2(b[(ch + 1) * span - 1:(ch + 1) * span, :])
        for h in range(GLA_HEADS):
            tile = h // 2
            upd = _dot_tn(gv[:, h * GLA_DV:(h + 1) * GLA_DV], k_state[rows, tile * LANES:(tile + 1) * LANES])
            upd = jnp.where(low_half_s if h % 2 == 0 else ~low_half_s, upd, 0.0)
            g_state[h] = g_state[h] * decay[:, tile * LANES:(tile + 1) * LANES] + upd
        for h in range(GLA_HEADS):
            o = o_all[:, h * GLA_DV:(h + 1) * GLA_DV]
            o = o * lax.rsqrt(jnp.mean(o * o, axis=-1, keepdims=True) + LN_EPS)
            gg = slab[orow, OFF_GG + h * GLA_DV:OFF_GG + (h + 1) * GLA_DV].astype(F32)
            o = o * gnw_ref[:, h * GLA_DV:(h + 1) * GLA_DV] * _silu(gg)
            out_ref[0, orow, RET_WIDTH + h * GLA_DV:RET_WIDTH + (h + 1) * GLA_DV] = o.astype(out_ref.dtype)
        yield


def _interleave(gen_a, n_a, gen_b, n_b):
    done_a = done_b = 0
    while done_a < n_a or done_b < n_b:
        if done_b >= n_b or (done_a < n_a and (done_a + 0.5) * n_b <= (done_b + 0.5) * n_a):
            next(gen_a)
            done_a += 1
        else:
            next(gen_b)
            done_b += 1
    for g in (gen_a, gen_b):
        assert next(g, "done") == "done", "piece count mismatch"


def _projmix_kernel(x_ref, mod_ref, cos_ref, sin_ref, w_ref, gw_ref, gb_ref,
                    dmask_ref, qdec_ref, kdec_ref, csum_ref, rnw_ref, gnw_ref,
                    wo_ref, w1_ref, w2_ref, c_ref, wa_ref, ba_ref,
                    out_ref, wo_bf_ref, w1_bf_ref, w2_bf_ref, late_mod_ref,
                    w_bf, w_tail, slab2, la2, bounded_ref, r_state, g_state,
                    *, tb, blocks_per_seq, n_blocks, block_decay):
    j = pl.program_id(0)
    n_proj = 2 + SLAB // MXU_WIDTH
    n_mix = {bounded: (tb // MIX_ROWS) * (RET_HEADS + 1 + MIX_ROWS // (BOUNDED_SPAN if bounded else CHUNK))
             for bounded in (True, False)}

    @pl.when(j == 0)
    def _():
        for g in range(0, SLAB, MXU_WIDTH):
            w_bf[:, g:g + MXU_WIDTH] = w_ref[g:g + MXU_WIDTH, :].T.astype(BF16)
        w_tail[...] = w_ref[SLAB:SLAB + GATE_RANK, :].astype(BF16)
        bounded_ref[1] = 0

    def side_jobs():
        wo_bf_ref[...] = wo_ref[...].astype(BF16)
        w1_bf_ref[...] = w1_ref[...].astype(BF16)
        w2_bf_ref[...] = w2_ref[...].astype(BF16)
        late_mod_ref[...] = (_dot(_silu(c_ref[...]).astype(BF16), wa_ref[...].astype(BF16))
                             + ba_ref[...])

    @pl.when((j - 1) % blocks_per_seq == 0)
    def _():
        r_state[...] = jnp.zeros_like(r_state)
        g_state[...] = jnp.zeros_like(g_state)

    def project(buf):
        return _proj_stage(x_ref, mod_ref, cos_ref, sin_ref, w_bf, w_tail, gw_ref, gb_ref,
                           slab2.at[buf], la2.at[buf], bounded_ref, buf)

    def mix(buf, bounded):
        for r0 in range(0, tb, MIX_ROWS):
            yield from _mix_stage(slab2.at[buf], la2.at[buf], r0, dmask_ref, qdec_ref, kdec_ref,
                                  csum_ref, rnw_ref, gnw_ref, out_ref, r_state, g_state, block_decay,
                                  bounded)

    @pl.when(j == 0)
    def _():
        side_jobs()
        for _ in project(0):
            pass

    steady = (j > 0) & (j < n_blocks)
    bounded = bounded_ref[1 - j % 2] == 1
    for parity in (0, 1):
        @pl.when(steady & bounded & (j % 2 == parity))
        def _():
            side_jobs()
            _interleave(project(parity), n_proj, mix(1 - parity, True), n_mix[True])

    @pl.when(steady & jnp.logical_not(bounded))
    def _():
        side_jobs()
        _interleave(project(j % 2), n_proj, mix(1 - j % 2, False), n_mix[False])

    @pl.when(j == n_blocks)
    def _():
        side_jobs()
        for _ in mix((n_blocks - 1) % 2, False):
            pass


def _layer_row(v):
    return v.reshape(v.shape[0], 1, v.shape[1])


def _projmix(x, mod, cos_t, sin_t, w_in, gate_w, gate_b, ret_norm_w, gla_norm_w,
             w_out, w_ff1, w_ff2, c, w_ada, b_ada, layer, tb):
    bsz, s, d = x.shape
    d_ff = w_ff1.shape[2]
    bps = s // tb
    nblk = bsz * bps
    late_cols = (N_MOD - N_EARLY_MOD) * d // nblk
    late_first = N_EARLY_MOD * d // late_cols
    dmask, qdec, kdec, block_decay = _retention_tables(MIX_ROWS)
    csum = _cumsum_matrix(MIX_ROWS)
    const2 = lambda j: (0, 0)
    const3 = lambda j: (0, 0, 0)
    of_layer = lambda j: (layer, 0, 0)
    proj_blk = lambda j: jnp.minimum(j, nblk - 1)
    mix_blk = lambda j: jnp.maximum(j - 1, 0)
    single = dict(pipeline_mode=pl.Buffered(1))
    wo_rows, w1_rows, w2_rows = d // nblk, d // nblk, d_ff // nblk
    return pl.pallas_call(
        functools.partial(_projmix_kernel, tb=tb, blocks_per_seq=bps, n_blocks=nblk,
                          block_decay=block_decay),
        grid=(nblk + 1,),
        in_specs=[pl.BlockSpec((1, tb, d), lambda j: (proj_blk(j) // bps, proj_blk(j) % bps, 0)),
                  pl.BlockSpec((1, N_EARLY_MOD, d), lambda j: (proj_blk(j) // bps, 0, 0)),
                  pl.BlockSpec((tb, RET_D), lambda j: (proj_blk(j) % bps, 0)),
                  pl.BlockSpec((tb, RET_D), lambda j: (proj_blk(j) % bps, 0)),
                  pl.BlockSpec((None, SLAB + GATE_RANK, d), of_layer, **single),
                  pl.BlockSpec((None, GATE_RANK, GLA_KEY_WIDTH), of_layer, **single),
                  pl.BlockSpec((None, 1, GLA_KEY_WIDTH), of_layer, **single),
                  pl.BlockSpec(dmask.shape, const3, **single),
                  pl.BlockSpec(qdec.shape, const3, **single),
                  pl.BlockSpec(kdec.shape, const3, **single),
                  pl.BlockSpec(csum.shape, const2, **single),
                  pl.BlockSpec((None, 1, RET_WIDTH), of_layer, **single),
                  pl.BlockSpec((None, 1, GLA_WIDTH), of_layer, **single),
                  pl.BlockSpec((None, wo_rows, d), lambda j: (layer, proj_blk(j), 0)),
                  pl.BlockSpec((None, w1_rows, d_ff), lambda j: (layer, proj_blk(j), 0)),
                  pl.BlockSpec((None, w2_rows, d), lambda j: (layer, proj_blk(j), 0)),
                  pl.BlockSpec((bsz, d), const2, **single),
                  pl.BlockSpec((None, d, late_cols), lambda j: (layer, 0, late_first + proj_blk(j))),
                  pl.BlockSpec((None, 1, late_cols), lambda j: (layer, 0, late_first + proj_blk(j)))],
        out_specs=[pl.BlockSpec((1, tb, D_MIX), lambda j: (mix_blk(j) // bps, mix_blk(j) % bps, 0)),
                   pl.BlockSpec((wo_rows, d), lambda j: (proj_blk(j), 0)),
                   pl.BlockSpec((w1_rows, d_ff), lambda j: (proj_blk(j), 0)),
                   pl.BlockSpec((w2_rows, d), lambda j: (proj_blk(j), 0)),
                   pl.BlockSpec((bsz, late_cols), lambda j: (0, proj_blk(j)))],
        out_shape=[jax.ShapeDtypeStruct((bsz, s, D_MIX), BF16),
                   jax.ShapeDtypeStruct((d, d), BF16),
                   jax.ShapeDtypeStruct((d, d_ff), BF16),
                   jax.ShapeDtypeStruct((d_ff, d), BF16),
                   jax.ShapeDtypeStruct((bsz, (N_MOD - N_EARLY_MOD) * d), F32)],
        scratch_shapes=[pltpu.VMEM((d, SLAB), BF16), pltpu.VMEM((GATE_RANK, d), BF16),
                        pltpu.VMEM((2, tb, SLAB), BF16), pltpu.VMEM((2, tb, GLA_KEY_WIDTH), F32),
                        pltpu.SMEM((2,), jnp.int32),
                        pltpu.VMEM((RET_HEADS, RET_D, RET_D), F32),
                        pltpu.VMEM((GLA_HEADS, GLA_DV, LANES), F32)],
        compiler_params=pltpu.CompilerParams(
            dimension_semantics=("arbitrary",), vmem_limit_bytes=VMEM_LIMIT),
        name="projmix",
    )(x, mod, cos_t, sin_t, jnp.swapaxes(w_in, 1, 2), gate_w, _layer_row(gate_b),
      jnp.asarray(dmask), jnp.asarray(qdec), jnp.asarray(kdec), jnp.asarray(csum, BF16),
      _layer_row(ret_norm_w), _layer_row(gla_norm_w), w_out, w_ff1, w_ff2,
      c, w_ada, _layer_row(b_ada))


def _ffn_kernel(x_ref, mix_ref, mod_ref, wo_ref, w1_ref, w2_ref, ln1w_ref, ln1b_ref,
                ln2w_ref, ln2b_ref, out_ref, h_ref, *, alpha):
    seq = pl.ds(pl.program_id(0), 1)
    d = x_ref.shape[2]
    gate1 = mod_ref[seq, 0:d]
    shift2 = mod_ref[seq, d:2 * d]
    scale2 = mod_ref[seq, 2 * d:3 * d]
    gate2 = mod_ref[seq, 3 * d:4 * d]
    d_ff = w1_ref.shape[1]
    n_groups = 4
    group = d_ff // n_groups
    tb = x_ref.shape[1]
    halves = (slice(0, tb // 2), slice(tb // 2, tb))

    def chain(rows, wait):
        m = _dot(mix_ref[0, rows, :], wo_ref[...])
        yield
        x1 = (_ln(alpha * x_ref[0, rows, :] + gate1 * m, wait.get("ln1")) * ln1w_ref[...]
              + ln1b_ref[...])
        u2f = _ln(x1) * (1.0 + scale2) + shift2
        wait["ln1_done"] = u2f[u2f.shape[0] - 8:, u2f.shape[1] - LANES:]
        u2 = u2f.astype(BF16)
        yield
        for j in range(n_groups):
            hj = jnp.maximum(_dot(u2, w1_ref[:, j * group:(j + 1) * group]), 0.0)
            h_ref[rows, j * group:(j + 1) * group] = (hj * hj).astype(BF16)
            yield
        f = _dot(h_ref[rows, :], w2_ref[...])
        yield
        out_ref[0, rows, :] = _ln(alpha * x1 + gate2 * f) * ln2w_ref[...] + ln2b_ref[...]
        yield

    wait_a, wait_b = {}, {}
    half_a, half_b = chain(halves[0], wait_a), chain(halves[1], wait_b)
    next(half_a)
    next(half_b)
    next(half_a)
    wait_b["ln1"] = wait_a["ln1_done"]
    n_rest = n_groups + 2
    _interleave(half_a, n_rest, half_b, n_rest + 1)


def _ffn(x, mixed, mod, w_out, w1, w2, ln1_w, ln1_b, ln2_w, ln2_b, layer, alpha, tb):
    bsz, s, d = x.shape
    nb = s // tb
    d_ff = w1.shape[1]
    const2 = lambda b, i: (0, 0)
    of_layer = lambda b, i: (layer, 0, 0)
    single = dict(pipeline_mode=pl.Buffered(1))
    return pl.pallas_call(
        functools.partial(_ffn_kernel, alpha=alpha),
        grid=(bsz, nb),
        in_specs=[pl.BlockSpec((1, tb, d), lambda b, i: (b, i, 0)),
                  pl.BlockSpec((1, tb, d), lambda b, i: (b, i, 0)),
                  pl.BlockSpec(mod.shape, const2, **single),
                  pl.BlockSpec((d, d), const2, **single),
                  pl.BlockSpec((d, d_ff), const2, **single),
                  pl.BlockSpec((d_ff, d), const2, **single),
                  pl.BlockSpec((None, 1, d), of_layer, **single),
                  pl.BlockSpec((None, 1, d), of_layer, **single),
                  pl.BlockSpec((None, 1, d), of_layer, **single),
                  pl.BlockSpec((None, 1, d), of_layer, **single)],
        out_specs=pl.BlockSpec((1, tb, d), lambda b, i: (b, i, 0)),
        out_shape=jax.ShapeDtypeStruct((bsz, s, d), x.dtype),
        scratch_shapes=[pltpu.VMEM((tb, d_ff), BF16)],
        compiler_params=pltpu.CompilerParams(
            dimension_semantics=("parallel", "parallel"), vmem_limit_bytes=VMEM_LIMIT),
        name="ffn",
    )(x, mixed, mod, w_out, w1, w2, _layer_row(ln1_w), _layer_row(ln1_b), _layer_row(ln2_w),
      _layer_row(ln2_b))


def kernel(x, c, w_ada, b_ada, w_in, ret_norm_w, gla_gate_w, gla_gate_b, gla_norm_w,
           w_out, ln1_w, ln1_b, w_ff1, w_ff2, ln2_w, ln2_b):
    depth = w_in.shape[0]
    bsz, s, d = x.shape
    alpha = (2.0 * depth) ** 0.25
    tb = min(PROJMIX_ROWS, s)
    cos_t, sin_t = (jnp.asarray(t) for t in _rotary_tables(s))
    for l in range(depth):
        mod = _adaln(c, w_ada, b_ada, l)
        mixed, wo_bf, w1_bf, w2_bf, late_mod = _projmix(
            x, mod, cos_t, sin_t, w_in, gla_gate_w, gla_gate_b, ret_norm_w, gla_norm_w,
            w_out, w_ff1, w_ff2, c, w_ada, b_ada, l, tb)
        x = _ffn(x, mixed, late_mod, wo_bf, w1_bf, w2_bf, ln1_w, ln1_b, ln2_w, ln2_b, l, alpha,
                 min(FFN_ROWS, s))
    return x
```

```python
import functools
import math

import numpy as np
import jax
import jax.numpy as jnp
from jax import lax
from jax.experimental import pallas as pl
from jax.experimental.pallas import tpu as pltpu

CHUNK = 64
RET_HEADS = 4
RET_D = 128
RET_WIDTH = RET_HEADS * RET_D
GLA_HEADS = 4
GLA_DK = 64
GLA_DV = 128
GLA_KEY_WIDTH = GLA_HEADS * GLA_DK
GLA_WIDTH = GLA_HEADS * GLA_DV
D_MIX = RET_WIDTH + GLA_WIDTH
GATE_RANK = 16
GATE_TAU = 16.0
ROPE_BASE = 10000.0
LN_EPS = 1e-5
LANES = 128
MXU_WIDTH = 256
N_MOD = 6
N_EARLY_MOD = 2
PROJMIX_ROWS = 512
FFN_ROWS = 1024
SLAB = 4 * RET_WIDTH + 2 * GLA_KEY_WIDTH + 2 * GLA_WIDTH
OFF_RQ, OFF_RK, OFF_RV, OFF_RG = 0, 512, 1024, 1536
OFF_GQ, OFF_GK, OFF_GV, OFF_GG = 2048, 2304, 2560, 3072
MIX_ROWS = 256
BOUNDED_LOG2_DECAY = 96.0
BOUNDED_LOG2_QK = 24.0
BOUNDED_SPAN = 2 * CHUNK
VMEM_LIMIT = 56 * 1024 * 1024
LOG2E = math.log2(math.e)

BF16 = jnp.bfloat16
F32 = jnp.float32


def _dot(a, b):
    return jnp.dot(a, b, preferred_element_type=F32)


def _dot_nt(a, b):
    return lax.dot_general(a, b, (((1,), (1,)), ((), ())), preferred_element_type=F32)


def _dot_tn(a, b):
    return lax.dot_general(a, b, (((0,), (0,)), ((), ())), preferred_element_type=F32)


def _zero_of(v):
    bits = lax.shift_right_logical(pltpu.bitcast(v, jnp.uint32), jnp.uint32(32))
    return pltpu.bitcast(bits, F32)


def _ln(x, not_before=None):
    mu = jnp.mean(x, axis=-1, keepdims=True)
    if not_before is not None:
        mu = mu + _zero_of(not_before)[0:1, 0:1]
    xc = x - mu
    var = jnp.mean(xc * xc, axis=-1, keepdims=True)
    return xc * lax.rsqrt(var + LN_EPS)


def _silu(x):
    return x * (1.0 / (1.0 + jnp.exp(-x)))


def _adaln_kernel(c_ref, w_ref, b_ref, o_ref, *, per_step):
    c = c_ref[...]
    d = c.shape[1]
    val = _dot(_silu(c).astype(BF16), w_ref[...].astype(BF16)) + b_ref[...]
    for t in range(per_step):
        o_ref[:, pl.ds(pl.program_id(0) * per_step + t, 1), :] = val[:, None, t * d:(t + 1) * d]


def _adaln(c, w_ada, b_ada, layer):
    bsz, d = c.shape
    depth, _, n = w_ada.shape
    return pl.pallas_call(
        functools.partial(_adaln_kernel, per_step=N_EARLY_MOD),
        grid=(1,),
        in_specs=[pl.BlockSpec((bsz, d), lambda j: (0, 0)),
                  pl.BlockSpec((None, d, N_EARLY_MOD * d), lambda j: (layer, 0, 0)),
                  pl.BlockSpec((None, 1, N_EARLY_MOD * d), lambda j: (layer, 0, 0))],
        out_specs=pl.BlockSpec((bsz, N_EARLY_MOD, d), lambda j: (0, 0, 0)),
        out_shape=jax.ShapeDtypeStruct((bsz, N_EARLY_MOD, d), F32),
        compiler_params=pltpu.CompilerParams(vmem_limit_bytes=VMEM_LIMIT),
        name="adaln",
    )(c, w_ada, b_ada.reshape(depth, 1, n))


def _rotary_tables(s):
    half = RET_D // 2
    inv = 1.0 / (ROPE_BASE ** (np.arange(half, dtype=np.float64) / (half - 1)))
    ang = np.arange(s, dtype=np.float64)[:, None] * inv[None, :]
    cos = np.concatenate([np.cos(ang), np.cos(ang)], axis=1)
    sin = np.concatenate([-np.sin(ang), np.sin(ang)], axis=1)
    return cos.astype(np.float32), sin.astype(np.float32)


def _retention_tables(tb):
    log_gamma = np.log(1.0 - 2.0 ** (-5.0 - np.arange(RET_HEADS, dtype=np.float64)))
    idx = np.arange(tb)
    dist = np.abs(idx[:, None] - idx[None, :])
    visible = (idx[None, :] // CHUNK) <= (idx[:, None] // CHUNK)
    scale = RET_D ** -0.5
    dmask = np.exp(log_gamma[:, None, None] * dist[None]) * visible[None] * scale
    qdec = np.exp(log_gamma[:, None] * (idx[None, :] + 1.0)) * scale
    kdec = np.exp(log_gamma[:, None] * (tb - 1.0 - idx[None, :]))
    qdec = np.broadcast_to(qdec[:, :, None], (RET_HEADS, tb, RET_D))
    kdec = np.broadcast_to(kdec[:, :, None], (RET_HEADS, tb, RET_D))
    block_decay = [float(np.exp(lg * tb)) for lg in log_gamma]
    return (dmask.astype(np.float32), np.ascontiguousarray(qdec, np.float32),
            np.ascontiguousarray(kdec, np.float32), block_decay)


def _cumsum_matrix(tb):
    t = np.arange(tb)
    same_chunk = (t[:, None] // CHUNK) == (t[None, :] // CHUNK)
    return (same_chunk & (t[None, :] <= t[:, None])).astype(np.float32)


def _proj_stage(x_ref, mod_ref, cos_ref, sin_ref, w_ref, wtail_ref, gw_ref, gb_ref, slab, la_out,
                bounded_ref, slot):
    x = x_ref[0]
    shift1 = mod_ref[0, 0:1, :]
    scale1 = mod_ref[0, 1:2, :]
    ub = (_ln(x) * (1.0 + scale1) + shift1).astype(BF16)
    yield
    cos = cos_ref[...]
    sin = sin_ref[...]
    group = MXU_WIDTH
    qk_max = jnp.float32(0.0)
    for j in range(SLAB // group):
        p = _dot(ub, w_ref[:, j * group:(j + 1) * group])
        if OFF_GQ <= j * group < OFF_GV:
            qk_max = jnp.maximum(qk_max, jnp.max(jnp.abs(p)))
        if j * group < OFF_RV:
            for h in range(group // RET_D):
                xh = p[:, h * RET_D:(h + 1) * RET_D]
                rot = xh * cos + pltpu.roll(xh, RET_D // 2, 1) * sin
                slab[:, j * group + h * RET_D:j * group + (h + 1) * RET_D] = rot.astype(BF16)
        else:
            slab[:, j * group:(j + 1) * group] = p.astype(BF16)
        yield
    glr = _dot_nt(ub, wtail_ref[...])
    logit = _dot(glr.astype(BF16), gw_ref[...].astype(BF16)) + gb_ref[...]
    log_sig = jnp.minimum(logit, 0.0) - jnp.log(1.0 + jnp.exp(-jnp.abs(logit)))
    la = log_sig * (LOG2E / GATE_TAU)
    la_out[...] = la
    bounded = (jnp.min(la) * BOUNDED_SPAN >= -BOUNDED_LOG2_DECAY) & (qk_max <= 2.0 ** BOUNDED_LOG2_QK)
    bounded_ref[slot] = bounded.astype(jnp.int32)
    yield


def _row_of_block(b, block, row):
    n = b.shape[0] // block
    b3 = b.reshape(n, block, b.shape[1])
    return jnp.broadcast_to(b3[:, row:row + 1, :], b3.shape).reshape(b.shape)


def _mix_stage(slab, la_ref, r0, dmask_ref, qdec_ref, kdec_ref, csum_ref, rnw_ref, gnw_ref,
               out_ref, r_state, g_state, block_decay, bounded):
    tb = MIX_ROWS
    blk = slice(r0, r0 + tb)
    for h in range(RET_HEADS):
        q = slab[blk, OFF_RQ + h * RET_D:OFF_RQ + (h + 1) * RET_D]
        k = slab[blk, OFF_RK + h * RET_D:OFF_RK + (h + 1) * RET_D]
        v = slab[blk, OFF_RV + h * RET_D:OFF_RV + (h + 1) * RET_D]
        g = slab[blk, OFF_RG + h * RET_D:OFF_RG + (h + 1) * RET_D].astype(F32)
        p = (_dot_nt(q, k) * dmask_ref[h]).astype(BF16)
        r_prev = r_state[h]
        o = _dot(p, v) + _dot(q, r_prev.astype(BF16)) * qdec_ref[h]
        kd = (k.astype(F32) * kdec_ref[h]).astype(BF16)
        r_state[h] = block_decay[h] * r_prev + _dot_tn(kd, v)
        o = o - jnp.mean(o, axis=-1, keepdims=True)
        o = o * lax.rsqrt(jnp.mean(o * o, axis=-1, keepdims=True) + LN_EPS)
        o = o * rnw_ref[:, h * RET_D:(h + 1) * RET_D] * _silu(g)
        out_ref[0, blk, h * RET_D:(h + 1) * RET_D] = o.astype(out_ref.dtype)
        yield

    la = la_ref[blk, :]
    la_hi = la.astype(BF16)
    la_lo = (la - la_hi.astype(F32)).astype(BF16)
    csum = csum_ref[...]
    b = _dot(csum, la_hi) + _dot(csum, la_lo)
    gq = slab[blk, OFF_GQ:OFF_GQ + GLA_KEY_WIDTH] * jnp.asarray(GLA_DK ** -0.5, BF16)
    gk = slab[blk, OFF_GK:OFF_GK + GLA_KEY_WIDTH]
    row = lax.broadcasted_iota(jnp.int32, (tb, GLA_KEY_WIDTH), 0)
    span = BOUNDED_SPAN if bounded else CHUNK
    if bounded:
        for c in range(1, span // CHUNK):
            carry = _row_of_block(b, span, c * CHUNK - 1)
            b = b + jnp.where((row & (span - 1)) >= c * CHUNK, carry, 0.0)
    q_state = gq * jnp.exp2(b).astype(BF16)
    k_state = gk * jnp.exp2(_row_of_block(b, span, span - 1) - b).astype(BF16)
    if bounded:
        grow = jnp.exp2(-b).astype(BF16)
        q_lvl = [q_state, gq * grow]
        phis = [grow, jnp.exp2(b).astype(BF16)]
    else:
        psis = [-jnp.abs(b - _row_of_block(b, 2 * c, c - 1)) for c in (32, 16, 8, 4)]
        la_next = pltpu.roll(la, tb - 1, 0)
        la_prev = pltpu.roll(la, 1, 0)
        r4 = row & 3
        psis.append(jnp.where(r4 == 0, la_next,
                              jnp.where(r4 == 1, 0.0, jnp.where(r4 == 2, la, la + la_prev))))
        psis.append(jnp.where((row & 1) == 1, la, 0.0))
        phis = [jnp.exp2(p).astype(BF16) for p in psis]
        q_lvl = [gq] + [gq * phi for phi in phis]
        phis = [None] + phis
    yield

    lane = lax.broadcasted_iota(jnp.int32, (span, LANES), 1)
    low_half = lane < GLA_DK
    zero_k = jnp.zeros((span, LANES), BF16)
    zero_s = jnp.zeros((GLA_DV, LANES), BF16)
    t_idx = lax.broadcasted_iota(jnp.int32, (span, GLA_HEADS * span), 0)
    s_idx = lax.broadcasted_iota(jnp.int32, (span, GLA_HEADS * span), 1) & (span - 1)
    t_xor_s = t_idx ^ s_idx
    lane_s = lax.broadcasted_iota(jnp.int32, (GLA_DV, LANES), 1)
    low_half_s = lane_s < GLA_DK

    def lane_tiles(piece, tile, zero, n_tiles):
        return jnp.concatenate([piece if j == tile else zero for j in range(n_tiles)], axis=1)

    for ch in range(tb // span):
        rows = slice(ch * span, (ch + 1) * span)
        orow = slice(r0 + ch * span, r0 + (ch + 1) * span)
        k_heads = [jnp.where(low_half if h % 2 == 0 else ~low_half,
                             gk[rows, (h // 2) * LANES:(h // 2 + 1) * LANES], zero_k)
                   for h in range(GLA_HEADS)]
        a_lvl = []
        for lvl in range(len(q_lvl)):
            pieces = []
            for h in range(GLA_HEADS):
                tile = h // 2
                kh = k_heads[h]
                if phis[lvl] is not None:
                    kh = kh * phis[lvl][rows, tile * LANES:(tile + 1) * LANES]
                pieces.append(lane_tiles(kh, tile, zero_k, 2))
            k_stack = jnp.concatenate(pieces, axis=0)
            a_lvl.append(_dot_nt(q_lvl[lvl][rows], k_stack))
        if bounded:
            same_chunk = (t_idx // CHUNK) == (s_idx // CHUNK)
            a = jnp.where(t_idx >= s_idx, a_lvl[0], jnp.where(same_chunk, a_lvl[1], 0.0))
        else:
            a = a_lvl[0]
            for i, c in enumerate((1, 2, 4, 8, 16, 32)):
                a = jnp.where(t_xor_s >= c, a_lvl[len(a_lvl) - 1 - i], a)
        a = a.astype(BF16)
        gv = slab[orow, OFF_GV:OFF_GV + GLA_WIDTH]
        v_bd = jnp.concatenate(
            [lane_tiles(gv[:, h * GLA_DV:(h + 1) * GLA_DV], h, jnp.zeros((span, GLA_DV), BF16), GLA_HEADS)
             for h in range(GLA_HEADS)], axis=0)
        s_bd = jnp.concatenate(
            [lane_tiles(g_state[h].astype(BF16), h // 2, zero_s, 2) for h in range(GLA_HEADS)],
            axis=0)
        o_all = _dot(a, v_bd) + _dot_nt(q_state[rows], s_bd)
        decay = jnp.exp2(b[(ch + 1) * span - 1:(ch + 1) * span, :])
        for h in range(GLA_HEADS):
            tile = h // 2
            upd = _dot_tn(gv[:, h * GLA_DV:(h + 1) * GLA_DV], k_state[rows, tile * LANES:(tile + 1) * LANES])
            upd = jnp.where(low_half_s if h % 2 == 0 else ~low_half_s, upd, 0.0)
            g_state[h] = g_state[h] * decay[:, tile * LANES:(tile + 1) * LANES] + upd
        for h in range(GLA_HEADS):
            o = o_all[:, h * GLA_DV:(h + 1) * GLA_DV]
            o = o * lax.rsqrt(jnp.mean(o * o, axis=-1, keepdims=True) + LN_EPS)
            gg = slab[orow, OFF_GG + h * GLA_DV:OFF_GG + (h + 1) * GLA_DV].astype(F32)
            o = o * gnw_ref[:, h * GLA_DV:(h + 1) * GLA_DV] * _silu(gg)
            out_ref[0, orow, RET_WIDTH + h * GLA_DV:RET_WIDTH + (h + 1) * GLA_DV] = o.astype(out_ref.dtype)
        yield


def _interleave(gen_a, n_a, gen_b, n_b):
    done_a = done_b = 0
    while done_a < n_a or done_b < n_b:
        if done_b >= n_b or (done_a < n_a and (done_a + 0.5) * n_b <= (done_b + 0.5) * n_a):
            next(gen_a)
            done_a += 1
        else:
            next(gen_b)
            done_b += 1
    for g in (gen_a, gen_b):
        assert next(g, "done") == "done", "piece count mismatch"


def _projmix_kernel(x_ref, mod_ref, cos_ref, sin_ref, w_ref, gw_ref, gb_ref,
                    dmask_ref, qdec_ref, kdec_ref, csum_ref, rnw_ref, gnw_ref,
                    wo_ref, w1_ref, w2_ref, c_ref, wa_ref, ba_ref,
                    out_ref, wo_bf_ref, w1_bf_ref, w2_bf_ref, late_mod_ref,
                    w_bf, w_tail, slab2, la2, bounded_ref, r_state, g_state,
                    *, tb, blocks_per_seq, n_blocks, block_decay):
    j = pl.program_id(0)
    n_proj = 2 + SLAB // MXU_WIDTH
    n_mix = {bounded: (tb // MIX_ROWS) * (RET_HEADS + 1 + MIX_ROWS // (BOUNDED_SPAN if bounded else CHUNK))
             for bounded in (True, False)}

    @pl.when(j == 0)
    def _():
        for g in range(0, SLAB, MXU_WIDTH):
            w_bf[:, g:g + MXU_WIDTH] = w_ref[g:g + MXU_WIDTH, :].T.astype(BF16)
        w_tail[...] = w_ref[SLAB:SLAB + GATE_RANK, :].astype(BF16)
        bounded_ref[1] = 0

    def side_jobs():
        wo_bf_ref[...] = wo_ref[...].astype(BF16)
        w1_bf_ref[...] = w1_ref[...].astype(BF16)
        w2_bf_ref[...] = w2_ref[...].astype(BF16)
        late_mod_ref[...] = (_dot(_silu(c_ref[...]).astype(BF16), wa_ref[...].astype(BF16))
                             + ba_ref[...])

    @pl.when((j - 1) % blocks_per_seq == 0)
    def _():
        r_state[...] = jnp.zeros_like(r_state)
        g_state[...] = jnp.zeros_like(g_state)

    def project(buf):
        return _proj_stage(x_ref, mod_ref, cos_ref, sin_ref, w_bf, w_tail, gw_ref, gb_ref,
                           slab2.at[buf], la2.at[buf], bounded_ref, buf)

    def mix(buf, bounded):
        for r0 in range(0, tb, MIX_ROWS):
            yield from _mix_stage(slab2.at[buf], la2.at[buf], r0, dmask_ref, qdec_ref, kdec_ref,
                                  csum_ref, rnw_ref, gnw_ref, out_ref, r_state, g_state, block_decay,
                                  bounded)

    @pl.when(j == 0)
    def _():
        side_jobs()
        for _ in project(0):
            pass

    steady = (j > 0) & (j < n_blocks)
    bounded = bounded_ref[1 - j % 2] == 1
    for parity in (0, 1):
        @pl.when(steady & bounded & (j % 2 == parity))
        def _():
            side_jobs()
            _interleave(project(parity), n_proj, mix(1 - parity, True), n_mix[True])

    @pl.when(steady & jnp.logical_not(bounded))
    def _():
        side_jobs()
        _interleave(project(j % 2), n_proj, mix(1 - j % 2, False), n_mix[False])

    @pl.when(j == n_blocks)
    def _():
        side_jobs()
        for _ in mix((n_blocks - 1) % 2, False):
            pass


def _layer_row(v):
    return v.reshape(v.shape[0], 1, v.shape[1])


def _projmix(x, mod, cos_t, sin_t, w_in, gate_w, gate_b, ret_norm_w, gla_norm_w,
             w_out, w_ff1, w_ff2, c, w_ada, b_ada, layer, tb):
    bsz, s, d = x.shape
    d_ff = w_ff1.shape[2]
    bps = s // tb
    nblk = bsz * bps
    late_cols = (N_MOD - N_EARLY_MOD) * d // nblk
    late_first = N_EARLY_MOD * d // late_cols
    dmask, qdec, kdec, block_decay = _retention_tables(MIX_ROWS)
    csum = _cumsum_matrix(MIX_ROWS)
    const2 = lambda j: (0, 0)
    const3 = lambda j: (0, 0, 0)
    of_layer = lambda j: (layer, 0, 0)
    proj_blk = lambda j: jnp.minimum(j, nblk - 1)
    mix_blk = lambda j: jnp.maximum(j - 1, 0)
    single = dict(pipeline_mode=pl.Buffered(1))
    wo_rows, w1_rows, w2_rows = d // nblk, d // nblk, d_ff // nblk
    return pl.pallas_call(
        functools.partial(_projmix_kernel, tb=tb, blocks_per_seq=bps, n_blocks=nblk,
                          block_decay=block_decay),
        grid=(nblk + 1,),
        in_specs=[pl.BlockSpec((1, tb, d), lambda j: (proj_blk(j) // bps, proj_blk(j) % bps, 0)),
                  pl.BlockSpec((1, N_EARLY_MOD, d), lambda j: (proj_blk(j) // bps, 0, 0)),
                  pl.BlockSpec((tb, RET_D), lambda j: (proj_blk(j) % bps, 0)),
                  pl.BlockSpec((tb, RET_D), lambda j: (proj_blk(j) % bps, 0)),
                  pl.BlockSpec((None, SLAB + GATE_RANK, d), of_layer, **single),
                  pl.BlockSpec((None, GATE_RANK, GLA_KEY_WIDTH), of_layer, **single),
                  pl.BlockSpec((None, 1, GLA_KEY_WIDTH), of_layer, **single),
                  pl.BlockSpec(dmask.shape, const3, **single),
                  pl.BlockSpec(qdec.shape, const3, **single),
                  pl.BlockSpec(kdec.shape, const3, **single),
                  pl.BlockSpec(csum.shape, const2, **single),
                  pl.BlockSpec((None, 1, RET_WIDTH), of_layer, **single),
                  pl.BlockSpec((None, 1, GLA_WIDTH), of_layer, **single),
                  pl.BlockSpec((None, wo_rows, d), lambda j: (layer, proj_blk(j), 0)),
                  pl.BlockSpec((None, w1_rows, d_ff), lambda j: (layer, proj_blk(j), 0)),
                  pl.BlockSpec((None, w2_rows, d), lambda j: (layer, proj_blk(j), 0)),
                  pl.BlockSpec((bsz, d), const2, **single),
                  pl.BlockSpec((None, d, late_cols), lambda j: (layer, 0, late_first + proj_blk(j))),
                  pl.BlockSpec((None, 1, late_cols), lambda j: (layer, 0, late_first + proj_blk(j)))],
        out_specs=[pl.BlockSpec((1, tb, D_MIX), lambda j: (mix_blk(j) // bps, mix_blk(j) % bps, 0)),
                   pl.BlockSpec((wo_rows, d), lambda j: (proj_blk(j), 0)),
                   pl.BlockSpec((w1_rows, d_ff), lambda j: (proj_blk(j), 0)),
                   pl.BlockSpec((w2_rows, d), lambda j: (proj_blk(j), 0)),
                   pl.BlockSpec((bsz, late_cols), lambda j: (0, proj_blk(j)))],
        out_shape=[jax.ShapeDtypeStruct((bsz, s, D_MIX), BF16),
                   jax.ShapeDtypeStruct((d, d), BF16),
                   jax.ShapeDtypeStruct((d, d_ff), BF16),
                   jax.ShapeDtypeStruct((d_ff, d), BF16),
                   jax.ShapeDtypeStruct((bsz, (N_MOD - N_EARLY_MOD) * d), F32)],
        scratch_shapes=[pltpu.VMEM((d, SLAB), BF16), pltpu.VMEM((GATE_RANK, d), BF16),
                        pltpu.VMEM((2, tb, SLAB), BF16), pltpu.VMEM((2, tb, GLA_KEY_WIDTH), F32),
                        pltpu.SMEM((2,), jnp.int32),
                        pltpu.VMEM((RET_HEADS, RET_D, RET_D), F32),
                        pltpu.VMEM((GLA_HEADS, GLA_DV, LANES), F32)],
        compiler_params=pltpu.CompilerParams(
            dimension_semantics=("arbitrary",), vmem_limit_bytes=VMEM_LIMIT),
        name="projmix",
    )(x, mod, cos_t, sin_t, jnp.swapaxes(w_in, 1, 2), gate_w, _layer_row(gate_b),
      jnp.asarray(dmask), jnp.asarray(qdec), jnp.asarray(kdec), jnp.asarray(csum, BF16),
      _layer_row(ret_norm_w), _layer_row(gla_norm_w), w_out, w_ff1, w_ff2,
      c, w_ada, _layer_row(b_ada))


def _ffn_kernel(x_ref, mix_ref, mod_ref, wo_ref, w1_ref, w2_ref, ln1w_ref, ln1b_ref,
                ln2w_ref, ln2b_ref, out_ref, h_ref, *, alpha):
    seq = pl.ds(pl.program_id(0), 1)
    d = x_ref.shape[2]
    gate1 = mod_ref[seq, 0:d]
    shift2 = mod_ref[seq, d:2 * d]
    scale2 = mod_ref[seq, 2 * d:3 * d]
    gate2 = mod_ref[seq, 3 * d:4 * d]
    d_ff = w1_ref.shape[1]
    n_groups = 4
    group = d_ff // n_groups
    tb = x_ref.shape[1]
    halves = (slice(0, tb // 2), slice(tb // 2, tb))

    def chain(rows, wait):
        m = _dot(mix_ref[0, rows, :], wo_ref[...])
        yield
        x1 = (_ln(alpha * x_ref[0, rows, :] + gate1 * m, wait.get("ln1")) * ln1w_ref[...]
              + ln1b_ref[...])
        u2f = _ln(x1) * (1.0 + scale2) + shift2
        wait["ln1_done"] = u2f[u2f.shape[0] - 8:, u2f.shape[1] - LANES:]
        u2 = u2f.astype(BF16)
        yield
        for j in range(n_groups):
            hj = jnp.maximum(_dot(u2, w1_ref[:, j * group:(j + 1) * group]), 0.0)
            h_ref[rows, j * group:(j + 1) * group] = (hj * hj).astype(BF16)
            yield
        f = _dot(h_ref[rows, :], w2_ref[...])
        yield
        out_ref[0, rows, :] = _ln(alpha * x1 + gate2 * f) * ln2w_ref[...] + ln2b_ref[...]
        yield

    wait_a, wait_b = {}, {}
    half_a, half_b = chain(halves[0], wait_a), chain(halves[1], wait_b)
    next(half_a)
    next(half_b)
    next(half_a)
    wait_b["ln1"] = wait_a["ln1_done"]
    n_rest = n_groups + 2
    _interleave(half_a, n_rest, half_b, n_rest + 1)


def _ffn(x, mixed, mod, w_out, w1, w2, ln1_w, ln1_b, ln2_w, ln2_b, layer, alpha, tb):
    bsz, s, d = x.shape
    nb = s // tb
    d_ff = w1.shape[1]
    const2 = lambda b, i: (0, 0)
    of_layer = lambda b, i: (layer, 0, 0)
    single = dict(pipeline_mode=pl.Buffered(1))
    return pl.pallas_call(
        functools.partial(_ffn_kernel, alpha=alpha),
        grid=(bsz, nb),
        in_specs=[pl.BlockSpec((1, tb, d), lambda b, i: (b, i, 0)),
                  pl.BlockSpec((1, tb, d), lambda b, i: (b, i, 0)),
                  pl.BlockSpec(mod.shape, const2, **single),
                  pl.BlockSpec((d, d), const2, **single),
                  pl.BlockSpec((d, d_ff), const2, **single),
                  pl.BlockSpec((d_ff, d), const2, **single),
                  pl.BlockSpec((None, 1, d), of_layer, **single),
                  pl.BlockSpec((None, 1, d), of_layer, **single),
                  pl.BlockSpec((None, 1, d), of_layer, **single),
                  pl.BlockSpec((None, 1, d), of_layer, **single)],
        out_specs=pl.BlockSpec((1, tb, d), lambda b, i: (b, i, 0)),
        out_shape=jax.ShapeDtypeStruct((bsz, s, d), x.dtype),
        scratch_shapes=[pltpu.VMEM((tb, d_ff), BF16)],
        compiler_params=pltpu.CompilerParams(
            dimension_semantics=("parallel", "parallel"), vmem_limit_bytes=VMEM_LIMIT),
        name="ffn",
    )(x, mixed, mod, w_out, w1, w2, _layer_row(ln1_w), _layer_row(ln1_b), _layer_row(ln2_w),
      _layer_row(ln2_b))


def kernel(x, c, w_ada, b_ada, w_in, ret_norm_w, gla_gate_w, gla_gate_b, gla_norm_w,
           w_out, ln1_w, ln1_b, w_ff1, w_ff2, ln2_w, ln2_b):
    depth = w_in.shape[0]
    bsz, s, d = x.shape
    alpha = (2.0 * depth) ** 0.25
    tb = min(PROJMIX_ROWS, s)
    cos_t, sin_t = (jnp.asarray(t) for t in _rotary_tables(s))
    for l in range(depth):
        mod = _adaln(c, w_ada, b_ada, l)
        mixed, wo_bf, w1_bf, w2_bf, late_mod = _projmix(
            x, mod, cos_t, sin_t, w_in, gla_gate_w, gla_gate_b, ret_norm_w, gla_norm_w,
            w_out, w_ff1, w_ff2, c, w_ada, b_ada, l, tb)
        x = _ffn(x, mixed, late_mod, wo_bf, w1_bf, w2_bf, ln1_w, ln1_b, ln2_w, ln2_b, l, alpha,
                 min(FFN_ROWS, s))
    return x
```

```python
import functools
import math

import numpy as np
import jax
import jax.numpy as jnp
from jax import lax
from jax.experimental import pallas as pl
from jax.experimental.pallas import tpu as pltpu

CHUNK = 64
RET_HEADS = 4
RET_D = 128
RET_WIDTH = RET_HEADS * RET_D
GLA_HEADS = 4
GLA_DK = 64
GLA_DV = 128
GLA_KEY_WIDTH = GLA_HEADS * GLA_DK
GLA_WIDTH = GLA_HEADS * GLA_DV
D_MIX = RET_WIDTH + GLA_WIDTH
GATE_RANK = 16
GATE_TAU = 16.0
ROPE_BASE = 10000.0
LN_EPS = 1e-5
LANES = 128
MXU_WIDTH = 256
N_MOD = 6
N_EARLY_MOD = 2
PROJMIX_ROWS = 512
FFN_ROWS = 1024
SLAB = 4 * RET_WIDTH + 2 * GLA_KEY_WIDTH + 2 * GLA_WIDTH
OFF_RQ, OFF_RK, OFF_RV, OFF_RG = 0, 512, 1024, 1536
OFF_GQ, OFF_GK, OFF_GV, OFF_GG = 2048, 2304, 2560, 3072
MIX_ROWS = 256
BOUNDED_LOG2_DECAY = 96.0
BOUNDED_LOG2_QK = 24.0
BOUNDED_SPAN = 2 * CHUNK
VMEM_LIMIT = 56 * 1024 * 1024
LOG2E = math.log2(math.e)

BF16 = jnp.bfloat16
F32 = jnp.float32


def _dot(a, b):
    return jnp.dot(a, b, preferred_element_type=F32)


def _dot_nt(a, b):
    return lax.dot_general(a, b, (((1,), (1,)), ((), ())), preferred_element_type=F32)


def _dot_tn(a, b):
    return lax.dot_general(a, b, (((0,), (0,)), ((), ())), preferred_element_type=F32)


def _zero_of(v):
    bits = lax.shift_right_logical(pltpu.bitcast(v, jnp.uint32), jnp.uint32(32))
    return pltpu.bitcast(bits, F32)


def _ln(x, not_before=None):
    mu = jnp.mean(x, axis=-1, keepdims=True)
    if not_before is not None:
        mu = mu + _zero_of(not_before)[0:1, 0:1]
    xc = x - mu
    var = jnp.mean(xc * xc, axis=-1, keepdims=True)
    return xc * lax.rsqrt(var + LN_EPS)


def _silu(x):
    return x * (1.0 / (1.0 + jnp.exp(-x)))


def _adaln_kernel(c_ref, w_ref, b_ref, o_ref, *, per_step):
    c = c_ref[...]
    d = c.shape[1]
    val = _dot(_silu(c).astype(BF16), w_ref[...].astype(BF16)) + b_ref[...]
    for t in range(per_step):
        o_ref[:, pl.ds(pl.program_id(0) * per_step + t, 1), :] = val[:, None, t * d:(t + 1) * d]


def _adaln(c, w_ada, b_ada, layer):
    bsz, d = c.shape
    depth, _, n = w_ada.shape
    return pl.pallas_call(
        functools.partial(_adaln_kernel, per_step=N_EARLY_MOD),
        grid=(1,),
        in_specs=[pl.BlockSpec((bsz, d), lambda j: (0, 0)),
                  pl.BlockSpec((None, d, N_EARLY_MOD * d), lambda j: (layer, 0, 0)),
                  pl.BlockSpec((None, 1, N_EARLY_MOD * d), lambda j: (layer, 0, 0))],
        out_specs=pl.BlockSpec((bsz, N_EARLY_MOD, d), lambda j: (0, 0, 0)),
        out_shape=jax.ShapeDtypeStruct((bsz, N_EARLY_MOD, d), F32),
        compiler_params=pltpu.CompilerParams(vmem_limit_bytes=VMEM_LIMIT),
        name="adaln",
    )(c, w_ada, b_ada.reshape(depth, 1, n))


def _rotary_tables(s):
    half = RET_D // 2
    inv = 1.0 / (ROPE_BASE ** (np.arange(half, dtype=np.float64) / (half - 1)))
    ang = np.arange(s, dtype=np.float64)[:, None] * inv[None, :]
    cos = np.concatenate([np.cos(ang), np.cos(ang)], axis=1)
    sin = np.concatenate([-np.sin(ang), np.sin(ang)], axis=1)
    return cos.astype(np.float32), sin.astype(np.float32)


def _retention_tables(tb):
    log_gamma = np.log(1.0 - 2.0 ** (-5.0 - np.arange(RET_HEADS, dtype=np.float64)))
    idx = np.arange(tb)
    dist = np.abs(idx[:, None] - idx[None, :])
    visible = (idx[None, :] // CHUNK) <= (idx[:, None] // CHUNK)
    scale = RET_D ** -0.5
    dmask = np.exp(log_gamma[:, None, None] * dist[None]) * visible[None] * scale
    qdec = np.exp(log_gamma[:, None] * (idx[None, :] + 1.0)) * scale
    kdec = np.exp(log_gamma[:, None] * (tb - 1.0 - idx[None, :]))
    qdec = np.broadcast_to(qdec[:, :, None], (RET_HEADS, tb, RET_D))
    kdec = np.broadcast_to(kdec[:, :, None], (RET_HEADS, tb, RET_D))
    block_decay = [float(np.exp(lg * tb)) for lg in log_gamma]
    return (dmask.astype(np.float32), np.ascontiguousarray(qdec, np.float32),
            np.ascontiguousarray(kdec, np.float32), block_decay)


def _cumsum_matrix(tb):
    t = np.arange(tb)
    same_chunk = (t[:, None] // CHUNK) == (t[None, :] // CHUNK)
    return (same_chunk & (t[None, :] <= t[:, None])).astype(np.float32)


def _proj_stage(x_ref, mod_ref, cos_ref, sin_ref, w_ref, wtail_ref, gw_ref, gb_ref, slab, la_out,
                bounded_ref, slot):
    x = x_ref[0]
    shift1 = mod_ref[0, 0:1, :]
    scale1 = mod_ref[0, 1:2, :]
    ub = (_ln(x) * (1.0 + scale1) + shift1).astype(BF16)
    yield
    cos = cos_ref[...]
    sin = sin_ref[...]
    group = MXU_WIDTH
    qk_max = jnp.float32(0.0)
    for j in range(SLAB // group):
        p = _dot(ub, w_ref[:, j * group:(j + 1) * group])
        if OFF_GQ <= j * group < OFF_GV:
            qk_max = jnp.maximum(qk_max, jnp.max(jnp.abs(p)))
        if j * group < OFF_RV:
            for h in range(group // RET_D):
                xh = p[:, h * RET_D:(h + 1) * RET_D]
                rot = xh * cos + pltpu.roll(xh, RET_D // 2, 1) * sin
                slab[:, j * group + h * RET_D:j * group + (h + 1) * RET_D] = rot.astype(BF16)
        else:
            slab[:, j * group:(j + 1) * group] = p.astype(BF16)
        yield
    glr = _dot_nt(ub, wtail_ref[...])
    logit = _dot(glr.astype(BF16), gw_ref[...].astype(BF16)) + gb_ref[...]
    log_sig = jnp.minimum(logit, 0.0) - jnp.log(1.0 + jnp.exp(-jnp.abs(logit)))
    la = log_sig * (LOG2E / GATE_TAU)
    la_out[...] = la
    bounded = (jnp.min(la) * BOUNDED_SPAN >= -BOUNDED_LOG2_DECAY) & (qk_max <= 2.0 ** BOUNDED_LOG2_QK)
    bounded_ref[slot] = bounded.astype(jnp.int32)
    yield


def _row_of_block(b, block, row):
    n = b.shape[0] // block
    b3 = b.reshape(n, block, b.shape[1])
    return jnp.broadcast_to(b3[:, row:row + 1, :], b3.shape).reshape(b.shape)


def _mix_stage(slab, la_ref, r0, dmask_ref, qdec_ref, kdec_ref, csum_ref, rnw_ref, gnw_ref,
               out_ref, r_state, g_state, block_decay, bounded):
    tb = MIX_ROWS
    blk = slice(r0, r0 + tb)
    for h in range(RET_HEADS):
        q = slab[blk, OFF_RQ + h * RET_D:OFF_RQ + (h + 1) * RET_D]
        k = slab[blk, OFF_RK + h * RET_D:OFF_RK + (h + 1) * RET_D]
        v = slab[blk, OFF_RV + h * RET_D:OFF_RV + (h + 1) * RET_D]
        g = slab[blk, OFF_RG + h * RET_D:OFF_RG + (h + 1) * RET_D].astype(F32)
        p = (_dot_nt(q, k) * dmask_ref[h]).astype(BF16)
        r_prev = r_state[h]
        o = _dot(p, v) + _dot(q, r_prev.astype(BF16)) * qdec_ref[h]
        kd = (k.astype(F32) * kdec_ref[h]).astype(BF16)
        r_state[h] = block_decay[h] * r_prev + _dot_tn(kd, v)
        o = o - jnp.mean(o, axis=-1, keepdims=True)
        o = o * lax.rsqrt(jnp.mean(o * o, axis=-1, keepdims=True) + LN_EPS)
        o = o * rnw_ref[:, h * RET_D:(h + 1) * RET_D] * _silu(g)
        out_ref[0, blk, h * RET_D:(h + 1) * RET_D] = o.astype(out_ref.dtype)
        yield

    la = la_ref[blk, :]
    la_hi = la.astype(BF16)
    la_lo = (la - la_hi.astype(F32)).astype(BF16)
    csum = csum_ref[...]
    b = _dot(csum, la_hi) + _dot(csum, la_lo)
    gq = slab[blk, OFF_GQ:OFF_GQ + GLA_KEY_WIDTH] * jnp.asarray(GLA_DK ** -0.5, BF16)
    gk = slab[blk, OFF_GK:OFF_GK + GLA_KEY_WIDTH]
    row = lax.broadcasted_iota(jnp.int32, (tb, GLA_KEY_WIDTH), 0)
    span = BOUNDED_SPAN if bounded else CHUNK
    if bounded:
        for c in range(1, span // CHUNK):
            carry = _row_of_block(b, span, c * CHUNK - 1)
            b = b + jnp.where((row & (span - 1)) >= c * CHUNK, carry, 0.0)
    q_state = gq * jnp.exp2(b).astype(BF16)
    k_state = gk * jnp.exp2(_row_of_block(b, span, span - 1) - b).astype(BF16)
    if bounded:
        half = 0.5 * BOUNDED_LOG2_DECAY
        shrink = jnp.exp2(b + half).astype(BF16)
        grow = jnp.exp2(-b - half).astype(BF16)
        q_lvl = [gq * shrink, gq * grow]
        phis = [grow, shrink]
    else:
        psis = [-jnp.abs(b - _row_of_block(b, 2 * c, c - 1)) for c in (32, 16, 8, 4)]
        la_next = pltpu.roll(la, tb - 1, 0)
        la_prev = pltpu.roll(la, 1, 0)
        r4 = row & 3
        psis.append(jnp.where(r4 == 0, la_next,
                              jnp.where(r4 == 1, 0.0, jnp.where(r4 == 2, la, la + la_prev))))
        psis.append(jnp.where((row & 1) == 1, la, 0.0))
        phis = [jnp.exp2(p).astype(BF16) for p in psis]
        q_lvl = [gq] + [gq * phi for phi in phis]
        phis = [None] + phis
    yield

    lane = lax.broadcasted_iota(jnp.int32, (span, LANES), 1)
    low_half = lane < GLA_DK
    zero_k = jnp.zeros((span, LANES), BF16)
    zero_s = jnp.zeros((GLA_DV, LANES), BF16)
    t_idx = lax.broadcasted_iota(jnp.int32, (span, GLA_HEADS * span), 0)
    s_idx = lax.broadcasted_iota(jnp.int32, (span, GLA_HEADS * span), 1) & (span - 1)
    t_xor_s = t_idx ^ s_idx
    lane_s = lax.broadcasted_iota(jnp.int32, (GLA_DV, LANES), 1)
    low_half_s = lane_s < GLA_DK

    def lane_tiles(piece, tile, zero, n_tiles):
        return jnp.concatenate([piece if j == tile else zero for j in range(n_tiles)], axis=1)

    for ch in range(tb // span):
        rows = slice(ch * span, (ch + 1) * span)
        orow = slice(r0 + ch * span, r0 + (ch + 1) * span)
        k_heads = [jnp.where(low_half if h % 2 == 0 else ~low_half,
                             gk[rows, (h // 2) * LANES:(h // 2 + 1) * LANES], zero_k)
                   for h in range(GLA_HEADS)]
        a_lvl = []
        for lvl in range(len(q_lvl)):
            pieces = []
            for h in range(GLA_HEADS):
                tile = h // 2
                kh = k_heads[h]
                if phis[lvl] is not None:
                    kh = kh * phis[lvl][rows, tile * LANES:(tile + 1) * LANES]
                pieces.append(lane_tiles(kh, tile, zero_k, 2))
            k_stack = jnp.concatenate(pieces, axis=0)
            a_lvl.append(_dot_nt(q_lvl[lvl][rows], k_stack))
        if bounded:
            same_chunk = (t_idx // CHUNK) == (s_idx // CHUNK)
            a = jnp.where(t_idx >= s_idx, a_lvl[0], jnp.where(same_chunk, a_lvl[1], 0.0))
        else:
            a = a_lvl[0]
            for i, c in enumerate((1, 2, 4, 8, 16, 32)):
                a = jnp.where(t_xor_s >= c, a_lvl[len(a_lvl) - 1 - i], a)
        a = a.astype(BF16)
        gv = slab[orow, OFF_GV:OFF_GV + GLA_WIDTH]
        v_bd = jnp.concatenate(
            [lane_tiles(gv[:, h * GLA_DV:(h + 1) * GLA_DV], h, jnp.zeros((span, GLA_DV), BF16), GLA_HEADS)
             for h in range(GLA_HEADS)], axis=0)
        s_bd = jnp.concatenate(
            [lane_tiles(g_state[h].astype(BF16), h // 2, zero_s, 2) for h in range(GLA_HEADS)],
            axis=0)
        o_all = _dot(a, v_bd) + _dot_nt(q_state[rows], s_bd)
        decay = jnp.exp2(b[(ch + 1) * span - 1:(ch + 1) * span, :])
        for h in range(GLA_HEADS):
            tile = h // 2
            upd = _dot_tn(gv[:, h * GLA_DV:(h + 1) * GLA_DV], k_state[rows, tile * LANES:(tile + 1) * LANES])
            upd = jnp.where(low_half_s if h % 2 == 0 else ~low_half_s, upd, 0.0)
            g_state[h] = g_state[h] * decay[:, tile * LANES:(tile + 1) * LANES] + upd
        for h in range(GLA_HEADS):
            o = o_all[:, h * GLA_DV:(h + 1) * GLA_DV]
            o = o * lax.rsqrt(jnp.mean(o * o, axis=-1, keepdims=True) + LN_EPS)
            gg = slab[orow, OFF_GG + h * GLA_DV:OFF_GG + (h + 1) * GLA_DV].astype(F32)
            o = o * gnw_ref[:, h * GLA_DV:(h + 1) * GLA_DV] * _silu(gg)
            out_ref[0, orow, RET_WIDTH + h * GLA_DV:RET_WIDTH + (h + 1) * GLA_DV] = o.astype(out_ref.dtype)
        yield


def _interleave(gen_a, n_a, gen_b, n_b):
    done_a = done_b = 0
    while done_a < n_a or done_b < n_b:
        if done_b >= n_b or (done_a < n_a and (done_a + 0.5) * n_b <= (done_b + 0.5) * n_a):
            next(gen_a)
            done_a += 1
        else:
            next(gen_b)
            done_b += 1
    for g in (gen_a, gen_b):
        assert next(g, "done") == "done", "piece count mismatch"


def _projmix_kernel(x_ref, mod_ref, cos_ref, sin_ref, w_ref, gw_ref, gb_ref,
                    dmask_ref, qdec_ref, kdec_ref, csum_ref, rnw_ref, gnw_ref,
                    wo_ref, w1_ref, w2_ref, c_ref, wa_ref, ba_ref,
                    out_ref, wo_bf_ref, w1_bf_ref, w2_bf_ref, late_mod_ref,
                    w_bf, w_tail, slab2, la2, bounded_ref, r_state, g_state,
                    *, tb, blocks_per_seq, n_blocks, block_decay):
    j = pl.program_id(0)
    n_proj = 2 + SLAB // MXU_WIDTH
    n_mix = {bounded: (tb // MIX_ROWS) * (RET_HEADS + 1 + MIX_ROWS // (BOUNDED_SPAN if bounded else CHUNK))
             for bounded in (True, False)}

    @pl.when(j == 0)
    def _():
        for g in range(0, SLAB, MXU_WIDTH):
            w_bf[:, g:g + MXU_WIDTH] = w_ref[g:g + MXU_WIDTH, :].T.astype(BF16)
        w_tail[...] = w_ref[SLAB:SLAB + GATE_RANK, :].astype(BF16)
        bounded_ref[1] = 0

    def side_jobs():
        wo_bf_ref[...] = wo_ref[...].astype(BF16)
        w1_bf_ref[...] = w1_ref[...].astype(BF16)
        w2_bf_ref[...] = w2_ref[...].astype(BF16)
        late_mod_ref[...] = (_dot(_silu(c_ref[...]).astype(BF16), wa_ref[...].astype(BF16))
                             + ba_ref[...])

    @pl.when((j - 1) % blocks_per_seq == 0)
    def _():
        r_state[...] = jnp.zeros_like(r_state)
        g_state[...] = jnp.zeros_like(g_state)

    def project(buf):
        return _proj_stage(x_ref, mod_ref, cos_ref, sin_ref, w_bf, w_tail, gw_ref, gb_ref,
                           slab2.at[buf], la2.at[buf], bounded_ref, buf)

    def mix(buf, bounded):
        for r0 in range(0, tb, MIX_ROWS):
            yield from _mix_stage(slab2.at[buf], la2.at[buf], r0, dmask_ref, qdec_ref, kdec_ref,
                                  csum_ref, rnw_ref, gnw_ref, out_ref, r_state, g_state, block_decay,
                                  bounded)

    @pl.when(j == 0)
    def _():
        side_jobs()
        for _ in project(0):
            pass

    steady = (j > 0) & (j < n_blocks)
    bounded = bounded_ref[1 - j % 2] == 1
    for parity in (0, 1):
        @pl.when(steady & bounded & (j % 2 == parity))
        def _():
            side_jobs()
            _interleave(project(parity), n_proj, mix(1 - parity, True), n_mix[True])

    @pl.when(steady & jnp.logical_not(bounded))
    def _():
        side_jobs()
        _interleave(project(j % 2), n_proj, mix(1 - j % 2, False), n_mix[False])

    @pl.when(j == n_blocks)
    def _():
        side_jobs()
        for _ in mix((n_blocks - 1) % 2, False):
            pass


def _layer_row(v):
    return v.reshape(v.shape[0], 1, v.shape[1])


def _projmix(x, mod, cos_t, sin_t, w_in, gate_w, gate_b, ret_norm_w, gla_norm_w,
             w_out, w_ff1, w_ff2, c, w_ada, b_ada, layer, tb):
    bsz, s, d = x.shape
    d_ff = w_ff1.shape[2]
    bps = s // tb
    nblk = bsz * bps
    late_cols = (N_MOD - N_EARLY_MOD) * d // nblk
    late_first = N_EARLY_MOD * d // late_cols
    dmask, qdec, kdec, block_decay = _retention_tables(MIX_ROWS)
    csum = _cumsum_matrix(MIX_ROWS)
    const2 = lambda j: (0, 0)
    const3 = lambda j: (0, 0, 0)
    of_layer = lambda j: (layer, 0, 0)
    proj_blk = lambda j: jnp.minimum(j, nblk - 1)
    mix_blk = lambda j: jnp.maximum(j - 1, 0)
    single = dict(pipeline_mode=pl.Buffered(1))
    wo_rows, w1_rows, w2_rows = d // nblk, d // nblk, d_ff // nblk
    return pl.pallas_call(
        functools.partial(_projmix_kernel, tb=tb, blocks_per_seq=bps, n_blocks=nblk,
                          block_decay=block_decay),
        grid=(nblk + 1,),
        in_specs=[pl.BlockSpec((1, tb, d), lambda j: (proj_blk(j) // bps, proj_blk(j) % bps, 0)),
                  pl.BlockSpec((1, N_EARLY_MOD, d), lambda j: (proj_blk(j) // bps, 0, 0)),
                  pl.BlockSpec((tb, RET_D), lambda j: (proj_blk(j) % bps, 0)),
                  pl.BlockSpec((tb, RET_D), lambda j: (proj_blk(j) % bps, 0)),
                  pl.BlockSpec((None, SLAB + GATE_RANK, d), of_layer, **single),
                  pl.BlockSpec((None, GATE_RANK, GLA_KEY_WIDTH), of_layer, **single),
                  pl.BlockSpec((None, 1, GLA_KEY_WIDTH), of_layer, **single),
                  pl.BlockSpec(dmask.shape, const3, **single),
                  pl.BlockSpec(qdec.shape, const3, **single),
                  pl.BlockSpec(kdec.shape, const3, **single),
                  pl.BlockSpec(csum.shape, const2, **single),
                  pl.BlockSpec((None, 1, RET_WIDTH), of_layer, **single),
                  pl.BlockSpec((None, 1, GLA_WIDTH), of_layer, **single),
                  pl.BlockSpec((None, wo_rows, d), lambda j: (layer, proj_blk(j), 0)),
                  pl.BlockSpec((None, w1_rows, d_ff), lambda j: (layer, proj_blk(j), 0)),
                  pl.BlockSpec((None, w2_rows, d), lambda j: (layer, proj_blk(j), 0)),
                  pl.BlockSpec((bsz, d), const2, **single),
                  pl.BlockSpec((None, d, late_cols), lambda j: (layer, 0, late_first + proj_blk(j))),
                  pl.BlockSpec((None, 1, late_cols), lambda j: (layer, 0, late_first + proj_blk(j)))],
        out_specs=[pl.BlockSpec((1, tb, D_MIX), lambda j: (mix_blk(j) // bps, mix_blk(j) % bps, 0)),
                   pl.BlockSpec((wo_rows, d), lambda j: (proj_blk(j), 0)),
                   pl.BlockSpec((w1_rows, d_ff), lambda j: (proj_blk(j), 0)),
                   pl.BlockSpec((w2_rows, d), lambda j: (proj_blk(j), 0)),
                   pl.BlockSpec((bsz, late_cols), lambda j: (0, proj_blk(j)))],
        out_shape=[jax.ShapeDtypeStruct((bsz, s, D_MIX), BF16),
                   jax.ShapeDtypeStruct((d, d), BF16),
                   jax.ShapeDtypeStruct((d, d_ff), BF16),
                   jax.ShapeDtypeStruct((d_ff, d), BF16),
                   jax.ShapeDtypeStruct((bsz, (N_MOD - N_EARLY_MOD) * d), F32)],
        scratch_shapes=[pltpu.VMEM((d, SLAB), BF16), pltpu.VMEM((GATE_RANK, d), BF16),
                        pltpu.VMEM((2, tb, SLAB), BF16), pltpu.VMEM((2, tb, GLA_KEY_WIDTH), F32),
                        pltpu.SMEM((2,), jnp.int32),
                        pltpu.VMEM((RET_HEADS, RET_D, RET_D), F32),
                        pltpu.VMEM((GLA_HEADS, GLA_DV, LANES), F32)],
        compiler_params=pltpu.CompilerParams(
            dimension_semantics=("arbitrary",), vmem_limit_bytes=VMEM_LIMIT),
        name="projmix",
    )(x, mod, cos_t, sin_t, jnp.swapaxes(w_in, 1, 2), gate_w, _layer_row(gate_b),
      jnp.asarray(dmask), jnp.asarray(qdec), jnp.asarray(kdec), jnp.asarray(csum, BF16),
      _layer_row(ret_norm_w), _layer_row(gla_norm_w), w_out, w_ff1, w_ff2,
      c, w_ada, _layer_row(b_ada))


def _ffn_kernel(x_ref, mix_ref, mod_ref, wo_ref, w1_ref, w2_ref, ln1w_ref, ln1b_ref,
                ln2w_ref, ln2b_ref, out_ref, h_ref, *, alpha):
    seq = pl.ds(pl.program_id(0), 1)
    d = x_ref.shape[2]
    gate1 = mod_ref[seq, 0:d]
    shift2 = mod_ref[seq, d:2 * d]
    scale2 = mod_ref[seq, 2 * d:3 * d]
    gate2 = mod_ref[seq, 3 * d:4 * d]
    d_ff = w1_ref.shape[1]
    n_groups = 4
    group = d_ff // n_groups
    tb = x_ref.shape[1]
    halves = (slice(0, tb // 2), slice(tb // 2, tb))

    def chain(rows, wait):
        m = _dot(mix_ref[0, rows, :], wo_ref[...])
        yield
        x1 = (_ln(alpha * x_ref[0, rows, :] + gate1 * m, wait.get("ln1")) * ln1w_ref[...]
              + ln1b_ref[...])
        u2f = _ln(x1) * (1.0 + scale2) + shift2
        wait["ln1_done"] = u2f[u2f.shape[0] - 8:, u2f.shape[1] - LANES:]
        u2 = u2f.astype(BF16)
        yield
        for j in range(n_groups):
            hj = jnp.maximum(_dot(u2, w1_ref[:, j * group:(j + 1) * group]), 0.0)
            h_ref[rows, j * group:(j + 1) * group] = (hj * hj).astype(BF16)
            yield
        f = _dot(h_ref[rows, :], w2_ref[...])
        yield
        out_ref[0, rows, :] = _ln(alpha * x1 + gate2 * f) * ln2w_ref[...] + ln2b_ref[...]
        yield

    wait_a, wait_b = {}, {}
    half_a, half_b = chain(halves[0], wait_a), chain(halves[1], wait_b)
    next(half_a)
    next(half_b)
    next(half_a)
    wait_b["ln1"] = wait_a["ln1_done"]
    n_rest = n_groups + 2
    _interleave(half_a, n_rest, half_b, n_rest + 1)


def _ffn(x, mixed, mod, w_out, w1, w2, ln1_w, ln1_b, ln2_w, ln2_b, layer, alpha, tb):
    bsz, s, d = x.shape
    nb = s // tb
    d_ff = w1.shape[1]
    const2 = lambda b, i: (0, 0)
    of_layer = lambda b, i: (layer, 0, 0)
    single = dict(pipeline_mode=pl.Buffered(1))
    return pl.pallas_call(
        functools.partial(_ffn_kernel, alpha=alpha),
        grid=(bsz, nb),
        in_specs=[pl.BlockSpec((1, tb, d), lambda b, i: (b, i, 0)),
                  pl.BlockSpec((1, tb, d), lambda b, i: (b, i, 0)),
                  pl.BlockSpec(mod.shape, const2, **single),
                  pl.BlockSpec((d, d), const2, **single),
                  pl.BlockSpec((d, d_ff), const2, **single),
                  pl.BlockSpec((d_ff, d), const2, **single),
                  pl.BlockSpec((None, 1, d), of_layer, **single),
                  pl.BlockSpec((None, 1, d), of_layer, **single),
                  pl.BlockSpec((None, 1, d), of_layer, **single),
                  pl.BlockSpec((None, 1, d), of_layer, **single)],
        out_specs=pl.BlockSpec((1, tb, d), lambda b, i: (b, i, 0)),
        out_shape=jax.ShapeDtypeStruct((bsz, s, d), x.dtype),
        scratch_shapes=[pltpu.VMEM((tb, d_ff), BF16)],
        compiler_params=pltpu.CompilerParams(
            dimension_semantics=("parallel", "parallel"), vmem_limit_bytes=VMEM_LIMIT),
        name="ffn",
    )(x, mixed, mod, w_out, w1, w2, _layer_row(ln1_w), _layer_row(ln1_b), _layer_row(ln2_w),
      _layer_row(ln2_b))


def kernel(x, c, w_ada, b_ada, w_in, ret_norm_w, gla_gate_w, gla_gate_b, gla_norm_w,
           w_out, ln1_w, ln1_b, w_ff1, w_ff2, ln2_w, ln2_b):
    depth = w_in.shape[0]
    bsz, s, d = x.shape
    alpha = (2.0 * depth) ** 0.25
    tb = min(PROJMIX_ROWS, s)
    cos_t, sin_t = (jnp.asarray(t) for t in _rotary_tables(s))
    for l in range(depth):
        mod = _adaln(c, w_ada, b_ada, l)
        mixed, wo_bf, w1_bf, w2_bf, late_mod = _projmix(
            x, mod, cos_t, sin_t, w_in, gla_gate_w, gla_gate_b, ret_norm_w, gla_norm_w,
            w_out, w_ff1, w_ff2, c, w_ada, b_ada, l, tb)
        x = _ffn(x, mixed, late_mod, wo_bf, w1_bf, w2_bf, ln1_w, ln1_b, ln2_w, ln2_b, l, alpha,
                 min(FFN_ROWS, s))
    return x
```

```python
import functools
import math

import numpy as np
import jax
import jax.numpy as jnp
from jax import lax
from jax.experimental import pallas as pl
from jax.experimental.pallas import tpu as pltpu

CHUNK = 64
RET_HEADS = 4
RET_D = 128
RET_WIDTH = RET_HEADS * RET_D
GLA_HEADS = 4
GLA_DK = 64
GLA_DV = 128
GLA_KEY_WIDTH = GLA_HEADS * GLA_DK
GLA_WIDTH = GLA_HEADS * GLA_DV
D_MIX = RET_WIDTH + GLA_WIDTH
GATE_RANK = 16
GATE_TAU = 16.0
ROPE_BASE = 10000.0
LN_EPS = 1e-5
LANES = 128
MXU_WIDTH = 256
N_MOD = 6
N_EARLY_MOD = 2
PROJMIX_ROWS = 512
FFN_ROWS = 512
SLAB = 4 * RET_WIDTH + 2 * GLA_KEY_WIDTH + 2 * GLA_WIDTH
OFF_RQ, OFF_RK, OFF_RV, OFF_RG = 0, 512, 1024, 1536
OFF_GQ, OFF_GK, OFF_GV, OFF_GG = 2048, 2304, 2560, 3072
MIX_ROWS = 256
BOUNDED_LOG2_DECAY = 96.0
BOUNDED_LOG2_QK = 24.0
BOUNDED_SPAN = 2 * CHUNK
VMEM_LIMIT = 56 * 1024 * 1024
LOG2E = math.log2(math.e)

BF16 = jnp.bfloat16
F32 = jnp.float32


def _dot(a, b):
    return jnp.dot(a, b, preferred_element_type=F32)


def _dot_nt(a, b):
    return lax.dot_general(a, b, (((1,), (1,)), ((), ())), preferred_element_type=F32)


def _dot_tn(a, b):
    return lax.dot_general(a, b, (((0,), (0,)), ((), ())), preferred_element_type=F32)


def _zero_of(v):
    bits = lax.shift_right_logical(pltpu.bitcast(v, jnp.uint32), jnp.uint32(32))
    return pltpu.bitcast(bits, F32)


def _ln(x, not_before=None):
    mu = jnp.mean(x, axis=-1, keepdims=True)
    if not_before is not None:
        mu = mu + _zero_of(not_before)[0:1, 0:1]
    xc = x - mu
    var = jnp.mean(xc * xc, axis=-1, keepdims=True)
    return xc * lax.rsqrt(var + LN_EPS)


def _silu(x):
    return x * (1.0 / (1.0 + jnp.exp(-x)))


def _adaln_kernel(c_ref, w_ref, b_ref, o_ref, *, per_step):
    c = c_ref[...]
    d = c.shape[1]
    val = _dot(_silu(c).astype(BF16), w_ref[...].astype(BF16)) + b_ref[...]
    for t in range(per_step):
        o_ref[:, pl.ds(pl.program_id(0) * per_step + t, 1), :] = val[:, None, t * d:(t + 1) * d]


def _adaln(c, w_ada, b_ada, layer):
    bsz, d = c.shape
    depth, _, n = w_ada.shape
    return pl.pallas_call(
        functools.partial(_adaln_kernel, per_step=N_EARLY_MOD),
        grid=(1,),
        in_specs=[pl.BlockSpec((bsz, d), lambda j: (0, 0)),
                  pl.BlockSpec((None, d, N_EARLY_MOD * d), lambda j: (layer, 0, 0)),
                  pl.BlockSpec((None, 1, N_EARLY_MOD * d), lambda j: (layer, 0, 0))],
        out_specs=pl.BlockSpec((bsz, N_EARLY_MOD, d), lambda j: (0, 0, 0)),
        out_shape=jax.ShapeDtypeStruct((bsz, N_EARLY_MOD, d), F32),
        compiler_params=pltpu.CompilerParams(vmem_limit_bytes=VMEM_LIMIT),
        name="adaln",
    )(c, w_ada, b_ada.reshape(depth, 1, n))


def _rotary_tables(s):
    half = RET_D // 2
    inv = 1.0 / (ROPE_BASE ** (np.arange(half, dtype=np.float64) / (half - 1)))
    ang = np.arange(s, dtype=np.float64)[:, None] * inv[None, :]
    cos = np.concatenate([np.cos(ang), np.cos(ang)], axis=1)
    sin = np.concatenate([-np.sin(ang), np.sin(ang)], axis=1)
    return cos.astype(np.float32), sin.astype(np.float32)


def _retention_tables(tb):
    log_gamma = np.log(1.0 - 2.0 ** (-5.0 - np.arange(RET_HEADS, dtype=np.float64)))
    idx = np.arange(tb)
    dist = np.abs(idx[:, None] - idx[None, :])
    visible = (idx[None, :] // CHUNK) <= (idx[:, None] // CHUNK)
    scale = RET_D ** -0.5
    dmask = np.exp(log_gamma[:, None, None] * dist[None]) * visible[None] * scale
    qdec = np.exp(log_gamma[:, None] * (idx[None, :] + 1.0)) * scale
    kdec = np.exp(log_gamma[:, None] * (tb - 1.0 - idx[None, :]))
    qdec = np.broadcast_to(qdec[:, :, None], (RET_HEADS, tb, RET_D))
    kdec = np.broadcast_to(kdec[:, :, None], (RET_HEADS, tb, RET_D))
    block_decay = [float(np.exp(lg * tb)) for lg in log_gamma]
    return (dmask.astype(np.float32), np.ascontiguousarray(qdec, np.float32),
            np.ascontiguousarray(kdec, np.float32), block_decay)


def _cumsum_matrix(tb):
    t = np.arange(tb)
    same_chunk = (t[:, None] // CHUNK) == (t[None, :] // CHUNK)
    return (same_chunk & (t[None, :] <= t[:, None])).astype(np.float32)


def _proj_stage(x_ref, mod_ref, cos_ref, sin_ref, w_ref, wtail_ref, gw_ref, gb_ref, slab, la_out,
                bounded_ref, slot):
    x = x_ref[0]
    shift1 = mod_ref[0, 0:1, :]
    scale1 = mod_ref[0, 1:2, :]
    ub = (_ln(x) * (1.0 + scale1) + shift1).astype(BF16)
    yield
    cos = cos_ref[...]
    sin = sin_ref[...]
    group = MXU_WIDTH
    qk_max = jnp.float32(0.0)
    for j in range(SLAB // group):
        p = _dot(ub, w_ref[:, j * group:(j + 1) * group])
        if OFF_GQ <= j * group < OFF_GV:
            qk_max = jnp.maximum(qk_max, jnp.max(jnp.abs(p)))
        if j * group < OFF_RV:
            for h in range(group // RET_D):
                xh = p[:, h * RET_D:(h + 1) * RET_D]
                rot = xh * cos + pltpu.roll(xh, RET_D // 2, 1) * sin
                slab[:, j * group + h * RET_D:j * group + (h + 1) * RET_D] = rot.astype(BF16)
        else:
            slab[:, j * group:(j + 1) * group] = p.astype(BF16)
        yield
    glr = _dot_nt(ub, wtail_ref[...])
    logit = _dot(glr.astype(BF16), gw_ref[...].astype(BF16)) + gb_ref[...]
    log_sig = jnp.minimum(logit, 0.0) - jnp.log(1.0 + jnp.exp(-jnp.abs(logit)))
    la = log_sig * (LOG2E / GATE_TAU)
    la_out[...] = la
    bounded = (jnp.min(la) * BOUNDED_SPAN >= -BOUNDED_LOG2_DECAY) & (qk_max <= 2.0 ** BOUNDED_LOG2_QK)
    bounded_ref[slot] = bounded.astype(jnp.int32)
    yield


def _row_of_block(b, block, row):
    n = b.shape[0] // block
    b3 = b.reshape(n, block, b.shape[1])
    return jnp.broadcast_to(b3[:, row:row + 1, :], b3.shape).reshape(b.shape)


def _mix_stage(slab, la_ref, r0, dmask_ref, qdec_ref, kdec_ref, csum_ref, rnw_ref, gnw_ref,
               out_ref, r_state, g_state, block_decay, bounded):
    tb = MIX_ROWS
    blk = slice(r0, r0 + tb)
    for h in range(RET_HEADS):
        q = slab[blk, OFF_RQ + h * RET_D:OFF_RQ + (h + 1) * RET_D]
        k = slab[blk, OFF_RK + h * RET_D:OFF_RK + (h + 1) * RET_D]
        v = slab[blk, OFF_RV + h * RET_D:OFF_RV + (h + 1) * RET_D]
        g = slab[blk, OFF_RG + h * RET_D:OFF_RG + (h + 1) * RET_D].astype(F32)
        p = (_dot_nt(q, k) * dmask_ref[h]).astype(BF16)
        r_prev = r_state[h]
        o = _dot(p, v) + _dot(q, r_prev.astype(BF16)) * qdec_ref[h]
        kd = (k.astype(F32) * kdec_ref[h]).astype(BF16)
        r_state[h] = block_decay[h] * r_prev + _dot_tn(kd, v)
        o = o - jnp.mean(o, axis=-1, keepdims=True)
        o = o * lax.rsqrt(jnp.mean(o * o, axis=-1, keepdims=True) + LN_EPS)
        o = o * rnw_ref[:, h * RET_D:(h + 1) * RET_D] * _silu(g)
        out_ref[0, blk, h * RET_D:(h + 1) * RET_D] = o.astype(out_ref.dtype)
        yield

    la = la_ref[blk, :]
    la_hi = la.astype(BF16)
    la_lo = (la - la_hi.astype(F32)).astype(BF16)
    csum = csum_ref[...]
    b = _dot(csum, la_hi) + _dot(csum, la_lo)
    gq = slab[blk, OFF_GQ:OFF_GQ + GLA_KEY_WIDTH] * jnp.asarray(GLA_DK ** -0.5, BF16)
    gk = slab[blk, OFF_GK:OFF_GK + GLA_KEY_WIDTH]
    row = lax.broadcasted_iota(jnp.int32, (tb, GLA_KEY_WIDTH), 0)
    span = BOUNDED_SPAN if bounded else CHUNK
    if bounded:
        for c in range(1, span // CHUNK):
            carry = _row_of_block(b, span, c * CHUNK - 1)
            b = b + jnp.where((row & (span - 1)) >= c * CHUNK, carry, 0.0)
    q_state = gq * jnp.exp2(b).astype(BF16)
    k_state = gk * jnp.exp2(_row_of_block(b, span, span - 1) - b).astype(BF16)
    if bounded:
        half = 0.5 * BOUNDED_LOG2_DECAY
        shrink = jnp.exp2(b + half).astype(BF16)
        grow = jnp.exp2(-b - half).astype(BF16)
        q_lvl = [gq * shrink, gq * grow]
        phis = [grow, shrink]
    else:
        psis = [-jnp.abs(b - _row_of_block(b, 2 * c, c - 1)) for c in (32, 16, 8, 4)]
        la_next = pltpu.roll(la, tb - 1, 0)
        la_prev = pltpu.roll(la, 1, 0)
        r4 = row & 3
        psis.append(jnp.where(r4 == 0, la_next,
                              jnp.where(r4 == 1, 0.0, jnp.where(r4 == 2, la, la + la_prev))))
        psis.append(jnp.where((row & 1) == 1, la, 0.0))
        phis = [jnp.exp2(p).astype(BF16) for p in psis]
        q_lvl = [gq] + [gq * phi for phi in phis]
        phis = [None] + phis
    yield

    lane = lax.broadcasted_iota(jnp.int32, (span, LANES), 1)
    low_half = lane < GLA_DK
    zero_k = jnp.zeros((span, LANES), BF16)
    zero_s = jnp.zeros((GLA_DV, LANES), BF16)
    t_idx = lax.broadcasted_iota(jnp.int32, (span, GLA_HEADS * span), 0)
    s_idx = lax.broadcasted_iota(jnp.int32, (span, GLA_HEADS * span), 1) & (span - 1)
    t_xor_s = t_idx ^ s_idx
    lane_s = lax.broadcasted_iota(jnp.int32, (GLA_DV, LANES), 1)
    low_half_s = lane_s < GLA_DK

    def lane_tiles(piece, tile, zero, n_tiles):
        return jnp.concatenate([piece if j == tile else zero for j in range(n_tiles)], axis=1)

    for ch in range(tb // span):
        rows = slice(ch * span, (ch + 1) * span)
        orow = slice(r0 + ch * span, r0 + (ch + 1) * span)
        k_heads = [jnp.where(low_half if h % 2 == 0 else ~low_half,
                             gk[rows, (h // 2) * LANES:(h // 2 + 1) * LANES], zero_k)
                   for h in range(GLA_HEADS)]
        a_lvl = []
        for lvl in range(len(q_lvl)):
            pieces = []
            for h in range(GLA_HEADS):
                tile = h // 2
                kh = k_heads[h]
                if phis[lvl] is not None:
                    kh = kh * phis[lvl][rows, tile * LANES:(tile + 1) * LANES]
                pieces.append(lane_tiles(kh, tile, zero_k, 2))
            k_stack = jnp.concatenate(pieces, axis=0)
            a_lvl.append(_dot_nt(q_lvl[lvl][rows], k_stack))
        if bounded:
            same_chunk = (t_idx // CHUNK) == (s_idx // CHUNK)
            a = jnp.where(t_idx >= s_idx, a_lvl[0], jnp.where(same_chunk, a_lvl[1], 0.0))
        else:
            a = a_lvl[0]
            for i, c in enumerate((1, 2, 4, 8, 16, 32)):
                a = jnp.where(t_xor_s >= c, a_lvl[len(a_lvl) - 1 - i], a)
        a = a.astype(BF16)
        gv = slab[orow, OFF_GV:OFF_GV + GLA_WIDTH]
        v_bd = jnp.concatenate(
            [lane_tiles(gv[:, h * GLA_DV:(h + 1) * GLA_DV], h, jnp.zeros((span, GLA_DV), BF16), GLA_HEADS)
             for h in range(GLA_HEADS)], axis=0)
        s_bd = jnp.concatenate(
            [lane_tiles(g_state[h].astype(BF16), h // 2, zero_s, 2) for h in range(GLA_HEADS)],
            axis=0)
        o_all = _dot(a, v_bd) + _dot_nt(q_state[rows], s_bd)
        decay = jnp.exp2(b[(ch + 1) * span - 1:(ch + 1) * span, :])
        for h in range(GLA_HEADS):
            tile = h // 2
            upd = _dot_tn(gv[:, h * GLA_DV:(h + 1) * GLA_DV], k_state[rows, tile * LANES:(tile + 1) * LANES])
            upd = jnp.where(low_half_s if h % 2 == 0 else ~low_half_s, upd, 0.0)
            g_state[h] = g_state[h] * decay[:, tile * LANES:(tile + 1) * LANES] + upd
        for h in range(GLA_HEADS):
            o = o_all[:, h * GLA_DV:(h + 1) * GLA_DV]
            o = o * lax.rsqrt(jnp.mean(o * o, axis=-1, keepdims=True) + LN_EPS)
            gg = slab[orow, OFF_GG + h * GLA_DV:OFF_GG + (h + 1) * GLA_DV].astype(F32)
            o = o * gnw_ref[:, h * GLA_DV:(h + 1) * GLA_DV] * _silu(gg)
            out_ref[0, orow, RET_WIDTH + h * GLA_DV:RET_WIDTH + (h + 1) * GLA_DV] = o.astype(out_ref.dtype)
        yield


def _interleave(gen_a, n_a, gen_b, n_b):
    done_a = done_b = 0
    while done_a < n_a or done_b < n_b:
        if done_b >= n_b or (done_a < n_a and (done_a + 0.5) * n_b <= (done_b + 0.5) * n_a):
            next(gen_a)
            done_a += 1
        else:
            next(gen_b)
            done_b += 1
    for g in (gen_a, gen_b):
        assert next(g, "done") == "done", "piece count mismatch"


def _projmix_kernel(x_ref, mod_ref, cos_ref, sin_ref, w_ref, gw_ref, gb_ref,
                    dmask_ref, qdec_ref, kdec_ref, csum_ref, rnw_ref, gnw_ref,
                    wo_ref, w1_ref, w2_ref, c_ref, wa_ref, ba_ref,
                    out_ref, wo_bf_ref, w1_bf_ref, w2_bf_ref, late_mod_ref,
                    w_bf, w_tail, slab2, la2, bounded_ref, r_state, g_state,
                    *, tb, blocks_per_seq, n_blocks, block_decay):
    j = pl.program_id(0)
    n_proj = 2 + SLAB // MXU_WIDTH
    n_mix = {bounded: (tb // MIX_ROWS) * (RET_HEADS + 1 + MIX_ROWS // (BOUNDED_SPAN if bounded else CHUNK))
             for bounded in (True, False)}

    @pl.when(j == 0)
    def _():
        for g in range(0, SLAB, MXU_WIDTH):
            w_bf[:, g:g + MXU_WIDTH] = w_ref[g:g + MXU_WIDTH, :].T.astype(BF16)
        w_tail[...] = w_ref[SLAB:SLAB + GATE_RANK, :].astype(BF16)
        bounded_ref[1] = 0

    def side_jobs():
        wo_bf_ref[...] = wo_ref[...].astype(BF16)
        w1_bf_ref[...] = w1_ref[...].astype(BF16)
        w2_bf_ref[...] = w2_ref[...].astype(BF16)
        late_mod_ref[...] = (_dot(_silu(c_ref[...]).astype(BF16), wa_ref[...].astype(BF16))
                             + ba_ref[...])

    @pl.when((j - 1) % blocks_per_seq == 0)
    def _():
        r_state[...] = jnp.zeros_like(r_state)
        g_state[...] = jnp.zeros_like(g_state)

    def project(buf):
        return _proj_stage(x_ref, mod_ref, cos_ref, sin_ref, w_bf, w_tail, gw_ref, gb_ref,
                           slab2.at[buf], la2.at[buf], bounded_ref, buf)

    def mix(buf, bounded):
        for r0 in range(0, tb, MIX_ROWS):
            yield from _mix_stage(slab2.at[buf], la2.at[buf], r0, dmask_ref, qdec_ref, kdec_ref,
                                  csum_ref, rnw_ref, gnw_ref, out_ref, r_state, g_state, block_decay,
                                  bounded)

    @pl.when(j == 0)
    def _():
        side_jobs()
        for _ in project(0):
            pass

    steady = (j > 0) & (j < n_blocks)
    bounded = bounded_ref[1 - j % 2] == 1
    for parity in (0, 1):
        @pl.when(steady & bounded & (j % 2 == parity))
        def _():
            side_jobs()
            _interleave(project(parity), n_proj, mix(1 - parity, True), n_mix[True])

    @pl.when(steady & jnp.logical_not(bounded))
    def _():
        side_jobs()
        _interleave(project(j % 2), n_proj, mix(1 - j % 2, False), n_mix[False])

    @pl.when(j == n_blocks)
    def _():
        side_jobs()
        for _ in mix((n_blocks - 1) % 2, False):
            pass


def _layer_row(v):
    return v.reshape(v.shape[0], 1, v.shape[1])


def _projmix(x, mod, cos_t, sin_t, w_in, gate_w, gate_b, ret_norm_w, gla_norm_w,
             w_out, w_ff1, w_ff2, c, w_ada, b_ada, layer, tb):
    bsz, s, d = x.shape
    d_ff = w_ff1.shape[2]
    bps = s // tb
    nblk = bsz * bps
    late_cols = (N_MOD - N_EARLY_MOD) * d // nblk
    late_first = N_EARLY_MOD * d // late_cols
    dmask, qdec, kdec, block_decay = _retention_tables(MIX_ROWS)
    csum = _cumsum_matrix(MIX_ROWS)
    const2 = lambda j: (0, 0)
    const3 = lambda j: (0, 0, 0)
    of_layer = lambda j: (layer, 0, 0)
    proj_blk = lambda j: jnp.minimum(j, nblk - 1)
    mix_blk = lambda j: jnp.maximum(j - 1, 0)
    single = dict(pipeline_mode=pl.Buffered(1))
    wo_rows, w1_rows, w2_rows = d // nblk, d // nblk, d_ff // nblk
    return pl.pallas_call(
        functools.partial(_projmix_kernel, tb=tb, blocks_per_seq=bps, n_blocks=nblk,
                          block_decay=block_decay),
        grid=(nblk + 1,),
        in_specs=[pl.BlockSpec((1, tb, d), lambda j: (proj_blk(j) // bps, proj_blk(j) % bps, 0)),
                  pl.BlockSpec((1, N_EARLY_MOD, d), lambda j: (proj_blk(j) // bps, 0, 0)),
                  pl.BlockSpec((tb, RET_D), lambda j: (proj_blk(j) % bps, 0)),
                  pl.BlockSpec((tb, RET_D), lambda j: (proj_blk(j) % bps, 0)),
                  pl.BlockSpec((None, SLAB + GATE_RANK, d), of_layer, **single),
                  pl.BlockSpec((None, GATE_RANK, GLA_KEY_WIDTH), of_layer, **single),
                  pl.BlockSpec((None, 1, GLA_KEY_WIDTH), of_layer, **single),
                  pl.BlockSpec(dmask.shape, const3, **single),
                  pl.BlockSpec(qdec.shape, const3, **single),
                  pl.BlockSpec(kdec.shape, const3, **single),
                  pl.BlockSpec(csum.shape, const2, **single),
                  pl.BlockSpec((None, 1, RET_WIDTH), of_layer, **single),
                  pl.BlockSpec((None, 1, GLA_WIDTH), of_layer, **single),
                  pl.BlockSpec((None, wo_rows, d), lambda j: (layer, proj_blk(j), 0)),
                  pl.BlockSpec((None, w1_rows, d_ff), lambda j: (layer, proj_blk(j), 0)),
                  pl.BlockSpec((None, w2_rows, d), lambda j: (layer, proj_blk(j), 0)),
                  pl.BlockSpec((bsz, d), const2, **single),
                  pl.BlockSpec((None, d, late_cols), lambda j: (layer, 0, late_first + proj_blk(j))),
                  pl.BlockSpec((None, 1, late_cols), lambda j: (layer, 0, late_first + proj_blk(j)))],
        out_specs=[pl.BlockSpec((1, tb, D_MIX), lambda j: (mix_blk(j) // bps, mix_blk(j) % bps, 0)),
                   pl.BlockSpec((wo_rows, d), lambda j: (proj_blk(j), 0)),
                   pl.BlockSpec((w1_rows, d_ff), lambda j: (proj_blk(j), 0)),
                   pl.BlockSpec((w2_rows, d), lambda j: (proj_blk(j), 0)),
                   pl.BlockSpec((bsz, late_cols), lambda j: (0, proj_blk(j)))],
        out_shape=[jax.ShapeDtypeStruct((bsz, s, D_MIX), BF16),
                   jax.ShapeDtypeStruct((d, d), BF16),
                   jax.ShapeDtypeStruct((d, d_ff), BF16),
                   jax.ShapeDtypeStruct((d_ff, d), BF16),
                   jax.ShapeDtypeStruct((bsz, (N_MOD - N_EARLY_MOD) * d), F32)],
        scratch_shapes=[pltpu.VMEM((d, SLAB), BF16), pltpu.VMEM((GATE_RANK, d), BF16),
                        pltpu.VMEM((2, tb, SLAB), BF16), pltpu.VMEM((2, tb, GLA_KEY_WIDTH), F32),
                        pltpu.SMEM((2,), jnp.int32),
                        pltpu.VMEM((RET_HEADS, RET_D, RET_D), F32),
                        pltpu.VMEM((GLA_HEADS, GLA_DV, LANES), F32)],
        compiler_params=pltpu.CompilerParams(
            dimension_semantics=("arbitrary",), vmem_limit_bytes=VMEM_LIMIT),
        name="projmix",
    )(x, mod, cos_t, sin_t, jnp.swapaxes(w_in, 1, 2), gate_w, _layer_row(gate_b),
      jnp.asarray(dmask), jnp.asarray(qdec), jnp.asarray(kdec), jnp.asarray(csum, BF16),
      _layer_row(ret_norm_w), _layer_row(gla_norm_w), w_out, w_ff1, w_ff2,
      c, w_ada, _layer_row(b_ada))


def _ffn_kernel(x_ref, mix_ref, mod_ref, wo_hbm, w1_hbm, w2_hbm, ln1w_ref, ln1b_ref,
                ln2w_ref, ln2b_ref, out_ref, wo_ref, w1_ref, w2_ref, h_ref, sems, *, alpha):
    seq = pl.ds(pl.program_id(0), 1)
    d = x_ref.shape[2]
    gate1 = mod_ref[seq, 0:d]
    shift2 = mod_ref[seq, d:2 * d]
    scale2 = mod_ref[seq, 2 * d:3 * d]
    gate2 = mod_ref[seq, 3 * d:4 * d]
    d_ff = w1_ref.shape[1]
    n_groups = 4
    group = d_ff // n_groups
    tb = x_ref.shape[1]
    halves = (slice(0, tb // 2), slice(tb // 2, tb))
    copies = [pltpu.make_async_copy(src, dst, sems.at[k])
              for k, (src, dst) in enumerate(((wo_hbm, wo_ref), (w1_hbm, w1_ref), (w2_hbm, w2_ref)))]

    def chain(rows, wait, arrive):
        arrive(0)
        m = _dot(mix_ref[0, rows, :], wo_ref[...])
        yield
        x1 = (_ln(alpha * x_ref[0, rows, :] + gate1 * m, wait.get("ln1")) * ln1w_ref[...]
              + ln1b_ref[...])
        u2f = _ln(x1) * (1.0 + scale2) + shift2
        wait["ln1_done"] = u2f[u2f.shape[0] - 8:, u2f.shape[1] - LANES:]
        u2 = u2f.astype(BF16)
        yield
        arrive(1)
        for j in range(n_groups):
            hj = jnp.maximum(_dot(u2, w1_ref[:, j * group:(j + 1) * group]), 0.0)
            h_ref[rows, j * group:(j + 1) * group] = (hj * hj).astype(BF16)
            yield
        arrive(2)
        f = _dot(h_ref[rows, :], w2_ref[...])
        yield
        out_ref[0, rows, :] = _ln(alpha * x1 + gate2 * f) * ln2w_ref[...] + ln2b_ref[...]
        yield

    def body(first_step):
        waited = set()

        def arrive(k):
            if first_step and k not in waited:
                copies[k].wait()
                waited.add(k)

        if first_step:
            for cp in copies:
                cp.start()
        wait_a, wait_b = {}, {}
        half_a, half_b = chain(halves[0], wait_a, arrive), chain(halves[1], wait_b, arrive)
        next(half_a)
        next(half_b)
        next(half_a)
        wait_b["ln1"] = wait_a["ln1_done"]
        n_rest = n_groups + 2
        _interleave(half_a, n_rest, half_b, n_rest + 1)
        assert not first_step or waited == {0, 1, 2}

    is_first = (pl.program_id(0) == 0) & (pl.program_id(1) == 0)

    @pl.when(is_first)
    def _():
        body(True)

    @pl.when(jnp.logical_not(is_first))
    def _():
        body(False)


def _ffn(x, mixed, mod, w_out, w1, w2, ln1_w, ln1_b, ln2_w, ln2_b, layer, alpha, tb):
    bsz, s, d = x.shape
    nb = s // tb
    d_ff = w1.shape[1]
    const2 = lambda b, i: (0, 0)
    of_layer = lambda b, i: (layer, 0, 0)
    single = dict(pipeline_mode=pl.Buffered(1))
    return pl.pallas_call(
        functools.partial(_ffn_kernel, alpha=alpha),
        grid=(bsz, nb),
        in_specs=[pl.BlockSpec((1, tb, d), lambda b, i: (b, i, 0)),
                  pl.BlockSpec((1, tb, d), lambda b, i: (b, i, 0)),
                  pl.BlockSpec(mod.shape, const2, **single),
                  pl.BlockSpec(memory_space=pl.ANY),
                  pl.BlockSpec(memory_space=pl.ANY),
                  pl.BlockSpec(memory_space=pl.ANY),
                  pl.BlockSpec((None, 1, d), of_layer, **single),
                  pl.BlockSpec((None, 1, d), of_layer, **single),
                  pl.BlockSpec((None, 1, d), of_layer, **single),
                  pl.BlockSpec((None, 1, d), of_layer, **single)],
        out_specs=pl.BlockSpec((1, tb, d), lambda b, i: (b, i, 0)),
        out_shape=jax.ShapeDtypeStruct((bsz, s, d), x.dtype),
        scratch_shapes=[pltpu.VMEM((d, d), BF16), pltpu.VMEM((d, d_ff), BF16),
                        pltpu.VMEM((d_ff, d), BF16), pltpu.VMEM((tb, d_ff), BF16),
                        pltpu.SemaphoreType.DMA((3,))],
        compiler_params=pltpu.CompilerParams(
            dimension_semantics=("arbitrary", "arbitrary"), vmem_limit_bytes=VMEM_LIMIT),
        name="ffn",
    )(x, mixed, mod, w_out, w1, w2, _layer_row(ln1_w), _layer_row(ln1_b), _layer_row(ln2_w),
      _layer_row(ln2_b))


def kernel(x, c, w_ada, b_ada, w_in, ret_norm_w, gla_gate_w, gla_gate_b, gla_norm_w,
           w_out, ln1_w, ln1_b, w_ff1, w_ff2, ln2_w, ln2_b):
    depth = w_in.shape[0]
    bsz, s, d = x.shape
    alpha = (2.0 * depth) ** 0.25
    tb = min(PROJMIX_ROWS, s)
    cos_t, sin_t = (jnp.asarray(t) for t in _rotary_tables(s))
    for l in range(depth):
        mod = _adaln(c, w_ada, b_ada, l)
        mixed, wo_bf, w1_bf, w2_bf, late_mod = _projmix(
            x, mod, cos_t, sin_t, w_in, gla_gate_w, gla_gate_b, ret_norm_w, gla_norm_w,
            w_out, w_ff1, w_ff2, c, w_ada, b_ada, l, tb)
        x = _ffn(x, mixed, late_mod, wo_bf, w1_bf, w2_bf, ln1_w, ln1_b, ln2_w, ln2_b, l, alpha,
                 min(FFN_ROWS, s))
    return x
```

```python
import functools
import math

import numpy as np
import jax
import jax.numpy as jnp
from jax import lax
from jax.experimental import pallas as pl
from jax.experimental.pallas import tpu as pltpu

CHUNK = 64
RET_HEADS = 4
RET_D = 128
RET_WIDTH = RET_HEADS * RET_D
GLA_HEADS = 4
GLA_DK = 64
GLA_DV = 128
GLA_KEY_WIDTH = GLA_HEADS * GLA_DK
GLA_WIDTH = GLA_HEADS * GLA_DV
D_MIX = RET_WIDTH + GLA_WIDTH
GATE_RANK = 16
GATE_TAU = 16.0
ROPE_BASE = 10000.0
LN_EPS = 1e-5
LANES = 128
MXU_WIDTH = 256
N_MOD = 6
N_EARLY_MOD = 2
PROJMIX_ROWS = 512
FFN_ROWS = 1024
SLAB = 4 * RET_WIDTH + 2 * GLA_KEY_WIDTH + 2 * GLA_WIDTH
OFF_RQ, OFF_RK, OFF_RV, OFF_RG = 0, 512, 1024, 1536
OFF_GQ, OFF_GK, OFF_GV, OFF_GG = 2048, 2304, 2560, 3072
MIX_ROWS = 256
BOUNDED_LOG2_DECAY = 96.0
BOUNDED_LOG2_QK = 24.0
BOUNDED_SPAN = 2 * CHUNK
VMEM_LIMIT = 56 * 1024 * 1024
LOG2E = math.log2(math.e)

BF16 = jnp.bfloat16
F32 = jnp.float32


def _dot(a, b):
    return jnp.dot(a, b, preferred_element_type=F32)


def _dot_nt(a, b):
    return lax.dot_general(a, b, (((1,), (1,)), ((), ())), preferred_element_type=F32)


def _dot_tn(a, b):
    return lax.dot_general(a, b, (((0,), (0,)), ((), ())), preferred_element_type=F32)


def _zero_of(v):
    bits = lax.shift_right_logical(pltpu.bitcast(v, jnp.uint32), jnp.uint32(32))
    return pltpu.bitcast(bits, F32)


def _ln(x, not_before=None):
    mu = jnp.mean(x, axis=-1, keepdims=True)
    if not_before is not None:
        mu = mu + _zero_of(not_before)[0:1, 0:1]
    xc = x - mu
    var = jnp.mean(xc * xc, axis=-1, keepdims=True)
    return xc * lax.rsqrt(var + LN_EPS)


def _silu(x):
    return x * (1.0 / (1.0 + jnp.exp(-x)))


def _adaln_kernel(c_ref, w_ref, b_ref, o_ref, *, per_step):
    c = c_ref[...]
    d = c.shape[1]
    val = _dot(_silu(c).astype(BF16), w_ref[...].astype(BF16)) + b_ref[...]
    for t in range(per_step):
        o_ref[:, pl.ds(pl.program_id(0) * per_step + t, 1), :] = val[:, None, t * d:(t + 1) * d]


def _adaln(c, w_ada, b_ada, layer):
    bsz, d = c.shape
    depth, _, n = w_ada.shape
    return pl.pallas_call(
        functools.partial(_adaln_kernel, per_step=N_EARLY_MOD),
        grid=(1,),
        in_specs=[pl.BlockSpec((bsz, d), lambda j: (0, 0)),
                  pl.BlockSpec((None, d, N_EARLY_MOD * d), lambda j: (layer, 0, 0)),
                  pl.BlockSpec((None, 1, N_EARLY_MOD * d), lambda j: (layer, 0, 0))],
        out_specs=pl.BlockSpec((bsz, N_EARLY_MOD, d), lambda j: (0, 0, 0)),
        out_shape=jax.ShapeDtypeStruct((bsz, N_EARLY_MOD, d), F32),
        compiler_params=pltpu.CompilerParams(vmem_limit_bytes=VMEM_LIMIT),
        name="adaln",
    )(c, w_ada, b_ada.reshape(depth, 1, n))


def _rotary_tables(s):
    half = RET_D // 2
    inv = 1.0 / (ROPE_BASE ** (np.arange(half, dtype=np.float64) / (half - 1)))
    ang = np.arange(s, dtype=np.float64)[:, None] * inv[None, :]
    cos = np.concatenate([np.cos(ang), np.cos(ang)], axis=1)
    sin = np.concatenate([-np.sin(ang), np.sin(ang)], axis=1)
    return cos.astype(np.float32), sin.astype(np.float32)


def _retention_tables(tb):
    log_gamma = np.log(1.0 - 2.0 ** (-5.0 - np.arange(RET_HEADS, dtype=np.float64)))
    idx = np.arange(tb)
    dist = np.abs(idx[:, None] - idx[None, :])
    visible = (idx[None, :] // CHUNK) <= (idx[:, None] // CHUNK)
    scale = RET_D ** -0.5
    dmask = np.exp(log_gamma[:, None, None] * dist[None]) * visible[None] * scale
    qdec = np.exp(log_gamma[:, None] * (idx[None, :] + 1.0)) * scale
    kdec = np.exp(log_gamma[:, None] * (tb - 1.0 - idx[None, :]))
    qdec = np.broadcast_to(qdec[:, :, None], (RET_HEADS, tb, RET_D))
    kdec = np.broadcast_to(kdec[:, :, None], (RET_HEADS, tb, RET_D))
    block_decay = [float(np.exp(lg * tb)) for lg in log_gamma]
    return (dmask.astype(np.float32), np.ascontiguousarray(qdec, np.float32),
            np.ascontiguousarray(kdec, np.float32), block_decay)


def _cumsum_matrix(tb):
    t = np.arange(tb)
    same_chunk = (t[:, None] // CHUNK) == (t[None, :] // CHUNK)
    return (same_chunk & (t[None, :] <= t[:, None])).astype(np.float32)


def _proj_stage(x_ref, mod_ref, cos_ref, sin_ref, w_ref, wtail_ref, gw_ref, gb_ref, slab, la_out,
                bounded_ref, slot):
    x = x_ref[0]
    shift1 = mod_ref[0, 0:1, :]
    scale1 = mod_ref[0, 1:2, :]
    ub = (_ln(x) * (1.0 + scale1) + shift1).astype(BF16)
    yield
    cos = cos_ref[...]
    sin = sin_ref[...]
    group = MXU_WIDTH
    qk_max = jnp.float32(0.0)
    for j in range(SLAB // group):
        p = _dot(ub, w_ref[:, j * group:(j + 1) * group])
        if OFF_GQ <= j * group < OFF_GV:
            qk_max = jnp.maximum(qk_max, jnp.max(jnp.abs(p)))
        if j * group < OFF_RV:
            for h in range(group // RET_D):
                xh = p[:, h * RET_D:(h + 1) * RET_D]
                rot = xh * cos + pltpu.roll(xh, RET_D // 2, 1) * sin
                slab[:, j * group + h * RET_D:j * group + (h + 1) * RET_D] = rot.astype(BF16)
        else:
            slab[:, j * group:(j + 1) * group] = p.astype(BF16)
        yield
    glr = _dot_nt(ub, wtail_ref[...])
    logit = _dot(glr.astype(BF16), gw_ref[...].astype(BF16)) + gb_ref[...]
    log_sig = jnp.minimum(logit, 0.0) - jnp.log(1.0 + jnp.exp(-jnp.abs(logit)))
    la = log_sig * (LOG2E / GATE_TAU)
    la_out[...] = la
    bounded = (jnp.min(la) * BOUNDED_SPAN >= -BOUNDED_LOG2_DECAY) & (qk_max <= 2.0 ** BOUNDED_LOG2_QK)
    bounded_ref[slot] = bounded.astype(jnp.int32)
    yield


def _row_of_block(b, block, row):
    n = b.shape[0] // block
    b3 = b.reshape(n, block, b.shape[1])
    return jnp.broadcast_to(b3[:, row:row + 1, :], b3.shape).reshape(b.shape)


def _mix_stage(slab, la_ref, r0, dmask_ref, qdec_ref, kdec_ref, csum_ref, rnw_ref, gnw_ref,
               out_ref, r_state, g_state, block_decay, bounded):
    tb = MIX_ROWS
    blk = slice(r0, r0 + tb)
    for h in range(RET_HEADS):
        q = slab[blk, OFF_RQ + h * RET_D:OFF_RQ + (h + 1) * RET_D]
        k = slab[blk, OFF_RK + h * RET_D:OFF_RK + (h + 1) * RET_D]
        v = slab[blk, OFF_RV + h * RET_D:OFF_RV + (h + 1) * RET_D]
        g = slab[blk, OFF_RG + h * RET_D:OFF_RG + (h + 1) * RET_D].astype(F32)
        p = (_dot_nt(q, k) * dmask_ref[h]).astype(BF16)
        r_prev = r_state[h]
        o = _dot(p, v) + _dot(q, r_prev.astype(BF16)) * qdec_ref[h]
        kd = (k.astype(F32) * kdec_ref[h]).astype(BF16)
        r_state[h] = block_decay[h] * r_prev + _dot_tn(kd, v)
        o = o - jnp.mean(o, axis=-1, keepdims=True)
        o = o * lax.rsqrt(jnp.mean(o * o, axis=-1, keepdims=True) + LN_EPS)
        o = o * _silu(g)
        out_ref[0, blk, h * RET_D:(h + 1) * RET_D] = o.astype(out_ref.dtype)
        yield

    la = la_ref[blk, :]
    la_hi = la.astype(BF16)
    la_lo = (la - la_hi.astype(F32)).astype(BF16)
    csum = csum_ref[...]
    b = _dot(csum, la_hi) + _dot(csum, la_lo)
    gq = slab[blk, OFF_GQ:OFF_GQ + GLA_KEY_WIDTH] * jnp.asarray(GLA_DK ** -0.5, BF16)
    gk = slab[blk, OFF_GK:OFF_GK + GLA_KEY_WIDTH]
    row = lax.broadcasted_iota(jnp.int32, (tb, GLA_KEY_WIDTH), 0)
    span = BOUNDED_SPAN if bounded else CHUNK
    if bounded:
        for c in range(1, span // CHUNK):
            carry = _row_of_block(b, span, c * CHUNK - 1)
            b = b + jnp.where((row & (span - 1)) >= c * CHUNK, carry, 0.0)
    q_state = gq * jnp.exp2(b).astype(BF16)
    k_state = gk * jnp.exp2(_row_of_block(b, span, span - 1) - b).astype(BF16)
    if bounded:
        half = 0.5 * BOUNDED_LOG2_DECAY
        shrink = jnp.exp2(b + half).astype(BF16)
        grow = jnp.exp2(-b - half).astype(BF16)
        q_lvl = [gq * shrink, gq * grow]
        phis = [grow, shrink]
    else:
        psis = [-jnp.abs(b - _row_of_block(b, 2 * c, c - 1)) for c in (32, 16, 8, 4)]
        la_next = pltpu.roll(la, tb - 1, 0)
        la_prev = pltpu.roll(la, 1, 0)
        r4 = row & 3
        psis.append(jnp.where(r4 == 0, la_next,
                              jnp.where(r4 == 1, 0.0, jnp.where(r4 == 2, la, la + la_prev))))
        psis.append(jnp.where((row & 1) == 1, la, 0.0))
        phis = [jnp.exp2(p).astype(BF16) for p in psis]
        q_lvl = [gq] + [gq * phi for phi in phis]
        phis = [None] + phis
    yield

    lane = lax.broadcasted_iota(jnp.int32, (span, LANES), 1)
    low_half = lane < GLA_DK
    zero_k = jnp.zeros((span, LANES), BF16)
    zero_s = jnp.zeros((GLA_DV, LANES), BF16)
    t_idx = lax.broadcasted_iota(jnp.int32, (span, GLA_HEADS * span), 0)
    s_idx = lax.broadcasted_iota(jnp.int32, (span, GLA_HEADS * span), 1) & (span - 1)
    t_xor_s = t_idx ^ s_idx
    lane_s = lax.broadcasted_iota(jnp.int32, (GLA_DV, LANES), 1)
    low_half_s = lane_s < GLA_DK

    def lane_tiles(piece, tile, zero, n_tiles):
        return jnp.concatenate([piece if j == tile else zero for j in range(n_tiles)], axis=1)

    for ch in range(tb // span):
        rows = slice(ch * span, (ch + 1) * span)
        orow = slice(r0 + ch * span, r0 + (ch + 1) * span)
        k_heads = [jnp.where(low_half if h % 2 == 0 else ~low_half,
                             gk[rows, (h // 2) * LANES:(h // 2 + 1) * LANES], zero_k)
                   for h in range(GLA_HEADS)]
        a_lvl = []
        for lvl in range(len(q_lvl)):
            pieces = []
            for h in range(GLA_HEADS):
                tile = h // 2
                kh = k_heads[h]
                if phis[lvl] is not None:
                    kh = kh * phis[lvl][rows, tile * LANES:(tile + 1) * LANES]
                pieces.append(lane_tiles(kh, tile, zero_k, 2))
            k_stack = jnp.concatenate(pieces, axis=0)
            a_lvl.append(_dot_nt(q_lvl[lvl][rows], k_stack))
        if bounded:
            same_chunk = (t_idx // CHUNK) == (s_idx // CHUNK)
            a = jnp.where(t_idx >= s_idx, a_lvl[0], jnp.where(same_chunk, a_lvl[1], 0.0))
        else:
            a = a_lvl[0]
            for i, c in enumerate((1, 2, 4, 8, 16, 32)):
                a = jnp.where(t_xor_s >= c, a_lvl[len(a_lvl) - 1 - i], a)
        a = a.astype(BF16)
        gv = slab[orow, OFF_GV:OFF_GV + GLA_WIDTH]
        v_bd = jnp.concatenate(
            [lane_tiles(gv[:, h * GLA_DV:(h + 1) * GLA_DV], h, jnp.zeros((span, GLA_DV), BF16), GLA_HEADS)
             for h in range(GLA_HEADS)], axis=0)
        s_bd = jnp.concatenate(
            [lane_tiles(g_state[h].astype(BF16), h // 2, zero_s, 2) for h in range(GLA_HEADS)],
            axis=0)
        o_all = _dot(a, v_bd) + _dot_nt(q_state[rows], s_bd)
        decay = jnp.exp2(b[(ch + 1) * span - 1:(ch + 1) * span, :])
        for h in range(GLA_HEADS):
            tile = h // 2
            upd = _dot_tn(gv[:, h * GLA_DV:(h + 1) * GLA_DV], k_state[rows, tile * LANES:(tile + 1) * LANES])
            upd = jnp.where(low_half_s if h % 2 == 0 else ~low_half_s, upd, 0.0)
            g_state[h] = g_state[h] * decay[:, tile * LANES:(tile + 1) * LANES] + upd
        for h in range(GLA_HEADS):
            o = o_all[:, h * GLA_DV:(h + 1) * GLA_DV]
            o = o * lax.rsqrt(jnp.mean(o * o, axis=-1, keepdims=True) + LN_EPS)
            gg = slab[orow, OFF_GG + h * GLA_DV:OFF_GG + (h + 1) * GLA_DV].astype(F32)
            o = o * _silu(gg)
            out_ref[0, orow, RET_WIDTH + h * GLA_DV:RET_WIDTH + (h + 1) * GLA_DV] = o.astype(out_ref.dtype)
        yield


def _interleave(gen_a, n_a, gen_b, n_b):
    done_a = done_b = 0
    while done_a < n_a or done_b < n_b:
        if done_b >= n_b or (done_a < n_a and (done_a + 0.5) * n_b <= (done_b + 0.5) * n_a):
            next(gen_a)
            done_a += 1
        else:
            next(gen_b)
            done_b += 1
    for g in (gen_a, gen_b):
        assert next(g, "done") == "done", "piece count mismatch"


def _projmix_kernel(x_ref, mod_ref, cos_ref, sin_ref, w_ref, gw_ref, gb_ref,
                    dmask_ref, qdec_ref, kdec_ref, csum_ref, rnw_ref, gnw_ref,
                    wo_ref, w1_ref, w2_ref, c_ref, wa_ref, ba_ref,
                    out_ref, wo_bf_ref, w1_bf_ref, w2_bf_ref, late_mod_ref,
                    w_bf, w_tail, slab2, la2, bounded_ref, r_state, g_state,
                    *, tb, blocks_per_seq, n_blocks, block_decay):
    j = pl.program_id(0)
    n_proj = 2 + SLAB // MXU_WIDTH
    n_mix = {bounded: (tb // MIX_ROWS) * (RET_HEADS + 1 + MIX_ROWS // (BOUNDED_SPAN if bounded else CHUNK))
             for bounded in (True, False)}

    @pl.when(j == 0)
    def _():
        for g in range(0, SLAB, MXU_WIDTH):
            w_bf[:, g:g + MXU_WIDTH] = w_ref[g:g + MXU_WIDTH, :].T.astype(BF16)
        w_tail[...] = w_ref[SLAB:SLAB + GATE_RANK, :].astype(BF16)
        bounded_ref[1] = 0

    def side_jobs():
        in_ret = jnp.minimum(j, n_blocks - 1) < RET_WIDTH // wo_ref.shape[0]
        norm_w = jnp.where(in_ret, rnw_ref[...], gnw_ref[...])
        wo_bf_ref[...] = (wo_ref[...] * norm_w).astype(BF16)
        w1_bf_ref[...] = w1_ref[...].astype(BF16)
        w2_bf_ref[...] = w2_ref[...].astype(BF16)
        late_mod_ref[...] = (_dot(_silu(c_ref[...]).astype(BF16), wa_ref[...].astype(BF16))
                             + ba_ref[...])

    @pl.when((j - 1) % blocks_per_seq == 0)
    def _():
        r_state[...] = jnp.zeros_like(r_state)
        g_state[...] = jnp.zeros_like(g_state)

    def project(buf):
        return _proj_stage(x_ref, mod_ref, cos_ref, sin_ref, w_bf, w_tail, gw_ref, gb_ref,
                           slab2.at[buf], la2.at[buf], bounded_ref, buf)

    def mix(buf, bounded):
        for r0 in range(0, tb, MIX_ROWS):
            yield from _mix_stage(slab2.at[buf], la2.at[buf], r0, dmask_ref, qdec_ref, kdec_ref,
                                  csum_ref, rnw_ref, gnw_ref, out_ref, r_state, g_state, block_decay,
                                  bounded)

    @pl.when(j == 0)
    def _():
        side_jobs()
        for _ in project(0):
            pass

    steady = (j > 0) & (j < n_blocks)
    bounded = bounded_ref[1 - j % 2] == 1
    for parity in (0, 1):
        @pl.when(steady & bounded & (j % 2 == parity))
        def _():
            side_jobs()
            _interleave(project(parity), n_proj, mix(1 - parity, True), n_mix[True])

    @pl.when(steady & jnp.logical_not(bounded))
    def _():
        side_jobs()
        _interleave(project(j % 2), n_proj, mix(1 - j % 2, False), n_mix[False])

    @pl.when(j == n_blocks)
    def _():
        side_jobs()
        for _ in mix((n_blocks - 1) % 2, False):
            pass


def _layer_row(v):
    return v.reshape(v.shape[0], 1, v.shape[1])


def _projmix(x, mod, cos_t, sin_t, w_in, gate_w, gate_b, ret_norm_w, gla_norm_w,
             w_out, w_ff1, w_ff2, c, w_ada, b_ada, layer, tb):
    bsz, s, d = x.shape
    d_ff = w_ff1.shape[2]
    bps = s // tb
    nblk = bsz * bps
    late_cols = (N_MOD - N_EARLY_MOD) * d // nblk
    late_first = N_EARLY_MOD * d // late_cols
    dmask, qdec, kdec, block_decay = _retention_tables(MIX_ROWS)
    csum = _cumsum_matrix(MIX_ROWS)
    const2 = lambda j: (0, 0)
    const3 = lambda j: (0, 0, 0)
    of_layer = lambda j: (layer, 0, 0)
    proj_blk = lambda j: jnp.minimum(j, nblk - 1)
    mix_blk = lambda j: jnp.maximum(j - 1, 0)
    single = dict(pipeline_mode=pl.Buffered(1))
    wo_rows, w1_rows, w2_rows = d // nblk, d // nblk, d_ff // nblk
    return pl.pallas_call(
        functools.partial(_projmix_kernel, tb=tb, blocks_per_seq=bps, n_blocks=nblk,
                          block_decay=block_decay),
        grid=(nblk + 1,),
        in_specs=[pl.BlockSpec((1, tb, d), lambda j: (proj_blk(j) // bps, proj_blk(j) % bps, 0)),
                  pl.BlockSpec((1, N_EARLY_MOD, d), lambda j: (proj_blk(j) // bps, 0, 0)),
                  pl.BlockSpec((tb, RET_D), lambda j: (proj_blk(j) % bps, 0)),
                  pl.BlockSpec((tb, RET_D), lambda j: (proj_blk(j) % bps, 0)),
                  pl.BlockSpec((None, SLAB + GATE_RANK, d), of_layer, **single),
                  pl.BlockSpec((None, GATE_RANK, GLA_KEY_WIDTH), of_layer, **single),
                  pl.BlockSpec((None, 1, GLA_KEY_WIDTH), of_layer, **single),
                  pl.BlockSpec(dmask.shape, const3, **single),
                  pl.BlockSpec(qdec.shape, const3, **single),
                  pl.BlockSpec(kdec.shape, const3, **single),
                  pl.BlockSpec(csum.shape, const2, **single),
                  pl.BlockSpec((None, wo_rows, 1),
                               lambda j: (layer, jnp.minimum(proj_blk(j), RET_WIDTH // wo_rows - 1), 0)),
                  pl.BlockSpec((None, wo_rows, 1),
                               lambda j: (layer, jnp.maximum(proj_blk(j) - RET_WIDTH // wo_rows, 0), 0)),
                  pl.BlockSpec((None, wo_rows, d), lambda j: (layer, proj_blk(j), 0)),
                  pl.BlockSpec((None, w1_rows, d_ff), lambda j: (layer, proj_blk(j), 0)),
                  pl.BlockSpec((None, w2_rows, d), lambda j: (layer, proj_blk(j), 0)),
                  pl.BlockSpec((bsz, d), const2, **single),
                  pl.BlockSpec((None, d, late_cols), lambda j: (layer, 0, late_first + proj_blk(j))),
                  pl.BlockSpec((None, 1, late_cols), lambda j: (layer, 0, late_first + proj_blk(j)))],
        out_specs=[pl.BlockSpec((1, tb, D_MIX), lambda j: (mix_blk(j) // bps, mix_blk(j) % bps, 0)),
                   pl.BlockSpec((wo_rows, d), lambda j: (proj_blk(j), 0)),
                   pl.BlockSpec((w1_rows, d_ff), lambda j: (proj_blk(j), 0)),
                   pl.BlockSpec((w2_rows, d), lambda j: (proj_blk(j), 0)),
                   pl.BlockSpec((bsz, late_cols), lambda j: (0, proj_blk(j)))],
        out_shape=[jax.ShapeDtypeStruct((bsz, s, D_MIX), BF16),
                   jax.ShapeDtypeStruct((d, d), BF16),
                   jax.ShapeDtypeStruct((d, d_ff), BF16),
                   jax.ShapeDtypeStruct((d_ff, d), BF16),
                   jax.ShapeDtypeStruct((bsz, (N_MOD - N_EARLY_MOD) * d), F32)],
        scratch_shapes=[pltpu.VMEM((d, SLAB), BF16), pltpu.VMEM((GATE_RANK, d), BF16),
                        pltpu.VMEM((2, tb, SLAB), BF16), pltpu.VMEM((2, tb, GLA_KEY_WIDTH), F32),
                        pltpu.SMEM((2,), jnp.int32),
                        pltpu.VMEM((RET_HEADS, RET_D, RET_D), F32),
                        pltpu.VMEM((GLA_HEADS, GLA_DV, LANES), F32)],
        compiler_params=pltpu.CompilerParams(
            dimension_semantics=("arbitrary",), vmem_limit_bytes=VMEM_LIMIT),
        name="projmix",
    )(x, mod, cos_t, sin_t, jnp.swapaxes(w_in, 1, 2), gate_w, _layer_row(gate_b),
      jnp.asarray(dmask), jnp.asarray(qdec), jnp.asarray(kdec), jnp.asarray(csum, BF16),
      ret_norm_w[:, :, None], gla_norm_w[:, :, None], w_out, w_ff1, w_ff2,
      c, w_ada, _layer_row(b_ada))


def _ffn_kernel(x_ref, mix_ref, mod_ref, wo_ref, w1_ref, w2_ref, ln1w_ref, ln1b_ref,
                ln2w_ref, ln2b_ref, out_ref, h_ref, *, alpha):
    seq = pl.ds(pl.program_id(0), 1)
    d = x_ref.shape[2]
    gate1 = mod_ref[seq, 0:d]
    shift2 = mod_ref[seq, d:2 * d]
    scale2 = mod_ref[seq, 2 * d:3 * d]
    gate2 = mod_ref[seq, 3 * d:4 * d]
    d_ff = w1_ref.shape[1]
    n_groups = 4
    group = d_ff // n_groups
    tb = x_ref.shape[1]
    halves = (slice(0, tb // 2), slice(tb // 2, tb))

    def chain(rows, wait):
        m = _dot(mix_ref[0, rows, :], wo_ref[...])
        yield
        x1 = (_ln(alpha * x_ref[0, rows, :] + gate1 * m, wait.get("ln1")) * ln1w_ref[...]
              + ln1b_ref[...])
        u2f = _ln(x1) * (1.0 + scale2) + shift2
        wait["ln1_done"] = u2f[u2f.shape[0] - 8:, u2f.shape[1] - LANES:]
        u2 = u2f.astype(BF16)
        yield
        for j in range(n_groups):
            hj = jnp.maximum(_dot(u2, w1_ref[:, j * group:(j + 1) * group]), 0.0)
            h_ref[rows, j * group:(j + 1) * group] = (hj * hj).astype(BF16)
            yield
        f = _dot(h_ref[rows, :], w2_ref[...])
        yield
        out_ref[0, rows, :] = _ln(alpha * x1 + gate2 * f) * ln2w_ref[...] + ln2b_ref[...]
        yield

    wait_a, wait_b = {}, {}
    half_a, half_b = chain(halves[0], wait_a), chain(halves[1], wait_b)
    next(half_a)
    next(half_b)
    next(half_a)
    wait_b["ln1"] = wait_a["ln1_done"]
    n_rest = n_groups + 2
    _interleave(half_a, n_rest, half_b, n_rest + 1)


def _ffn(x, mixed, mod, w_out, w1, w2, ln1_w, ln1_b, ln2_w, ln2_b, layer, alpha, tb):
    bsz, s, d = x.shape
    nb = s // tb
    d_ff = w1.shape[1]
    const2 = lambda b, i: (0, 0)
    of_layer = lambda b, i: (layer, 0, 0)
    single = dict(pipeline_mode=pl.Buffered(1))
    return pl.pallas_call(
        functools.partial(_ffn_kernel, alpha=alpha),
        grid=(bsz, nb),
        in_specs=[pl.BlockSpec((1, tb, d), lambda b, i: (b, i, 0)),
                  pl.BlockSpec((1, tb, d), lambda b, i: (b, i, 0)),
                  pl.BlockSpec(mod.shape, const2, **single),
                  pl.BlockSpec((d, d), const2, **single),
                  pl.BlockSpec((d, d_ff), const2, **single),
                  pl.BlockSpec((d_ff, d), const2, **single),
                  pl.BlockSpec((None, 1, d), of_layer, **single),
                  pl.BlockSpec((None, 1, d), of_layer, **single),
                  pl.BlockSpec((None, 1, d), of_layer, **single),
                  pl.BlockSpec((None, 1, d), of_layer, **single)],
        out_specs=pl.BlockSpec((1, tb, d), lambda b, i: (b, i, 0)),
        out_shape=jax.ShapeDtypeStruct((bsz, s, d), x.dtype),
        scratch_shapes=[pltpu.VMEM((tb, d_ff), BF16)],
        compiler_params=pltpu.CompilerParams(
            dimension_semantics=("parallel", "parallel"), vmem_limit_bytes=VMEM_LIMIT),
        name="ffn",
    )(x, mixed, mod, w_out, w1, w2, _layer_row(ln1_w), _layer_row(ln1_b), _layer_row(ln2_w),
      _layer_row(ln2_b))


def kernel(x, c, w_ada, b_ada, w_in, ret_norm_w, gla_gate_w, gla_gate_b, gla_norm_w,
           w_out, ln1_w, ln1_b, w_ff1, w_ff2, ln2_w, ln2_b):
    depth = w_in.shape[0]
    bsz, s, d = x.shape
    alpha = (2.0 * depth) ** 0.25
    tb = min(PROJMIX_ROWS, s)
    cos_t, sin_t = (jnp.asarray(t) for t in _rotary_tables(s))
    for l in range(depth):
        mod = _adaln(c, w_ada, b_ada, l)
        mixed, wo_bf, w1_bf, w2_bf, late_mod = _projmix(
            x, mod, cos_t, sin_t, w_in, gla_gate_w, gla_gate_b, ret_norm_w, gla_norm_w,
            w_out, w_ff1, w_ff2, c, w_ada, b_ada, l, tb)
        x = _ffn(x, mixed, late_mod, wo_bf, w1_bf, w2_bf, ln1_w, ln1_b, ln2_w, ln2_b, l, alpha,
                 min(FFN_ROWS, s))
    return x
```

```python
import functools
import math

import numpy as np
import jax
import jax.numpy as jnp
from jax import lax
from jax.experimental import pallas as pl
from jax.experimental.pallas import tpu as pltpu

CHUNK = 64
RET_HEADS = 4
RET_D = 128
RET_WIDTH = RET_HEADS * RET_D
GLA_HEADS = 4
GLA_DK = 64
GLA_DV = 128
GLA_KEY_WIDTH = GLA_HEADS * GLA_DK
GLA_WIDTH = GLA_HEADS * GLA_DV
D_MIX = RET_WIDTH + GLA_WIDTH
GATE_RANK = 16
GATE_TAU = 16.0
ROPE_BASE = 10000.0
LN_EPS = 1e-5
LANES = 128
MXU_WIDTH = 256
N_MOD = 6
N_EARLY_MOD = 2
PROJMIX_ROWS = 512
FFN_ROWS = 1024
SLAB = 4 * RET_WIDTH + 2 * GLA_KEY_WIDTH + 2 * GLA_WIDTH
OFF_RQ, OFF_RK, OFF_RV, OFF_RG = 0, 512, 1024, 1536
OFF_GQ, OFF_GK, OFF_GV, OFF_GG = 2048, 2304, 2560, 3072
MIX_ROWS = 256
BOUNDED_LOG2_DECAY = 96.0
BOUNDED_LOG2_QK = 24.0
BOUNDED_SPAN = 2 * CHUNK
VMEM_LIMIT = 56 * 1024 * 1024
LOG2E = math.log2(math.e)

BF16 = jnp.bfloat16
F32 = jnp.float32


def _dot(a, b):
    return jnp.dot(a, b, preferred_element_type=F32)


def _dot_nt(a, b):
    return lax.dot_general(a, b, (((1,), (1,)), ((), ())), preferred_element_type=F32)


def _dot_tn(a, b):
    return lax.dot_general(a, b, (((0,), (0,)), ((), ())), preferred_element_type=F32)


def _zero_of(v):
    bits = lax.shift_right_logical(pltpu.bitcast(v, jnp.uint32), jnp.uint32(32))
    return pltpu.bitcast(bits, F32)


def _ln(x, not_before=None):
    mu = jnp.mean(x, axis=-1, keepdims=True)
    if not_before is not None:
        mu = mu + _zero_of(not_before)[0:1, 0:1]
    xc = x - mu
    var = jnp.mean(xc * xc, axis=-1, keepdims=True)
    return xc * lax.rsqrt(var + LN_EPS)


def _silu(x):
    return x * (1.0 / (1.0 + jnp.exp(-x)))


def _adaln_kernel(c_ref, w_ref, b_ref, o_ref, *, per_step):
    c = c_ref[...]
    d = c.shape[1]
    val = _dot(_silu(c).astype(BF16), w_ref[...].astype(BF16)) + b_ref[...]
    for t in range(per_step):
        o_ref[:, pl.ds(pl.program_id(0) * per_step + t, 1), :] = val[:, None, t * d:(t + 1) * d]


def _adaln(c, w_ada, b_ada, layer):
    bsz, d = c.shape
    depth, _, n = w_ada.shape
    return pl.pallas_call(
        functools.partial(_adaln_kernel, per_step=N_EARLY_MOD),
        grid=(1,),
        in_specs=[pl.BlockSpec((bsz, d), lambda j: (0, 0)),
                  pl.BlockSpec((None, d, N_EARLY_MOD * d), lambda j: (layer, 0, 0)),
                  pl.BlockSpec((None, 1, N_EARLY_MOD * d), lambda j: (layer, 0, 0))],
        out_specs=pl.BlockSpec((bsz, N_EARLY_MOD, d), lambda j: (0, 0, 0)),
        out_shape=jax.ShapeDtypeStruct((bsz, N_EARLY_MOD, d), F32),
        compiler_params=pltpu.CompilerParams(vmem_limit_bytes=VMEM_LIMIT),
        name="adaln",
    )(c, w_ada, b_ada.reshape(depth, 1, n))


def _rotary_tables(s):
    half = RET_D // 2
    inv = 1.0 / (ROPE_BASE ** (np.arange(half, dtype=np.float64) / (half - 1)))
    ang = np.arange(s, dtype=np.float64)[:, None] * inv[None, :]
    cos = np.concatenate([np.cos(ang), np.cos(ang)], axis=1)
    sin = np.concatenate([-np.sin(ang), np.sin(ang)], axis=1)
    return cos.astype(np.float32), sin.astype(np.float32)


def _retention_tables(tb):
    log_gamma = np.log(1.0 - 2.0 ** (-5.0 - np.arange(RET_HEADS, dtype=np.float64)))
    idx = np.arange(tb)
    dist = np.abs(idx[:, None] - idx[None, :])
    visible = (idx[None, :] // CHUNK) <= (idx[:, None] // CHUNK)
    scale = RET_D ** -0.5
    dmask = np.exp(log_gamma[:, None, None] * dist[None]) * visible[None] * scale
    qdec = np.exp(log_gamma[:, None] * (idx[None, :] + 1.0)) * scale
    kdec = np.exp(log_gamma[:, None] * (tb - 1.0 - idx[None, :]))
    qdec = np.broadcast_to(qdec[:, :, None], (RET_HEADS, tb, RET_D))
    kdec = np.broadcast_to(kdec[:, :, None], (RET_HEADS, tb, RET_D))
    block_decay = [float(np.exp(lg * tb)) for lg in log_gamma]
    return (dmask.astype(np.float32), np.ascontiguousarray(qdec, np.float32),
            np.ascontiguousarray(kdec, np.float32), block_decay)


def _cumsum_matrix(tb):
    t = np.arange(tb)
    same_chunk = (t[:, None] // CHUNK) == (t[None, :] // CHUNK)
    return (same_chunk & (t[None, :] <= t[:, None])).astype(np.float32)


def _proj_stage(x_ref, mod_ref, cos_ref, sin_ref, w_ref, wtail_ref, gw_ref, gb_ref, slab, la_out,
                bounded_ref, slot):
    x = x_ref[0]
    shift1 = mod_ref[0, 0:1, :]
    scale1 = mod_ref[0, 1:2, :]
    ub = (_ln(x) * (1.0 + scale1) + shift1).astype(BF16)
    yield
    cos = cos_ref[...]
    sin = sin_ref[...]
    group = MXU_WIDTH
    qk_max = jnp.float32(0.0)
    for j in range(SLAB // group):
        p = _dot(ub, w_ref[:, j * group:(j + 1) * group])
        if OFF_GQ <= j * group < OFF_GV:
            qk_max = jnp.maximum(qk_max, jnp.max(jnp.abs(p)))
        if j * group < OFF_RV:
            for h in range(group // RET_D):
                xh = p[:, h * RET_D:(h + 1) * RET_D]
                rot = xh * cos + pltpu.roll(xh, RET_D // 2, 1) * sin
                slab[:, j * group + h * RET_D:j * group + (h + 1) * RET_D] = rot.astype(BF16)
        else:
            slab[:, j * group:(j + 1) * group] = p.astype(BF16)
        yield
    glr = _dot_nt(ub, wtail_ref[...])
    logit = _dot(glr.astype(BF16), gw_ref[...].astype(BF16)) + gb_ref[...]
    log_sig = jnp.minimum(logit, 0.0) - jnp.log(1.0 + jnp.exp(-jnp.abs(logit)))
    la = log_sig * (LOG2E / GATE_TAU)
    la_out[...] = la
    bounded = (jnp.min(la) * BOUNDED_SPAN >= -BOUNDED_LOG2_DECAY) & (qk_max <= 2.0 ** BOUNDED_LOG2_QK)
    bounded_ref[slot] = bounded.astype(jnp.int32)
    yield


def _row_of_block(b, block, row):
    n = b.shape[0] // block
    b3 = b.reshape(n, block, b.shape[1])
    return jnp.broadcast_to(b3[:, row:row + 1, :], b3.shape).reshape(b.shape)


def _mix_stage(slab, la_ref, r0, dmask_ref, qdec_ref, kdec_ref, csum_ref, rnw_ref, gnw_ref,
               out_ref, r_state, g_state, block_decay, bounded):
    tb = MIX_ROWS
    blk = slice(r0, r0 + tb)
    for h in range(RET_HEADS):
        q = slab[blk, OFF_RQ + h * RET_D:OFF_RQ + (h + 1) * RET_D]
        k = slab[blk, OFF_RK + h * RET_D:OFF_RK + (h + 1) * RET_D]
        v = slab[blk, OFF_RV + h * RET_D:OFF_RV + (h + 1) * RET_D]
        g = slab[blk, OFF_RG + h * RET_D:OFF_RG + (h + 1) * RET_D].astype(F32)
        p = (_dot_nt(q, k) * dmask_ref[h]).astype(BF16)
        r_prev = r_state[h]
        o = _dot(p, v) + _dot(q, r_prev.astype(BF16)) * qdec_ref[h]
        kd = (k.astype(F32) * kdec_ref[h]).astype(BF16)
        r_state[h] = block_decay[h] * r_prev + _dot_tn(kd, v)
        o = o - jnp.mean(o, axis=-1, keepdims=True)
        o = o * lax.rsqrt(jnp.mean(o * o, axis=-1, keepdims=True) + LN_EPS)
        o = o * rnw_ref[:, h * RET_D:(h + 1) * RET_D] * _silu(g)
        out_ref[0, blk, h * RET_D:(h + 1) * RET_D] = o.astype(out_ref.dtype)
        yield

    la = la_ref[blk, :]
    la_hi = la.astype(BF16)
    la_lo = (la - la_hi.astype(F32)).astype(BF16)
    csum = csum_ref[...]
    b = _dot(csum, la_hi) + _dot(csum, la_lo)
    gq = slab[blk, OFF_GQ:OFF_GQ + GLA_KEY_WIDTH] * jnp.asarray(GLA_DK ** -0.5, BF16)
    gk = slab[blk, OFF_GK:OFF_GK + GLA_KEY_WIDTH]
    row = lax.broadcasted_iota(jnp.int32, (tb, GLA_KEY_WIDTH), 0)
    span = BOUNDED_SPAN if bounded else CHUNK
    if bounded:
        for c in range(1, span // CHUNK):
            carry = _row_of_block(b, span, c * CHUNK - 1)
            b = b + jnp.where((row & (span - 1)) >= c * CHUNK, carry, 0.0)
    q_state = gq * jnp.exp2(b).astype(BF16)
    k_state = gk * jnp.exp2(_row_of_block(b, span, span - 1) - b).astype(BF16)
    if bounded:
        half = 0.5 * BOUNDED_LOG2_DECAY
        shrink = jnp.exp2(b + half).astype(BF16)
        grow = jnp.exp2(-b - half).astype(BF16)
        q_lvl = [gq * shrink, gq * grow]
        phis = [grow, shrink]
    else:
        psis = [-jnp.abs(b - _row_of_block(b, 2 * c, c - 1)) for c in (32, 16, 8, 4)]
        la_next = pltpu.roll(la, tb - 1, 0)
        la_prev = pltpu.roll(la, 1, 0)
        r4 = row & 3
        psis.append(jnp.where(r4 == 0, la_next,
                              jnp.where(r4 == 1, 0.0, jnp.where(r4 == 2, la, la + la_prev))))
        psis.append(jnp.where((row & 1) == 1, la, 0.0))
        phis = [jnp.exp2(p).astype(BF16) for p in psis]
        q_lvl = [gq] + [gq * phi for phi in phis]
        phis = [None] + phis
    yield

    lane = lax.broadcasted_iota(jnp.int32, (span, LANES), 1)
    low_half = lane < GLA_DK
    zero_k = jnp.zeros((span, LANES), BF16)
    zero_s = jnp.zeros((GLA_DV, LANES), BF16)
    t_idx = lax.broadcasted_iota(jnp.int32, (span, GLA_HEADS * span), 0)
    s_idx = lax.broadcasted_iota(jnp.int32, (span, GLA_HEADS * span), 1) & (span - 1)
    t_xor_s = t_idx ^ s_idx
    lane_s = lax.broadcasted_iota(jnp.int32, (GLA_DV, LANES), 1)
    low_half_s = lane_s < GLA_DK

    def lane_tiles(piece, tile, zero, n_tiles):
        return jnp.concatenate([piece if j == tile else zero for j in range(n_tiles)], axis=1)

    for ch in range(tb // span):
        rows = slice(ch * span, (ch + 1) * span)
        orow = slice(r0 + ch * span, r0 + (ch + 1) * span)
        k_heads = [jnp.where(low_half if h % 2 == 0 else ~low_half,
                             gk[rows, (h // 2) * LANES:(h // 2 + 1) * LANES], zero_k)
                   for h in range(GLA_HEADS)]
        a_lvl = []
        for lvl in range(len(q_lvl)):
            pieces = []
            for h in range(GLA_HEADS):
                tile = h // 2
                kh = k_heads[h]
                if phis[lvl] is not None:
                    kh = kh * phis[lvl][rows, tile * LANES:(tile + 1) * LANES]
                pieces.append(lane_tiles(kh, tile, zero_k, 2))
            k_stack = jnp.concatenate(pieces, axis=0)
            a_lvl.append(_dot_nt(q_lvl[lvl][rows], k_stack))
        if bounded:
            same_chunk = (t_idx // CHUNK) == (s_idx // CHUNK)
            a = jnp.where(t_idx >= s_idx, a_lvl[0], jnp.where(same_chunk, a_lvl[1], 0.0))
        else:
            a = a_lvl[0]
            for i, c in enumerate((1, 2, 4, 8, 16, 32)):
                a = jnp.where(t_xor_s >= c, a_lvl[len(a_lvl) - 1 - i], a)
        a = a.astype(BF16)
        gv = slab[orow, OFF_GV:OFF_GV + GLA_WIDTH]
        v_bd = jnp.concatenate(
            [lane_tiles(gv[:, h * GLA_DV:(h + 1) * GLA_DV], h, jnp.zeros((span, GLA_DV), BF16), GLA_HEADS)
             for h in range(GLA_HEADS)], axis=0)
        s_bd = jnp.concatenate(
            [lane_tiles(g_state[h].astype(BF16), h // 2, zero_s, 2) for h in range(GLA_HEADS)],
            axis=0)
        o_all = _dot(a, v_bd) + _dot_nt(q_state[rows], s_bd)
        decay = jnp.exp2(b[(ch + 1) * span - 1:(ch + 1) * span, :])
        for h in range(GLA_HEADS):
            tile = h // 2
            upd = _dot_tn(gv[:, h * GLA_DV:(h + 1) * GLA_DV], k_state[rows, tile * LANES:(tile + 1) * LANES])
            upd = jnp.where(low_half_s if h % 2 == 0 else ~low_half_s, upd, 0.0)
            g_state[h] = g_state[h] * decay[:, tile * LANES:(tile + 1) * LANES] + upd
        for h in range(GLA_HEADS):
            o = o_all[:, h * GLA_DV:(h + 1) * GLA_DV]
            o = o * lax.rsqrt(jnp.mean(o * o, axis=-1, keepdims=True) + LN_EPS)
            gg = slab[orow, OFF_GG + h * GLA_DV:OFF_GG + (h + 1) * GLA_DV].astype(F32)
            o = o * gnw_ref[:, h * GLA_DV:(h + 1) * GLA_DV] * _silu(gg)
            out_ref[0, orow, RET_WIDTH + h * GLA_DV:RET_WIDTH + (h + 1) * GLA_DV] = o.astype(out_ref.dtype)
        yield


def _interleave(gen_a, n_a, gen_b, n_b):
    done_a = done_b = 0
    while done_a < n_a or done_b < n_b:
        if done_b >= n_b or (done_a < n_a and (done_a + 0.5) * n_b <= (done_b + 0.5) * n_a):
            next(gen_a)
            done_a += 1
        else:
            next(gen_b)
            done_b += 1
    for g in (gen_a, gen_b):
        assert next(g, "done") == "done", "piece count mismatch"


def _projmix_kernel(x_ref, mod_ref, cos_ref, sin_ref, w_ref, gw_ref, gb_ref,
                    dmask_ref, qdec_ref, kdec_ref, csum_ref, rnw_ref, gnw_ref,
                    wo_ref, w1_ref, w2_ref, c_ref, wa_ref, ba_ref,
                    out_ref, wo_bf_ref, w1_bf_ref, w2_bf_ref, late_mod_ref,
                    w_bf, w_tail, slab2, la2, bounded_ref, r_state, g_state,
                    *, tb, blocks_per_seq, n_blocks, block_decay):
    j = pl.program_id(0)
    n_proj = 2 + SLAB // MXU_WIDTH
    n_mix = {bounded: (tb // MIX_ROWS) * (RET_HEADS + 1 + MIX_ROWS // (BOUNDED_SPAN if bounded else CHUNK))
             for bounded in (True, False)}

    @pl.when(j == 0)
    def _():
        for g in range(0, SLAB, MXU_WIDTH):
            w_bf[:, g:g + MXU_WIDTH] = w_ref[g:g + MXU_WIDTH, :].T.astype(BF16)
        w_tail[...] = w_ref[SLAB:SLAB + GATE_RANK, :].astype(BF16)
        bounded_ref[1] = 0

    def side_jobs():
        wo_bf_ref[...] = wo_ref[...].astype(BF16)
        w1_bf_ref[...] = w1_ref[...].astype(BF16)
        w2_bf_ref[...] = w2_ref[...].astype(BF16)
        late_mod_ref[...] = (_dot(_silu(c_ref[...]).astype(BF16), wa_ref[...].astype(BF16))
                             + ba_ref[...])

    @pl.when((j - 1) % blocks_per_seq == 0)
    def _():
        r_state[...] = jnp.zeros_like(r_state)
        g_state[...] = jnp.zeros_like(g_state)

    def project(buf):
        return _proj_stage(x_ref, mod_ref, cos_ref, sin_ref, w_bf, w_tail, gw_ref, gb_ref,
                           slab2.at[buf], la2.at[buf], bounded_ref, buf)

    def mix(buf, bounded):
        for r0 in range(0, tb, MIX_ROWS):
            yield from _mix_stage(slab2.at[buf], la2.at[buf], r0, dmask_ref, qdec_ref, kdec_ref,
                                  csum_ref, rnw_ref, gnw_ref, out_ref, r_state, g_state, block_decay,
                                  bounded)

    @pl.when(j == 0)
    def _():
        side_jobs()
        for _ in project(0):
            pass

    steady = (j > 0) & (j < n_blocks)
    bounded = bounded_ref[1 - j % 2] == 1
    for parity in (0, 1):
        @pl.when(steady & bounded & (j % 2 == parity))
        def _():
            side_jobs()
            _interleave(project(parity), n_proj, mix(1 - parity, True), n_mix[True])

    @pl.when(steady & jnp.logical_not(bounded))
    def _():
        side_jobs()
        _interleave(project(j % 2), n_proj, mix(1 - j % 2, False), n_mix[False])

    @pl.when(j == n_blocks)
    def _():
        side_jobs()
        for _ in mix((n_blocks - 1) % 2, False):
            pass


def _layer_row(v):
    return v.reshape(v.shape[0], 1, v.shape[1])


def _projmix(x, mod, cos_t, sin_t, w_in, gate_w, gate_b, ret_norm_w, gla_norm_w,
             w_out, w_ff1, w_ff2, c, w_ada, b_ada, layer, tb):
    bsz, s, d = x.shape
    d_ff = w_ff1.shape[2]
    bps = s // tb
    nblk = bsz * bps
    late_cols = (N_MOD - N_EARLY_MOD) * d // nblk
    late_first = N_EARLY_MOD * d // late_cols
    dmask, qdec, kdec, block_decay = _retention_tables(MIX_ROWS)
    csum = _cumsum_matrix(MIX_ROWS)
    const2 = lambda j: (0, 0)
    const3 = lambda j: (0, 0, 0)
    of_layer = lambda j: (layer, 0, 0)
    proj_blk = lambda j: jnp.minimum(j, nblk - 1)
    mix_blk = lambda j: jnp.maximum(j - 1, 0)
    single = dict(pipeline_mode=pl.Buffered(1))
    wo_rows, w1_rows, w2_rows = d // nblk, d // nblk, d_ff // nblk
    return pl.pallas_call(
        functools.partial(_projmix_kernel, tb=tb, blocks_per_seq=bps, n_blocks=nblk,
                          block_decay=block_decay),
        grid=(nblk + 1,),
        in_specs=[pl.BlockSpec((1, tb, d), lambda j: (proj_blk(j) // bps, proj_blk(j) % bps, 0)),
                  pl.BlockSpec((1, N_EARLY_MOD, d), lambda j: (proj_blk(j) // bps, 0, 0)),
                  pl.BlockSpec((tb, RET_D), lambda j: (proj_blk(j) % bps, 0)),
                  pl.BlockSpec((tb, RET_D), lambda j: (proj_blk(j) % bps, 0)),
                  pl.BlockSpec((None, SLAB + GATE_RANK, d), of_layer, **single),
                  pl.BlockSpec((None, GATE_RANK, GLA_KEY_WIDTH), of_layer, **single),
                  pl.BlockSpec((None, 1, GLA_KEY_WIDTH), of_layer, **single),
                  pl.BlockSpec(dmask.shape, const3, **single),
                  pl.BlockSpec(qdec.shape, const3, **single),
                  pl.BlockSpec(kdec.shape, const3, **single),
                  pl.BlockSpec(csum.shape, const2, **single),
                  pl.BlockSpec((None, 1, RET_WIDTH), of_layer, **single),
                  pl.BlockSpec((None, 1, GLA_WIDTH), of_layer, **single),
                  pl.BlockSpec((None, wo_rows, d), lambda j: (layer, proj_blk(j), 0)),
                  pl.BlockSpec((None, w1_rows, d_ff), lambda j: (layer, proj_blk(j), 0)),
                  pl.BlockSpec((None, w2_rows, d), lambda j: (layer, proj_blk(j), 0)),
                  pl.BlockSpec((bsz, d), const2, **single),
                  pl.BlockSpec((None, d, late_cols), lambda j: (layer, 0, late_first + proj_blk(j))),
                  pl.BlockSpec((None, 1, late_cols), lambda j: (layer, 0, late_first + proj_blk(j)))],
        out_specs=[pl.BlockSpec((1, tb, D_MIX), lambda j: (mix_blk(j) // bps, mix_blk(j) % bps, 0)),
                   pl.BlockSpec((wo_rows, d), lambda j: (proj_blk(j), 0)),
                   pl.BlockSpec((w1_rows, d_ff), lambda j: (proj_blk(j), 0)),
                   pl.BlockSpec((w2_rows, d), lambda j: (proj_blk(j), 0)),
                   pl.BlockSpec((bsz, late_cols), lambda j: (0, proj_blk(j)))],
        out_shape=[jax.ShapeDtypeStruct((bsz, s, D_MIX), BF16),
                   jax.ShapeDtypeStruct((d, d), BF16),
                   jax.ShapeDtypeStruct((d, d_ff), BF16),
                   jax.ShapeDtypeStruct((d_ff, d), BF16),
                   jax.ShapeDtypeStruct((bsz, (N_MOD - N_EARLY_MOD) * d), F32)],
        scratch_shapes=[pltpu.VMEM((d, SLAB), BF16), pltpu.VMEM((GATE_RANK, d), BF16),
                        pltpu.VMEM((2, tb, SLAB), BF16), pltpu.VMEM((2, tb, GLA_KEY_WIDTH), F32),
                        pltpu.SMEM((2,), jnp.int32),
                        pltpu.VMEM((RET_HEADS, RET_D, RET_D), F32),
                        pltpu.VMEM((GLA_HEADS, GLA_DV, LANES), F32)],
        compiler_params=pltpu.CompilerParams(
            dimension_semantics=("arbitrary",), vmem_limit_bytes=VMEM_LIMIT),
        name="projmix",
    )(x, mod, cos_t, sin_t, jnp.swapaxes(w_in, 1, 2), gate_w, _layer_row(gate_b),
      jnp.asarray(dmask), jnp.asarray(qdec), jnp.asarray(kdec), jnp.asarray(csum, BF16),
      _layer_row(ret_norm_w), _layer_row(gla_norm_w), w_out, w_ff1, w_ff2,
      c, w_ada, _layer_row(b_ada))


def _ffn_kernel(x_ref, mix_ref, mod_ref, wo_ref, w1_ref, w2_ref, ln1w_ref, ln1b_ref,
                ln2w_ref, ln2b_ref, out_ref, h_ref, *, alpha):
    seq = pl.ds(pl.program_id(0), 1)
    d = x_ref.shape[2]
    gate1 = mod_ref[seq, 0:d]
    shift2 = mod_ref[seq, d:2 * d]
    scale2 = mod_ref[seq, 2 * d:3 * d]
    gate2 = mod_ref[seq, 3 * d:4 * d]
    d_ff = w1_ref.shape[1]
    n_groups = 4
    group = d_ff // n_groups
    tb = x_ref.shape[1]
    halves = (slice(0, tb // 2), slice(tb // 2, tb))

    def chain(rows, wait):
        m = _dot(mix_ref[0, rows, :], wo_ref[...])
        yield
        x1 = (_ln(alpha * x_ref[0, rows, :] + gate1 * m, wait.get("ln1")) * ln1w_ref[...]
              + ln1b_ref[...])
        u2f = _ln(x1) * (1.0 + scale2) + shift2
        wait["ln1_done"] = u2f[u2f.shape[0] - 8:, u2f.shape[1] - LANES:]
        u2 = u2f.astype(BF16)
        yield
        for j in range(n_groups):
            hj = jnp.maximum(_dot(u2, w1_ref[:, j * group:(j + 1) * group]), 0.0)
            h_ref[rows, j * group:(j + 1) * group] = (hj * hj).astype(BF16)
            yield
        f = _dot(h_ref[rows, :], w2_ref[...])
        yield
        out_ref[0, rows, :] = _ln(alpha * x1 + gate2 * f) * ln2w_ref[...] + ln2b_ref[...]
        yield

    wait_a, wait_b = {}, {}
    half_a, half_b = chain(halves[0], wait_a), chain(halves[1], wait_b)
    next(half_a)
    next(half_b)
    next(half_a)
    n_rest = n_groups + 2
    _interleave(half_a, n_rest, half_b, n_rest + 1)


def _ffn(x, mixed, mod, w_out, w1, w2, ln1_w, ln1_b, ln2_w, ln2_b, layer, alpha, tb):
    bsz, s, d = x.shape
    nb = s // tb
    d_ff = w1.shape[1]
    const2 = lambda b, i: (0, 0)
    of_layer = lambda b, i: (layer, 0, 0)
    single = dict(pipeline_mode=pl.Buffered(1))
    return pl.pallas_call(
        functools.partial(_ffn_kernel, alpha=alpha),
        grid=(bsz, nb),
        in_specs=[pl.BlockSpec((1, tb, d), lambda b, i: (b, i, 0)),
                  pl.BlockSpec((1, tb, d), lambda b, i: (b, i, 0)),
                  pl.BlockSpec(mod.shape, const2, **single),
                  pl.BlockSpec((d, d), const2, **single),
                  pl.BlockSpec((d, d_ff), const2, **single),
                  pl.BlockSpec((d_ff, d), const2, **single),
                  pl.BlockSpec((None, 1, d), of_layer, **single),
                  pl.BlockSpec((None, 1, d), of_layer, **single),
                  pl.BlockSpec((None, 1, d), of_layer, **single),
                  pl.BlockSpec((None, 1, d), of_layer, **single)],
        out_specs=pl.BlockSpec((1, tb, d), lambda b, i: (b, i, 0)),
        out_shape=jax.ShapeDtypeStruct((bsz, s, d), x.dtype),
        scratch_shapes=[pltpu.VMEM((tb, d_ff), BF16)],
        compiler_params=pltpu.CompilerParams(
            dimension_semantics=("parallel", "parallel"), vmem_limit_bytes=VMEM_LIMIT),
        name="ffn",
    )(x, mixed, mod, w_out, w1, w2, _layer_row(ln1_w), _layer_row(ln1_b), _layer_row(ln2_w),
      _layer_row(ln2_b))


def kernel(x, c, w_ada, b_ada, w_in, ret_norm_w, gla_gate_w, gla_gate_b, gla_norm_w,
           w_out, ln1_w, ln1_b, w_ff1, w_ff2, ln2_w, ln2_b):
    depth = w_in.shape[0]
    bsz, s, d = x.shape
    alpha = (2.0 * depth) ** 0.25
    tb = min(PROJMIX_ROWS, s)
    cos_t, sin_t = (jnp.asarray(t) for t in _rotary_tables(s))
    for l in range(depth):
        mod = _adaln(c, w_ada, b_ada, l)
        mixed, wo_bf, w1_bf, w2_bf, late_mod = _projmix(
            x, mod, cos_t, sin_t, w_in, gla_gate_w, gla_gate_b, ret_norm_w, gla_norm_w,
            w_out, w_ff1, w_ff2, c, w_ada, b_ada, l, tb)
        x = _ffn(x, mixed, late_mod, wo_bf, w1_bf, w2_bf, ln1_w, ln1_b, ln2_w, ln2_b, l, alpha,
                 min(FFN_ROWS, s))
    return x
```
